```python
import math
import jax
import jax.numpy as jnp
from jax import lax
import numpy as np

D_MODEL = 1024
BATCH = 8
SEQ = 4096
DEPTH = 4

N_EVEN = (DEPTH + 1) // 2
N_ODD = DEPTH // 2
EPS = 1e-6
ROPE_BASE = 10000.0

ML_HEADS = 4
ML_DK = 128
ML_DV = 128
ML_CONV = 5
ML_CHUNK = 128

MLA_HEADS = 8
MLA_Q_LORA = 256
MLA_KV_LORA = 128
MLA_NOPE = 64
MLA_ROPE = 32
MLA_V = 64
MLA_QK = MLA_NOPE + MLA_ROPE
Q_BLOCK = 128

EVEN_IN_SIZES = (ML_HEADS * ML_DK, ML_HEADS * ML_DK, ML_HEADS * ML_DV, ML_HEADS * ML_DV, 4 * ML_HEADS, MLA_Q_LORA, MLA_KV_LORA, MLA_ROPE)
EVEN_IN = sum(EVEN_IN_SIZES)
MIX_WIDTH = ML_HEADS * ML_DV + MLA_HEADS * MLA_V

RET_HEADS = 4
RET_DK = 256
RET_DV = 512
RET_CHUNK = 128
RET_IN_SIZES = (RET_HEADS * RET_DK, RET_HEADS * RET_DK, RET_HEADS * RET_DV, RET_HEADS * RET_DV)
RET_IN = sum(RET_IN_SIZES)

N_EXPERTS = 32
TOP_K = 4
EXPERT_FF = 1024
SWIGLU_LIMIT = 7.0
SWIGLU_ALPHA = 1.702
MOE_BLOCK = 256

kernel_name = 'hybrid_mlstm_mla_retention_moe_encoder'


def _split(t, sizes):
    return jnp.split(t, np.cumsum(sizes)[:-1].tolist(), axis=-1)


def _rms_norm(x, g):
    xf = x.astype(jnp.float32)
    y = xf * lax.rsqrt(jnp.mean(xf * xf, axis=-1, keepdims=True) + EPS)
    return (y * g.astype(jnp.float32)).astype(x.dtype)


def _head_group_norm(y, g):
    mu = jnp.mean(y, axis=-1, keepdims=True)
    yc = y - mu
    var = jnp.mean(yc * yc, axis=-1, keepdims=True)
    return yc * lax.rsqrt(var + EPS) * g.astype(jnp.float32)


def _rotary(x, pos):
    d = x.shape[-1]
    inv = ROPE_BASE ** (-jnp.arange(0, d, 2, dtype=jnp.float32) / d)
    ang = pos.astype(jnp.float32)[..., None] * inv
    ang = ang.reshape(ang.shape[:2] + (1,) * (x.ndim - 3) + (d // 2,))
    cos = jnp.cos(ang).astype(x.dtype)
    sin = jnp.sin(ang).astype(x.dtype)
    x1, x2 = jnp.split(x, 2, axis=-1)
    return jnp.concatenate([x1 * cos - x2 * sin, x2 * cos + x1 * sin], axis=-1)


def _to_heads(t, n):
    b, s, _ = t.shape
    return t.reshape(b, s, n, -1).transpose(0, 2, 1, 3)


def _from_heads(t):
    b, n, s, d = t.shape
    return t.transpose(0, 2, 1, 3).reshape(b, s, n * d)


def _to_chunks(t, L):
    b, h, s = t.shape[:3]
    return jnp.moveaxis(t.reshape((b, h, s // L, L) + t.shape[3:]), 2, 0)


def _from_chunks(t):
    nc, b, h, L, d = t.shape
    return jnp.moveaxis(t, 0, 2).reshape(b, h, nc * L, d)


def _centred_dwconv(x, w, bias):
    y = lax.conv_general_dilated(x, w[:, None, :].astype(x.dtype), window_strides=(1,), padding='SAME',
                                 dimension_numbers=('NWC', 'WIO', 'NWC'), feature_group_count=x.shape[-1])
    return y + bias


def _mlstm_scan(q, k, v, log_i, log_f):
    b, h, s, dk = q.shape
    dv = v.shape[-1]
    L = ML_CHUNK
    lower = jnp.tril(jnp.ones((L, L), dtype=bool))

    def step(carry, inp):
        cmat, nvec, m = carry
        qc, kc, vc, ic, fc = inp
        bcum = jnp.cumsum(fc, axis=-1)
        log_d = jnp.where(lower, bcum[..., :, None] - bcum[..., None, :] + ic[..., None, :], -jnp.inf)
        m_inter = bcum + m[..., None]
        m_t = jnp.maximum(m_inter, jnp.max(log_d, axis=-1))
        w = jnp.einsum('bhld,bhsd->bhls', qc, kc) * jnp.exp(log_d - m_t[..., None])
        a_inter = jnp.exp(m_inter - m_t)[..., None]
        num = jnp.einsum('bhls,bhsv->bhlv', w, vc) + a_inter * jnp.einsum('bhld,bhdv->bhlv', qc, cmat)
        den = jnp.sum(w, axis=-1, keepdims=True) + a_inter * jnp.einsum('bhld,bhd->bhl', qc, nvec)[..., None]
        out = num / jnp.maximum(jnp.abs(den), jnp.exp(-m_t)[..., None])
        g = bcum[..., -1:] - bcum + ic
        m_new = jnp.maximum(bcum[..., -1] + m, jnp.max(g, axis=-1))
        a_state = jnp.exp(bcum[..., -1] + m - m_new)
        kw = kc * jnp.exp(g - m_new[..., None])[..., None]
        cmat = a_state[..., None, None] * cmat + jnp.einsum('bhsd,bhsv->bhdv', kw, vc)
        nvec = a_state[..., None] * nvec + jnp.sum(kw, axis=2)
        return (cmat, nvec, m_new), out

    ch = lambda t: _to_chunks(t.astype(jnp.float32), L)
    carry0 = (jnp.zeros((b, h, dk, dv), jnp.float32), jnp.zeros((b, h, dk), jnp.float32), jnp.zeros((b, h), jnp.float32))
    _, out = lax.scan(step, carry0, (ch(q), ch(k), ch(v), ch(log_i), ch(log_f)))
    return _from_chunks(out)


def _retention_scan(q, k, v, log_gamma):
    b, h, s, dk = q.shape
    dv = v.shape[-1]
    L = RET_CHUNK
    idx = jnp.arange(L, dtype=jnp.float32)
    rel = idx[:, None] - idx[None, :]
    lg = log_gamma.astype(jnp.float32)[:, None, None]
    decay_in = jnp.where(rel >= 0, jnp.exp(jnp.maximum(rel, 0.0) * lg), 0.0)
    decay_q = jnp.exp((idx + 1.0) * lg[:, :, 0])[:, :, None]
    decay_k = jnp.exp((L - 1.0 - idx) * lg[:, :, 0])[:, :, None]
    decay_c = jnp.exp(L * lg)

    def step(state, inp):
        qc, kc, vc = inp
        scores = jnp.einsum('bhld,bhsd->bhls', qc, kc) * decay_in
        out = jnp.einsum('bhls,bhsv->bhlv', scores, vc) + decay_q * jnp.einsum('bhld,bhdv->bhlv', qc, state)
        state = decay_c * state + jnp.einsum('bhsd,bhsv->bhdv', kc * decay_k, vc)
        return state, out

    ch = lambda t: _to_chunks(t.astype(jnp.float32), L)
    _, out = lax.scan(step, jnp.zeros((b, h, dk, dv), jnp.float32), (ch(q), ch(k), ch(v)))
    return _from_chunks(out)


def _blocked_softmax_attention(q, k, v):
    b, h, s, d = q.shape
    qb = jnp.moveaxis(q.reshape(b, h, s // Q_BLOCK, Q_BLOCK, d), 2, 0)
    scale = d ** -0.5

    def block(qi):
        sc = jnp.einsum('bhqd,bhkd->bhqk', qi, k).astype(jnp.float32) * scale
        p = jax.nn.softmax(sc, axis=-1).astype(v.dtype)
        return jnp.einsum('bhqk,bhkv->bhqv', p, v)

    o = lax.map(block, qb)
    return _from_chunks(o)


def _hybrid_mixer(h, pos, w_in, conv_w, conv_b, gate_b, ml_norm_g, q_norm_g, kv_norm_g, w_qb, w_kvb, qk_q_g, qk_k_g, w_out):
    b, s, _ = h.shape
    q_m, k_m, v_m, o_m, gates, q_a, kv_a, k_r = _split(h @ w_in, EVEN_IN_SIZES)
    qk_m = jax.nn.silu(_centred_dwconv(jnp.concatenate([q_m, k_m], axis=-1), conv_w, conv_b))
    q_m, k_m = jnp.split(qk_m, 2, axis=-1)
    qh = _to_heads(q_m, ML_HEADS)
    kh = _to_heads(k_m, ML_HEADS) * (ML_DK ** -0.5)
    vh = _to_heads(v_m, ML_HEADS)
    gates = (gates + gate_b).astype(jnp.float32).reshape(b, s, 4, ML_HEADS).transpose(2, 0, 3, 1)
    i_f, f_f, i_b, f_b = gates[0], gates[1], gates[2], gates[3]
    flip = lambda t: jnp.flip(t, axis=2)
    h_fwd = _mlstm_scan(qh, kh, vh, i_f, jax.nn.log_sigmoid(f_f))
    h_bwd = flip(_mlstm_scan(flip(qh), flip(kh), flip(vh), flip(i_b), flip(jax.nn.log_sigmoid(f_b))))
    h_ml = _rms_norm(h_fwd + h_bwd, ml_norm_g.reshape(ML_HEADS, 1, ML_DV)).astype(h.dtype)
    h_ml = _from_heads(h_ml) * jax.nn.sigmoid(o_m)
    q = (_rms_norm(q_a, q_norm_g) @ w_qb).reshape(b, s, MLA_HEADS, MLA_QK)
    kv = (_rms_norm(kv_a, kv_norm_g) @ w_kvb).reshape(b, s, MLA_HEADS, MLA_NOPE + MLA_V)
    k_nope, v = jnp.split(kv, [MLA_NOPE], axis=-1)
    k = jnp.concatenate([k_nope, jnp.broadcast_to(k_r[:, :, None, :], (b, s, MLA_HEADS, MLA_ROPE))], axis=-1)
    q = _rms_norm(q, qk_q_g)
    k = _rms_norm(k, qk_k_g)
    q = jnp.concatenate([q[..., :MLA_NOPE], _rotary(q[..., MLA_NOPE:], pos)], axis=-1)
    k = jnp.concatenate([k[..., :MLA_NOPE], _rotary(k[..., MLA_NOPE:], pos)], axis=-1)
    tr = lambda t: t.transpose(0, 2, 1, 3)
    o_mla = _from_heads(_blocked_softmax_attention(tr(q), tr(k), tr(v)))
    return jnp.concatenate([h_ml, o_mla], axis=-1) @ w_out


def _retention_mixer(h, pos, w_in, decay_f, decay_b, gn_g, w_out):
    b, s, _ = h.shape
    q, k, v, g = _split(h @ w_in, RET_IN_SIZES)
    q = _rotary(q.reshape(b, s, RET_HEADS, RET_DK), pos) * (RET_DK ** -0.5)
    k = _rotary(k.reshape(b, s, RET_HEADS, RET_DK), pos)
    tr = lambda t: t.transpose(0, 2, 1, 3)
    qh, kh, vh = tr(q), tr(k), _to_heads(v, RET_HEADS)
    flip = lambda t: jnp.flip(t, axis=2)
    lg_f = jax.nn.log_sigmoid(decay_f.astype(jnp.float32))
    lg_b = jax.nn.log_sigmoid(decay_b.astype(jnp.float32))
    y = _retention_scan(qh, kh, vh, lg_f) + flip(_retention_scan(flip(qh), flip(kh), flip(vh), lg_b))
    y = _head_group_norm(y, gn_g.reshape(RET_HEADS, 1, RET_DV)).astype(h.dtype)
    return (_from_heads(y) * jax.nn.silu(g)) @ w_out


def _clamped_swiglu(gu):
    gate, up = jnp.split(gu, 2, axis=-1)
    gate = jnp.minimum(gate, SWIGLU_LIMIT)
    up = jnp.clip(up, -SWIGLU_LIMIT, SWIGLU_LIMIT)
    return (up + 1.0) * gate * jax.nn.sigmoid(SWIGLU_ALPHA * gate)


def _moe_ffn(h, router_w, router_b, w_gu, b_gu, w_down, b_down):
    b, s, d = h.shape
    xt = h.reshape(-1, d)
    n = xt.shape[0]
    logits = (xt @ router_w + router_b).astype(jnp.float32)
    top_v, top_i = lax.top_k(logits, TOP_K)
    top_w = jax.nn.softmax(top_v, axis=-1)
    a = n * TOP_K
    e_flat = top_i.reshape(-1)
    tok_flat = jnp.arange(a, dtype=jnp.int32) // TOP_K
    w_flat = top_w.reshape(-1)
    order = jnp.argsort(e_flat)
    e_sorted, tok_sorted, w_sorted = e_flat[order], tok_flat[order], w_flat[order]
    counts = jnp.zeros((N_EXPERTS,), jnp.int32).at[e_flat].add(1)
    padded = (counts + MOE_BLOCK - 1) // MOE_BLOCK * MOE_BLOCK
    start = jnp.cumsum(counts) - counts
    pend = jnp.cumsum(padded)
    pstart = pend - padded
    dest = pstart[e_sorted] + jnp.arange(a, dtype=jnp.int32) - start[e_sorted]
    nb = -(-a // MOE_BLOCK) + N_EXPERTS
    tok_buf = jnp.full((nb * MOE_BLOCK,), n, jnp.int32).at[dest].set(tok_sorted)
    w_buf = jnp.zeros((nb * MOE_BLOCK,), jnp.float32).at[dest].set(w_sorted)
    blk_expert = jnp.clip(jnp.searchsorted(pend, jnp.arange(nb, dtype=jnp.int32) * MOE_BLOCK, side='right'), 0, N_EXPERTS - 1)
    x_pad = jnp.concatenate([xt, jnp.zeros((1, d), xt.dtype)], axis=0)

    def run_block(args):
        tok, e = args
        gu = x_pad[tok] @ w_gu[e] + b_gu[e]
        return _clamped_swiglu(gu) @ w_down[e] + b_down[e]

    y = lax.map(run_block, (tok_buf.reshape(nb, MOE_BLOCK), blk_expert))
    y = y.reshape(-1, d) * w_buf[:, None].astype(y.dtype)
    out = jnp.zeros((n + 1, d), y.dtype).at[tok_buf].add(y)[:n]
    return out.reshape(b, s, d)


def setup_inputs(seed: int = 0) -> dict:
    key = jax.random.key(seed)
    ks = iter(jax.random.split(key, 40))
    nrm = lambda shape, scale: jax.random.normal(next(ks), shape, jnp.float32) * scale
    gain = lambda shape: 1.0 + nrm(shape, 0.02)
    D = D_MODEL
    x = nrm((BATCH, SEQ, D), 1.0)
    c = nrm((BATCH, D), 1.0)
    positions = jnp.arange(SEQ, dtype=jnp.int32)[None, :] + jax.random.randint(next(ks), (BATCH, 1), 0, 256, jnp.int32)
    fb = jnp.linspace(3.0, 6.0, ML_HEADS, dtype=jnp.float32)
    ml_gate_b = jnp.concatenate([nrm((N_EVEN, ML_HEADS), 0.1), fb + nrm((N_EVEN, ML_HEADS), 0.1),
                                 nrm((N_EVEN, ML_HEADS), 0.1), fb + nrm((N_EVEN, ML_HEADS), 0.1)], axis=-1)
    gamma = 1.0 - jnp.exp(jnp.linspace(math.log(1.0 / 32.0), math.log(1.0 / 512.0), RET_HEADS, dtype=jnp.float32))
    decay_logit = jnp.log(gamma) - jnp.log1p(-gamma)
    return {
        'x': x,
        'c': c,
        'positions': positions,
        'ada_w': nrm((DEPTH, D, 6 * D), 0.5 * D ** -0.5),
        'ada_b': nrm((DEPTH, 6 * D), 0.02),
        'norm_mix_g': gain((DEPTH, D)),
        'norm_ffn_g': gain((DEPTH, D)),
        'hy_w_in': nrm((N_EVEN, D, EVEN_IN), D ** -0.5),
        'ml_conv_w': nrm((N_EVEN, ML_CONV, 2 * ML_HEADS * ML_DK), ML_CONV ** -0.5),
        'ml_conv_b': nrm((N_EVEN, 2 * ML_HEADS * ML_DK), 0.02),
        'ml_gate_b': ml_gate_b,
        'ml_norm_g': gain((N_EVEN, ML_HEADS * ML_DV)),
        'mla_q_norm_g': gain((N_EVEN, MLA_Q_LORA)),
        'mla_kv_norm_g': gain((N_EVEN, MLA_KV_LORA)),
        'mla_w_qb': nrm((N_EVEN, MLA_Q_LORA, MLA_HEADS * MLA_QK), MLA_Q_LORA ** -0.5),
        'mla_w_kvb': nrm((N_EVEN, MLA_KV_LORA, MLA_HEADS * (MLA_NOPE + MLA_V)), MLA_KV_LORA ** -0.5),
        'mla_qk_q_g': gain((N_EVEN, MLA_QK)),
        'mla_qk_k_g': gain((N_EVEN, MLA_QK)),
        'hy_w_out': nrm((N_EVEN, MIX_WIDTH, D), MIX_WIDTH ** -0.5),
        'ret_w_in': nrm((N_ODD, D, RET_IN), D ** -0.5),
        'ret_decay_f': decay_logit + nrm((N_ODD, RET_HEADS), 0.05),
        'ret_decay_b': decay_logit + nrm((N_ODD, RET_HEADS), 0.05),
        'ret_gn_g': gain((N_ODD, RET_HEADS * RET_DV)),
        'ret_w_out': nrm((N_ODD, RET_HEADS * RET_DV, D), (RET_HEADS * RET_DV) ** -0.5),
        'moe_router_w': nrm((DEPTH, D, N_EXPERTS), D ** -0.5),
        'moe_router_b': nrm((DEPTH, N_EXPERTS), 0.01),
        'moe_w_gu': nrm((DEPTH, N_EXPERTS, D, 2 * EXPERT_FF), D ** -0.5),
        'moe_b_gu': nrm((DEPTH, N_EXPERTS, 2 * EXPERT_FF), 0.01),
        'moe_w_down': nrm((DEPTH, N_EXPERTS, EXPERT_FF, D), EXPERT_FF ** -0.5),
        'moe_b_down': nrm((DEPTH, N_EXPERTS, D), 0.01),
    }


def reference(x, c, positions, ada_w, ada_b, norm_mix_g, norm_ffn_g, hy_w_in, ml_conv_w, ml_conv_b, ml_gate_b,
              ml_norm_g, mla_q_norm_g, mla_kv_norm_g, mla_w_qb, mla_w_kvb, mla_qk_q_g, mla_qk_k_g, hy_w_out,
              ret_w_in, ret_decay_f, ret_decay_b, ret_gn_g, ret_w_out, moe_router_w, moe_router_b, moe_w_gu,
              moe_b_gu, moe_w_down, moe_b_down):
    cs = jax.nn.silu(c)
    for layer in range(DEPTH):
        mod = (cs @ ada_w[layer] + ada_b[layer])[:, None, :]
        sh1, sc1, g1, sh2, sc2, g2 = jnp.split(mod, 6, axis=-1)
        hm = _rms_norm(x, norm_mix_g[layer]) * (1.0 + sc1) + sh1
        j = layer // 2
        if layer % 2 == 0:
            y = _hybrid_mixer(hm, positions, hy_w_in[j], ml_conv_w[j], ml_conv_b[j], ml_gate_b[j], ml_norm_g[j],
                              mla_q_norm_g[j], mla_kv_norm_g[j], mla_w_qb[j], mla_w_kvb[j], mla_qk_q_g[j],
                              mla_qk_k_g[j], hy_w_out[j])
        else:
            y = _retention_mixer(hm, positions, ret_w_in[j], ret_decay_f[j], ret_decay_b[j], ret_gn_g[j], ret_w_out[j])
        x = x + g1 * y.astype(x.dtype)
        hf = _rms_norm(x, norm_ffn_g[layer]) * (1.0 + sc2) + sh2
        x = x + g2 * _moe_ffn(hf, moe_router_w[layer], moe_router_b[layer], moe_w_gu[layer], moe_b_gu[layer],
                              moe_w_down[layer], moe_b_down[layer]).astype(x.dtype)
    return x
```

```python
import functools

import jax
import jax.numpy as jnp
from jax import lax
from jax.experimental import pallas as pl
from jax.experimental.pallas import tpu as pltpu

F32 = jnp.float32
BF16 = jnp.bfloat16
HIGHEST = lax.Precision.HIGHEST

D = 1024
DEPTH = 4
EPS = 1e-6
ROPE_BASE = 10000.0

ML_HEADS = 4
ML_DK = 128
ML_DV = 128
ML_CONV = 5
ML_CHUNK = 128

MLA_HEADS = 8
MLA_Q_LORA = 256
MLA_KV_LORA = 128
MLA_NOPE = 64
MLA_ROPE = 32
MLA_V = 64
MLA_QK = MLA_NOPE + MLA_ROPE
MLA_PAD = 128

RET_HEADS = 4
RET_DK = 256
RET_DV = 512
RET_CHUNK = 128

N_EXPERTS = 32
TOP_K = 4
EXPERT_FF = 1024
SWIGLU_LIMIT = 7.0
SWIGLU_ALPHA = 1.702
MOE_BLOCK = 256

LANES = 128
NEG_BIG = -1e30

ROW_TILE = 512
ATTN_Q_TILE = 256
ROUTE_TILE = 256
MOVE_TILE = 256
VMEM_LIMIT = 56 * 1024 * 1024


def _params(semantics, **kw):
    return pltpu.CompilerParams(dimension_semantics=semantics, vmem_limit_bytes=VMEM_LIMIT, **kw)


def _dot(a, b):
    return jnp.dot(a, b, preferred_element_type=F32)


def _dot_nt(a, b):
    return lax.dot_general(a, b, (((1,), (1,)), ((), ())), preferred_element_type=F32)


def _dot_tn(a, b):
    return lax.dot_general(a, b, (((0,), (0,)), ((), ())), preferred_element_type=F32)


def _log_sigmoid(x):
    return jnp.minimum(x, 0.0) - jnp.log1p(jnp.exp(-jnp.abs(x)))


def _rms(x, denom=None):
    n = x.shape[-1] if denom is None else denom
    return x * lax.rsqrt(jnp.sum(x * x, axis=-1, keepdims=True) / n + EPS)


def _norm_mod(x, g, sc, sh):
    return (_rms(x) * g) * (1.0 + sc) + sh


def _ada_kernel(c_ref, w_ref, b_ref, o_ref):
    c = c_ref[...]
    cs = c * jax.nn.sigmoid(c)
    o_ref[0] = jnp.dot(cs, w_ref[0], precision=HIGHEST, preferred_element_type=F32) + b_ref[0]


def _ada(c, ada_w, ada_b):
    b = c.shape[0]
    return pl.pallas_call(
        _ada_kernel,
        out_shape=jax.ShapeDtypeStruct((DEPTH, b, 6 * D), F32),
        grid=(DEPTH, 6),
        in_specs=[
            pl.BlockSpec((b, D), lambda l, j: (0, 0)),
            pl.BlockSpec((1, D, D), lambda l, j: (l, 0, j)),
            pl.BlockSpec((1, 1, D), lambda l, j: (l, 0, j)),
        ],
        out_specs=pl.BlockSpec((1, b, D), lambda l, j: (l, 0, j)),
        compiler_params=_params(("parallel", "parallel")),
        name="ada",
    )(c, ada_w, ada_b.reshape(DEPTH, 1, 6 * D))


def _rope_kernel(pos_ref, invr_ref, invm_ref, cr_ref, sr_ref, cm_ref, sa_ref, sb_ref):
    pos = pos_ref[...].astype(F32)
    ang_r = pos * invr_ref[...]
    cr_ref[...] = jnp.cos(ang_r)
    sr_ref[...] = jnp.sin(ang_r)
    ang_m = pos * invm_ref[...]
    lane = lax.broadcasted_iota(jnp.int32, ang_m.shape, 1)
    half = MLA_ROPE // 2
    s = jnp.sin(ang_m)
    cm_ref[...] = jnp.cos(ang_m)
    sa_ref[...] = jnp.where((lane >= MLA_NOPE) & (lane < MLA_NOPE + half), -s, 0.0)
    sb_ref[...] = jnp.where((lane >= MLA_NOPE + half) & (lane < MLA_QK), s, 0.0)


def _rope_tables(positions):
    n = positions.size
    inv_r = ROPE_BASE ** (-jnp.arange(0, RET_DK, 2, dtype=F32) / RET_DK)
    inv_m = ROPE_BASE ** (-jnp.arange(0, MLA_ROPE, 2, dtype=F32) / MLA_ROPE)
    half = MLA_ROPE // 2
    invm = jnp.zeros((LANES,), F32).at[MLA_NOPE:MLA_NOPE + half].set(inv_m).at[MLA_NOPE + half:MLA_QK].set(inv_m)
    t = ROW_TILE
    tab = jax.ShapeDtypeStruct((n, LANES), F32)
    row = pl.BlockSpec((t, LANES), lambda i: (i, 0))
    vec = pl.BlockSpec((1, LANES), lambda i: (0, 0))
    return pl.pallas_call(
        _rope_kernel,
        out_shape=(tab,) * 5,
        grid=(n // t,),
        in_specs=[pl.BlockSpec((t, 1), lambda i: (i, 0)), vec, vec],
        out_specs=(row,) * 5,
        compiler_params=_params(("parallel",)),
        name="rope",
    )(positions.reshape(n, 1), inv_r.reshape(1, LANES), invm.reshape(1, LANES))


EVEN_COLS = 4 * 512 + MLA_Q_LORA + 3 * LANES


def _in_even_kernel(x_ref, g_ref, sc_ref, sh_ref, w_ref, qng_ref, kvng_ref, wqb_ref, wkn_ref, wv_ref,
                    qg_ref, kg_ref, cm_ref, sa_ref, sb_ref,
                    qk_ref, v_ref, o_ref, misc_ref, q_out, k_out, v_out):
    h = _norm_mod(x_ref[...], g_ref[...], sc_ref[0], sh_ref[0]).astype(BF16)
    qk_ref[...] = _dot(h, w_ref[:, 0:1024])
    v_ref[...] = _dot(h, w_ref[:, 1024:1536]).astype(BF16)
    o_ref[...] = _dot(h, w_ref[:, 1536:2048])
    rest = _dot(h, w_ref[:, 2048:EVEN_COLS])
    q_a = rest[:, 0:256]
    kv_a = rest[:, 256:384]
    krp = rest[:, 384:512]
    misc_ref[...] = rest[:, 512:640]
    qn = (_rms(q_a) * qng_ref[...]).astype(BF16)
    kvn = (_rms(kv_a) * kvng_ref[...]).astype(BF16)
    q = _dot(qn, wqb_ref[...])
    kn = _dot(kvn, wkn_ref[...])
    v_out[...] = _dot(kvn, wv_ref[...]).astype(BF16)
    cm, sa, sb = cm_ref[...], sa_ref[...], sb_ref[...]
    half = MLA_ROPE // 2

    def rot(t):
        return t * cm + pltpu.roll(t, LANES - half, 1) * sa + pltpu.roll(t, half, 1) * sb

    for hd in range(MLA_HEADS):
        sl = slice(hd * MLA_PAD, (hd + 1) * MLA_PAD)
        qh = _rms(q[:, sl], MLA_QK) * qg_ref[...]
        q_out[:, sl] = rot(qh).astype(BF16)
        kh = _rms(kn[:, sl] + krp, MLA_QK) * kg_ref[...]
        k_out[:, sl] = rot(kh).astype(BF16)


def _in_even(x, mod, b_of, norm_g, w, qng, kvng, wqb, wkn, wv, qg, kg, cm, sa, sb):
    n = x.shape[0]
    t = ROW_TILE
    full = lambda shape: pl.BlockSpec(shape, lambda i: (0,) * len(shape))
    row = lambda c: pl.BlockSpec((t, c), lambda i: (i, 0))
    out_shape = (
        jax.ShapeDtypeStruct((n, 1024), F32),
        jax.ShapeDtypeStruct((n, 512), BF16),
        jax.ShapeDtypeStruct((n, 512), F32),
        jax.ShapeDtypeStruct((n, LANES), F32),
        jax.ShapeDtypeStruct((n, 1024), BF16),
        jax.ShapeDtypeStruct((n, 1024), BF16),
        jax.ShapeDtypeStruct((n, 512), BF16),
    )
    return pl.pallas_call(
        _in_even_kernel,
        out_shape=out_shape,
        grid=(n // t,),
        in_specs=[
            row(D), full((1, D)),
            pl.BlockSpec((1, 1, D), lambda i: (b_of(i, t) * 6 + 1, 0, 0)),
            pl.BlockSpec((1, 1, D), lambda i: (b_of(i, t) * 6 + 0, 0, 0)),
            full((D, EVEN_COLS)), full((1, 256)), full((1, 128)), full((256, 1024)), full((128, 1024)),
            full((128, 512)), full((1, 128)), full((1, 128)), row(LANES), row(LANES), row(LANES),
        ],
        out_specs=(row(1024), row(512), row(512), row(LANES), row(1024), row(1024), row(512)),
        compiler_params=_params(("parallel",)),
        name="in_even",
    )(x, norm_g, mod, mod, w, qng, kvng, wqb, wkn, wv, qg, kg, cm, sa, sb)


def _conv_kernel(seq_tiles, prev_ref, x_ref, next_ref, w_ref, b_ref, s_ref, o_ref, buf):
    i = pl.program_id(0)
    t = x_ref.shape[0]
    first = (i % seq_tiles) == 0
    last = (i % seq_tiles) == seq_tiles - 1
    buf[0:8, :] = jnp.where(first, 0.0, prev_ref[...])
    buf[8:8 + t, :] = x_ref[...]
    buf[8 + t:16 + t, :] = jnp.where(last, 0.0, next_ref[...])
    pad = ML_CONV // 2
    acc = b_ref[...] + w_ref[0:1, :] * buf[8 - pad:8 - pad + t, :]
    for k in range(1, ML_CONV):
        acc = acc + w_ref[k:k + 1, :] * buf[8 - pad + k:8 - pad + k + t, :]
    o_ref[...] = (acc * jax.nn.sigmoid(acc) * s_ref[...]).astype(BF16)


def _conv(qk_pre, conv_w, conv_b, seq):
    n, c = qk_pre.shape
    t = ROW_TILE
    nb8 = n // 8
    scale = jnp.concatenate([jnp.ones((512,), F32), jnp.full((512,), ML_DK ** -0.5, F32)]).reshape(1, c)
    return pl.pallas_call(
        functools.partial(_conv_kernel, seq // t),
        out_shape=jax.ShapeDtypeStruct((n, c), BF16),
        grid=(n // t,),
        in_specs=[
            pl.BlockSpec((8, c), lambda i: (jnp.maximum(i * (t // 8) - 1, 0), 0)),
            pl.BlockSpec((t, c), lambda i: (i, 0)),
            pl.BlockSpec((8, c), lambda i: (jnp.minimum((i + 1) * (t // 8), nb8 - 1), 0)),
            pl.BlockSpec((ML_CONV, c), lambda i: (0, 0)),
            pl.BlockSpec((1, c), lambda i: (0, 0)),
            pl.BlockSpec((1, c), lambda i: (0, 0)),
        ],
        out_specs=pl.BlockSpec((t, c), lambda i: (i, 0)),
        scratch_shapes=[pltpu.VMEM((t + 16, c), F32)],
        compiler_params=_params(("parallel",)),
        name="conv",
    )(qk_pre, qk_pre, qk_pre, conv_w, conv_b.reshape(1, c), scale)


def _mlstm_kernel(q_ref, k_ref, v_ref, gc_ref, gr_ref, bc_ref, br_ref, o_ref, c_ref, m_ref):
    d = pl.program_id(0)
    c = pl.program_id(2)
    L = ML_CHUNK

    @pl.when(c == 0)
    def _():
        c_ref[...] = jnp.zeros_like(c_ref)
        m_ref[...] = jnp.zeros_like(m_ref)

    fwd = d == 0
    row = lax.broadcasted_iota(jnp.int32, (L, L), 0)
    col = lax.broadcasted_iota(jnp.int32, (L, L), 1)
    rel = jnp.where(fwd, row - col, col - row)
    causal = rel >= 0
    causal_t = rel <= 0
    gc = gc_ref[0] + bc_ref[0]
    gr = gr_ref[0] + br_ref[0]
    i_col = gc[:, 0:ML_HEADS]
    f_col = _log_sigmoid(gc[:, ML_HEADS:2 * ML_HEADS])
    i_row = gr[0:ML_HEADS, :]
    f_row = _log_sigmoid(gr[ML_HEADS:2 * ML_HEADS, :])
    bcum_col = jnp.dot(causal.astype(F32), f_col, precision=HIGHEST, preferred_element_type=F32)
    bcum_row = jnp.dot(f_row, causal_t.astype(F32), precision=HIGHEST, preferred_element_type=F32)
    lane = lax.broadcasted_iota(jnp.int32, (L, ML_DV), 1)
    ones_col = jnp.where(lane == 0, 1.0, 0.0).astype(BF16)

    for h in range(ML_HEADS):
        sl = slice(h * ML_DK, (h + 1) * ML_DK)
        q = q_ref[:, sl]
        k = k_ref[:, sl]
        vext = jnp.concatenate([v_ref[:, sl], ones_col], axis=1)
        bc = bcum_col[:, h:h + 1]
        br = bcum_row[h:h + 1, :]
        ir = i_row[h:h + 1, :]
        ic = i_col[:, h:h + 1]
        m_prev = m_ref[h:h + 1, 0:1]
        log_d = jnp.where(causal, bc - br + ir, -jnp.inf)
        m_inter = bc + m_prev
        m_t = jnp.maximum(m_inter, jnp.max(log_d, axis=-1, keepdims=True))
        w = (_dot_nt(q, k) * jnp.exp(log_d - m_t)).astype(BF16)
        a_inter = jnp.exp(m_inter - m_t)
        tot = _dot(w, vext) + a_inter * _dot(q, c_ref[h].astype(BF16))
        den = tot[:, ML_DV:ML_DV + 1]
        o_ref[0, :, sl] = tot[:, 0:ML_DV] / jnp.maximum(jnp.abs(den), jnp.exp(-m_t))
        total = jnp.sum(f_row[h:h + 1, :], axis=-1, keepdims=True)
        g_row = total - br + ir
        m_new = jnp.maximum(total + m_prev, jnp.max(g_row, axis=-1, keepdims=True))
        a_state = jnp.exp(total + m_prev - m_new)
        kw = (k.astype(F32) * jnp.exp(total - bc + ic - m_new)).astype(BF16)
        c_ref[h] = a_state * c_ref[h] + _dot_tn(kw, vext)
        m_ref[h:h + 1, :] = jnp.broadcast_to(m_new, (1, LANES))


def _mlstm(qk, v, gcol, grow, bcol, brow, batch, seq):
    n = qk.shape[0]
    L = ML_CHUNK
    nc = seq // L

    def rb(d, b, c):
        return b * nc + c + d * (nc - 1 - 2 * c)

    return pl.pallas_call(
        _mlstm_kernel,
        out_shape=jax.ShapeDtypeStruct((2, n, ML_HEADS * ML_DV), F32),
        grid=(2, batch, nc),
        in_specs=[
            pl.BlockSpec((L, 512), lambda d, b, c: (rb(d, b, c), 0)),
            pl.BlockSpec((L, 512), lambda d, b, c: (rb(d, b, c), 1)),
            pl.BlockSpec((L, 512), lambda d, b, c: (rb(d, b, c), 0)),
            pl.BlockSpec((1, L, 8), lambda d, b, c: (d, rb(d, b, c), 0)),
            pl.BlockSpec((1, 8, L), lambda d, b, c: (d, 0, rb(d, b, c))),
            pl.BlockSpec((1, 1, 8), lambda d, b, c: (d, 0, 0)),
            pl.BlockSpec((1, 8, 1), lambda d, b, c: (d, 0, 0)),
        ],
        out_specs=pl.BlockSpec((1, L, 512), lambda d, b, c: (d, rb(d, b, c), 0)),
        scratch_shapes=[pltpu.VMEM((ML_HEADS, ML_DK, 2 * ML_DV), F32), pltpu.VMEM((8, LANES), F32)],
        compiler_params=_params(("parallel", "parallel", "arbitrary")),
        name="mlstm",
    )(qk, qk, v, gcol, grow, bcol, brow)


def _attn_kernel(q_ref, k_ref, v_ref, o_ref):
    outs = []
    for j in range(2):
        q = q_ref[:, j * MLA_PAD:(j + 1) * MLA_PAD]
        k = k_ref[:, j * MLA_PAD:(j + 1) * MLA_PAD]
        v = v_ref[:, j * MLA_V:(j + 1) * MLA_V]
        s = _dot_nt(q, k)
        p = jnp.exp(s - jnp.max(s, axis=-1, keepdims=True))
        l = jnp.sum(p, axis=-1, keepdims=True)
        outs.append(_dot(p.astype(BF16), v) / l)
    o_ref[...] = jnp.concatenate(outs, axis=1).astype(BF16)


def _attn(q, k, v, batch, seq):
    n = q.shape[0]
    t = ATTN_Q_TILE
    nq = seq // t
    return pl.pallas_call(
        _attn_kernel,
        out_shape=jax.ShapeDtypeStruct((n, MLA_HEADS * MLA_V), BF16),
        grid=(batch, MLA_HEADS // 2, nq),
        in_specs=[
            pl.BlockSpec((t, 2 * MLA_PAD), lambda b, h, i: (b * nq + i, h)),
            pl.BlockSpec((seq, 2 * MLA_PAD), lambda b, h, i: (b, h)),
            pl.BlockSpec((seq, 2 * MLA_V), lambda b, h, i: (b, h)),
        ],
        out_specs=pl.BlockSpec((t, 2 * MLA_V), lambda b, h, i: (b * nq + i, h)),
        compiler_params=_params(("parallel", "parallel", "arbitrary")),
        name="attn",
    )(q, k, v)


def _mixer_tail(y, x_ref, g1_ref, fg_ref, sc2_ref, sh2_ref, rw_ref, rb_ref, x_out, hf_out, lg_out):
    x1 = x_ref[...] + g1_ref[0] * y
    x_out[...] = x1
    hf = _norm_mod(x1, fg_ref[...], sc2_ref[0], sh2_ref[0])
    hf_out[...] = hf
    lg_out[...] = jnp.dot(hf, rw_ref[...], precision=HIGHEST, preferred_element_type=F32) + rb_ref[...]


def _tail_specs(n, t, b_of):
    full = lambda shape: pl.BlockSpec(shape, lambda i: (0,) * len(shape))
    mod = lambda j: pl.BlockSpec((1, 1, D), lambda i: (b_of(i, t) * 6 + j, 0, 0))
    row = lambda c: pl.BlockSpec((t, c), lambda i: (i, 0))
    in_specs = [row(D), mod(2), full((1, D)), mod(4), mod(3), full((D, LANES)), full((1, LANES))]
    out_shape = (jax.ShapeDtypeStruct((n, D), F32), jax.ShapeDtypeStruct((n, D), F32),
                 jax.ShapeDtypeStruct((n, LANES), F32))
    out_specs = (row(D), row(D), row(LANES))
    return in_specs, out_shape, out_specs


def _out_even_kernel(hd_ref, om_ref, oa_ref, mg_ref, wt_ref, wb_ref, *tail):
    hs = hd_ref[0] + hd_ref[1]
    gate = jax.nn.sigmoid(om_ref[...])
    parts = []
    for h in range(ML_HEADS):
        sl = slice(h * ML_DV, (h + 1) * ML_DV)
        parts.append(_rms(hs[:, sl]) * mg_ref[:, sl] * gate[:, sl])
    hml = jnp.concatenate(parts, axis=1).astype(BF16)
    y = _dot(hml, wt_ref[...]) + _dot(oa_ref[...], wb_ref[...])
    _mixer_tail(y, *tail)


def _out_even(hdir, o_m, o_mla, ml_g, w_top, w_bot, x, mod, b_of, ffn_g, rw, rb):
    n = x.shape[0]
    t = ROW_TILE
    full = lambda shape: pl.BlockSpec(shape, lambda i: (0,) * len(shape))
    row = lambda c: pl.BlockSpec((t, c), lambda i: (i, 0))
    tin, out_shape, out_specs = _tail_specs(n, t, b_of)
    return pl.pallas_call(
        _out_even_kernel,
        out_shape=out_shape,
        grid=(n // t,),
        in_specs=[pl.BlockSpec((2, t, 512), lambda i: (0, i, 0)), row(512), row(512), full((1, 512)),
                  full((512, D)), full((512, D))] + tin,
        out_specs=out_specs,
        compiler_params=_params(("parallel",)),
        name="out_even",
    )(hdir, o_m, o_mla, ml_g, w_top, w_bot, x, mod, ffn_g, mod, mod, rw, rb)


def _in_odd_kernel(x_ref, g_ref, sc_ref, sh_ref, w_ref, cr_ref, sr_ref, q_out, k_out, v_out, g_out):
    h = _norm_mod(x_ref[...], g_ref[...], sc_ref[0], sh_ref[0]).astype(BF16)
    cos, sin = cr_ref[...], sr_ref[...]
    hw = RET_DK // 2
    for idx, (dst, scale) in enumerate(((q_out, RET_DK ** -0.5), (k_out, 1.0))):
        z = _dot(h, w_ref[:, idx * 1024:(idx + 1) * 1024])
        for hd in range(RET_HEADS):
            x1 = z[:, hd * RET_DK:hd * RET_DK + hw]
            x2 = z[:, hd * RET_DK + hw:(hd + 1) * RET_DK]
            dst[:, hd * RET_DK:hd * RET_DK + hw] = ((x1 * cos - x2 * sin) * scale).astype(BF16)
            dst[:, hd * RET_DK + hw:(hd + 1) * RET_DK] = ((x2 * cos + x1 * sin) * scale).astype(BF16)
    v_out[...] = _dot(h, w_ref[:, 2048:4096]).astype(BF16)
    g_out[...] = _dot(h, w_ref[:, 4096:6144])


def _in_odd(x, mod, b_of, norm_g, w, cr, sr):
    n = x.shape[0]
    t = ROW_TILE
    full = lambda shape: pl.BlockSpec(shape, lambda i: (0,) * len(shape))
    row = lambda c: pl.BlockSpec((t, c), lambda i: (i, 0))
    out_shape = (
        jax.ShapeDtypeStruct((n, 1024), BF16), jax.ShapeDtypeStruct((n, 1024), BF16),
        jax.ShapeDtypeStruct((n, 2048), BF16), jax.ShapeDtypeStruct((n, 2048), F32),
    )
    return pl.pallas_call(
        _in_odd_kernel,
        out_shape=out_shape,
        grid=(n // t,),
        in_specs=[
            row(D), full((1, D)),
            pl.BlockSpec((1, 1, D), lambda i: (b_of(i, t) * 6 + 1, 0, 0)),
            pl.BlockSpec((1, 1, D), lambda i: (b_of(i, t) * 6 + 0, 0, 0)),
            full((D, 6144)), row(LANES), row(LANES),
        ],
        out_specs=(row(1024), row(1024), row(2048), row(2048)),
        compiler_params=_params(("parallel",)),
        name="in_odd",
    )(x, norm_g, mod, mod, w, cr, sr)


def _ret_kernel(q_ref, k_ref, v_ref, dec_ref, o_ref, s_ref):
    d = pl.program_id(0)
    c = pl.program_id(2)
    L = RET_CHUNK

    @pl.when(c == 0)
    def _():
        s_ref[...] = jnp.zeros_like(s_ref)

    fwd = d == 0
    row = lax.broadcasted_iota(jnp.int32, (L, L), 0)
    col = lax.broadcasted_iota(jnp.int32, (L, L), 1)
    rel = jnp.where(fwd, row - col, col - row).astype(F32)
    pos = lax.broadcasted_iota(jnp.int32, (L, 1), 0)
    exp_q = jnp.where(fwd, pos + 1, L - pos).astype(F32)
    exp_k = jnp.where(fwd, L - 1 - pos, pos).astype(F32)
    for h in range(RET_HEADS):
        lg = _log_sigmoid(dec_ref[0, h:h + 1, :])
        lg1 = lg[:, 0:1]
        decay_in = jnp.where(rel >= 0, jnp.exp(jnp.maximum(rel, 0.0) * lg), 0.0)
        q = q_ref[:, h * RET_DK:(h + 1) * RET_DK]
        k = k_ref[:, h * RET_DK:(h + 1) * RET_DK]
        v = v_ref[:, h * RET_DV:(h + 1) * RET_DV]
        sc = (_dot_nt(q, k) * decay_in).astype(BF16)
        o_ref[0, :, h * RET_DV:(h + 1) * RET_DV] = (
            _dot(sc, v) + jnp.exp(exp_q * lg1) * _dot(q, s_ref[h].astype(BF16)))
        kd = (k.astype(F32) * jnp.exp(exp_k * lg1)).astype(BF16)
        s_ref[h] = jnp.exp(L * lg1) * s_ref[h] + _dot_tn(kd, v)


def _retention(q, k, v, dec, batch, seq):
    n = q.shape[0]
    L = RET_CHUNK
    nc = seq // L

    def rb(d, b, c):
        return b * nc + c + d * (nc - 1 - 2 * c)

    return pl.pallas_call(
        _ret_kernel,
        out_shape=jax.ShapeDtypeStruct((2, n, RET_HEADS * RET_DV), F32),
        grid=(2, batch, nc),
        in_specs=[
            pl.BlockSpec((L, 1024), lambda d, b, c: (rb(d, b, c), 0)),
            pl.BlockSpec((L, 1024), lambda d, b, c: (rb(d, b, c), 0)),
            pl.BlockSpec((L, 2048), lambda d, b, c: (rb(d, b, c), 0)),
            pl.BlockSpec((1, RET_HEADS, LANES), lambda d, b, c: (d, 0, 0)),
        ],
        out_specs=pl.BlockSpec((1, L, 2048), lambda d, b, c: (d, rb(d, b, c), 0)),
        scratch_shapes=[pltpu.VMEM((RET_HEADS, RET_DK, RET_DV), F32)],
        compiler_params=_params(("parallel", "parallel", "arbitrary")),
        name="retention",
    )(q, k, v, dec)


def _out_odd_kernel(yd_ref, g_ref, gn_ref, w_ref, *tail):
    ys = yd_ref[0] + yd_ref[1]
    g = g_ref[...]
    gate = g * jax.nn.sigmoid(g)
    parts = []
    for h in range(RET_HEADS):
        sl = slice(h * RET_DV, (h + 1) * RET_DV)
        yh = ys[:, sl]
        yc = yh - jnp.mean(yh, axis=-1, keepdims=True)
        var = jnp.mean(yc * yc, axis=-1, keepdims=True)
        parts.append(yc * lax.rsqrt(var + EPS) * gn_ref[:, sl] * gate[:, sl])
    yn = jnp.concatenate(parts, axis=1).astype(BF16)
    _mixer_tail(_dot(yn, w_ref[...]), *tail)


def _out_odd(ydir, g, gn_g, w_out, x, mod, b_of, ffn_g, rw, rb):
    n = x.shape[0]
    t = ROW_TILE
    full = lambda shape: pl.BlockSpec(shape, lambda i: (0,) * len(shape))
    row = lambda c: pl.BlockSpec((t, c), lambda i: (i, 0))
    tin, out_shape, out_specs = _tail_specs(n, t, b_of)
    return pl.pallas_call(
        _out_odd_kernel,
        out_shape=out_shape,
        grid=(n // t,),
        in_specs=[pl.BlockSpec((2, t, 2048), lambda i: (0, i, 0)), row(2048), full((1, 2048)),
                  full((2048, D))] + tin,
        out_specs=out_specs,
        compiler_params=_params(("parallel",)),
        name="out_odd",
    )(ydir, g, gn_g, w_out, x, mod, ffn_g, mod, mod, rw, rb)


def _route_kernel(lg_ref, e_out, r_out, w_out, cnt_out, base_ref):
    i = pl.program_id(0)
    t = lg_ref.shape[0]

    @pl.when(i == 0)
    def _():
        base_ref[...] = jnp.zeros_like(base_ref)

    l = lg_ref[...]
    lane = lax.broadcasted_iota(jnp.int32, l.shape, 1)
    sel = jnp.zeros(l.shape, F32)
    vals, idxs, hots = [], [], []
    for _k in range(TOP_K):
        m = jnp.max(l, axis=-1, keepdims=True)
        idx = jnp.min(jnp.where(l == m, lane, LANES), axis=-1, keepdims=True)
        hot = lane == idx
        vals.append(m)
        idxs.append(idx)
        hots.append(hot)
        sel = sel + jnp.where(hot, 1.0, 0.0)
        l = jnp.where(hot, -jnp.inf, l)
    es = [jnp.exp(v - vals[0]) for v in vals]
    den = es[0] + es[1] + es[2] + es[3]
    row = lax.broadcasted_iota(jnp.int32, (t, t), 0)
    col = lax.broadcasted_iota(jnp.int32, (t, t), 1)
    before = jnp.where(row > col, 1.0, 0.0).astype(BF16)
    rank_mat = _dot(before, sel.astype(BF16)) + base_ref[...]
    e_acc = jnp.zeros(l.shape, jnp.int32)
    r_acc = jnp.zeros(l.shape, jnp.int32)
    w_acc = jnp.zeros(l.shape, F32)
    for k in range(TOP_K):
        rk = jnp.sum(jnp.where(hots[k], rank_mat, 0.0), axis=-1, keepdims=True).astype(jnp.int32)
        e_acc = jnp.where(lane == k, idxs[k], e_acc)
        r_acc = jnp.where(lane == k, rk, r_acc)
        w_acc = jnp.where(lane == k, es[k] / den, w_acc)
    e_out[...] = e_acc
    r_out[...] = r_acc
    w_out[...] = w_acc
    base_ref[...] = base_ref[...] + jnp.sum(sel, axis=0, keepdims=True)
    cnt_out[...] = base_ref[...]


def _route(logits):
    n = logits.shape[0]
    t = ROUTE_TILE
    row = pl.BlockSpec((t, LANES), lambda i: (i, 0))
    return pl.pallas_call(
        _route_kernel,
        out_shape=(jax.ShapeDtypeStruct((n, LANES), jnp.int32), jax.ShapeDtypeStruct((n, LANES), jnp.int32),
                   jax.ShapeDtypeStruct((n, LANES), F32), jax.ShapeDtypeStruct((1, LANES), F32)),
        grid=(n // t,),
        in_specs=[row],
        out_specs=(row, row, row, pl.BlockSpec((1, LANES), lambda i: (0, 0))),
        scratch_shapes=[pltpu.VMEM((1, LANES), F32)],
        compiler_params=_params(("arbitrary",)),
        name="route",
    )(logits)


def _dispatch_kernel(dest_ref, hf_ref, zero_ref, xs_ref, sem):
    del zero_ref
    t = hf_ref.shape[0]

    def issue(a, carry):
        tok = a // TOP_K
        pltpu.make_async_copy(hf_ref.at[pl.ds(tok, 1)], xs_ref.at[pl.ds(dest_ref[0, 0, a], 1)], sem).start()
        return carry

    lax.fori_loop(0, t * TOP_K, issue, 0)

    def drain(a, carry):
        pltpu.make_async_copy(hf_ref.at[pl.ds(0, 1)], xs_ref.at[pl.ds(0, 1)], sem).wait()
        return carry

    lax.fori_loop(0, t * TOP_K, drain, 0)


def _dispatch(hf, dest, rows):
    n = hf.shape[0]
    t = MOVE_TILE
    zeros = jnp.zeros((rows, D), F32)
    return pl.pallas_call(
        _dispatch_kernel,
        out_shape=jax.ShapeDtypeStruct((rows, D), F32),
        grid=(n // t,),
        in_specs=[
            pl.BlockSpec((1, 1, t * TOP_K), lambda i: (i, 0, 0), memory_space=pltpu.SMEM),
            pl.BlockSpec((t, D), lambda i: (i, 0)),
            pl.BlockSpec(memory_space=pl.ANY),
        ],
        out_specs=pl.BlockSpec(memory_space=pl.ANY),
        scratch_shapes=[pltpu.SemaphoreType.DMA(())],
        input_output_aliases={2: 0},
        compiler_params=_params(("arbitrary",), has_side_effects=True),
        name="dispatch",
    )(dest.reshape(n // t, 1, t * TOP_K), hf, zeros)


def _expert_kernel(be_ref, nu_ref, xs_ref, wgu_ref, bgu_ref, wd_ref, bd_ref, ys_ref, wgu_bf, wd_bf):
    i = pl.program_id(0)
    fresh = jnp.logical_or(i == 0, be_ref[i] != be_ref[jnp.maximum(i - 1, 0)])

    @pl.when(fresh)
    def _():
        wgu_bf[...] = wgu_ref[0].astype(BF16)
        wd_bf[...] = wd_ref[0].astype(BF16)

    @pl.when(i < nu_ref[0])
    def _():
        gu = _dot(xs_ref[...].astype(BF16), wgu_bf[...]) + bgu_ref[0]
        gate = jnp.minimum(gu[:, :EXPERT_FF], SWIGLU_LIMIT)
        up = jnp.clip(gu[:, EXPERT_FF:], -SWIGLU_LIMIT, SWIGLU_LIMIT)
        act = (up + 1.0) * gate * jax.nn.sigmoid(SWIGLU_ALPHA * gate)
        ys_ref[...] = _dot(act.astype(BF16), wd_bf[...]) + bd_ref[0]

    @pl.when(i >= nu_ref[0])
    def _():
        ys_ref[...] = jnp.zeros_like(ys_ref)


def _experts(xs, blk_expert, n_used, w_gu, b_gu, w_down, b_down):
    rows = xs.shape[0]
    nb = rows // MOE_BLOCK
    grid_spec = pltpu.PrefetchScalarGridSpec(
        num_scalar_prefetch=2,
        grid=(nb,),
        in_specs=[
            pl.BlockSpec((MOE_BLOCK, D), lambda i, be, nu: (i, 0)),
            pl.BlockSpec((1, D, 2 * EXPERT_FF), lambda i, be, nu: (be[i], 0, 0)),
            pl.BlockSpec((1, 1, 2 * EXPERT_FF), lambda i, be, nu: (be[i], 0, 0)),
            pl.BlockSpec((1, EXPERT_FF, D), lambda i, be, nu: (be[i], 0, 0)),
            pl.BlockSpec((1, 1, D), lambda i, be, nu: (be[i], 0, 0)),
        ],
        out_specs=pl.BlockSpec((MOE_BLOCK, D), lambda i, be, nu: (i, 0)),
        scratch_shapes=[pltpu.VMEM((D, 2 * EXPERT_FF), BF16), pltpu.VMEM((EXPERT_FF, D), BF16)],
    )
    return pl.pallas_call(
        _expert_kernel,
        out_shape=jax.ShapeDtypeStruct((rows, D), F32),
        grid_spec=grid_spec,
        compiler_params=_params(("arbitrary",)),
        name="experts",
    )(blk_expert, n_used, xs, w_gu, b_gu.reshape(N_EXPERTS, 1, 2 * EXPERT_FF), w_down,
      b_down.reshape(N_EXPERTS, 1, D))


def _combine_kernel(dest_ref, ys_ref, wt_ref, x_ref, g2_ref, o_ref, buf, sem):
    t = x_ref.shape[0]

    def issue(a, carry):
        tok = a // TOP_K
        k = a % TOP_K
        pltpu.make_async_copy(ys_ref.at[pl.ds(dest_ref[0, 0, a], 1)], buf.at[k, pl.ds(tok, 1)], sem).start()
        return carry

    lax.fori_loop(0, t * TOP_K, issue, 0)

    def drain(a, carry):
        pltpu.make_async_copy(ys_ref.at[pl.ds(0, 1)], buf.at[0, pl.ds(0, 1)], sem).wait()
        return carry

    lax.fori_loop(0, t * TOP_K, drain, 0)
    wt = wt_ref[...]
    acc = buf[0] * wt[:, 0:1]
    for k in range(1, TOP_K):
        acc = acc + buf[k] * wt[:, k:k + 1]
    o_ref[...] = x_ref[...] + g2_ref[0] * acc


def _combine(ys, dest, wts, x, mod, b_of):
    n = x.shape[0]
    t = MOVE_TILE
    return pl.pallas_call(
        _combine_kernel,
        out_shape=jax.ShapeDtypeStruct((n, D), F32),
        grid=(n // t,),
        in_specs=[
            pl.BlockSpec((1, 1, t * TOP_K), lambda i: (i, 0, 0), memory_space=pltpu.SMEM),
            pl.BlockSpec(memory_space=pl.ANY),
            pl.BlockSpec((t, LANES), lambda i: (i, 0)),
            pl.BlockSpec((t, D), lambda i: (i, 0)),
            pl.BlockSpec((1, 1, D), lambda i: (b_of(i, t) * 6 + 5, 0, 0)),
        ],
        out_specs=pl.BlockSpec((t, D), lambda i: (i, 0)),
        scratch_shapes=[pltpu.VMEM((TOP_K, t, D), F32), pltpu.SemaphoreType.DMA(())],
        compiler_params=_params(("arbitrary",)),
        name="combine",
    )(dest.reshape(n // t, 1, t * TOP_K), ys, wts, x, mod)


def _moe(x1, hf, logits, mod, b_of, w_gu, b_gu, w_down, b_down):
    n = x1.shape[0]
    eidx, rank, wts, counts = _route(logits)
    cnt = counts[0, :N_EXPERTS].astype(jnp.int32)
    padded = (cnt + MOE_BLOCK - 1) // MOE_BLOCK * MOE_BLOCK
    pend = jnp.cumsum(padded)
    pstart = pend - padded
    dest = pstart[eidx[:, :TOP_K]] + rank[:, :TOP_K]
    nb = -(-(n * TOP_K) // MOE_BLOCK) + N_EXPERTS
    blk_expert = jnp.clip(jnp.searchsorted(pend, jnp.arange(nb, dtype=jnp.int32) * MOE_BLOCK, side='right'),
                          0, N_EXPERTS - 1).astype(jnp.int32)
    n_used = (pend[-1:] // MOE_BLOCK).astype(jnp.int32)
    xs = _dispatch(hf, dest, nb * MOE_BLOCK)
    ys = _experts(xs, blk_expert, n_used, w_gu, b_gu, w_down, b_down)
    return _combine(ys, dest, wts, x1, mod, b_of)


def _prep_even(w_in, gate_b, w_qb, w_kvb, qk_q_g, qk_k_g, w_out):
    q_m, k_m, v_m, o_m, gates, q_a, kv_a, k_r = jnp.split(
        w_in, [512, 1024, 1536, 2048, 2064, 2320, 2448], axis=1)
    zeros = lambda c: jnp.zeros((D, c), F32)
    krp = jnp.concatenate([zeros(MLA_NOPE), k_r, zeros(MLA_PAD - MLA_QK)], axis=1)
    misc = jnp.concatenate([gates, zeros(LANES - 16)], axis=1)
    w = jnp.concatenate([q_m, k_m, v_m, o_m, q_a, kv_a, krp, misc], axis=1).astype(BF16)
    wqb = jnp.pad(w_qb.reshape(MLA_Q_LORA, MLA_HEADS, MLA_QK), ((0, 0), (0, 0), (0, MLA_PAD - MLA_QK)))
    wqb = wqb.reshape(MLA_Q_LORA, MLA_HEADS * MLA_PAD).astype(BF16)
    kvb = w_kvb.reshape(MLA_KV_LORA, MLA_HEADS, MLA_NOPE + MLA_V)
    wkn = jnp.pad(kvb[:, :, :MLA_NOPE], ((0, 0), (0, 0), (0, MLA_PAD - MLA_NOPE)))
    wkn = wkn.reshape(MLA_KV_LORA, MLA_HEADS * MLA_PAD).astype(BF16)
    wv = kvb[:, :, MLA_NOPE:].reshape(MLA_KV_LORA, MLA_HEADS * MLA_V).astype(BF16)
    qg = jnp.pad(qk_q_g * (MLA_QK ** -0.5), (0, MLA_PAD - MLA_QK)).reshape(1, MLA_PAD)
    kg = jnp.pad(qk_k_g, (0, MLA_PAD - MLA_QK)).reshape(1, MLA_PAD)
    gb = gate_b.reshape(2, 8)
    w_top = w_out[:512].astype(BF16)
    w_bot = w_out[512:].astype(BF16)
    return w, wqb, wkn, wv, qg, kg, gb.reshape(2, 1, 8), gb.reshape(2, 8, 1), w_top, w_bot


def kernel(x, c, positions, ada_w, ada_b, norm_mix_g, norm_ffn_g, hy_w_in, ml_conv_w, ml_conv_b, ml_gate_b, ml_norm_g, mla_q_norm_g, mla_kv_norm_g, mla_w_qb, mla_w_kvb, mla_qk_q_g, mla_qk_k_g, hy_w_out, ret_w_in, ret_decay_f, ret_decay_b, ret_gn_g, ret_w_out, moe_router_w, moe_router_b, moe_w_gu, moe_b_gu, moe_w_down, moe_b_down):
    batch, seq, _ = x.shape
    n = batch * seq

    def b_of(i, t):
        return (i * t) // seq

    mod_all = _ada(c, ada_w, ada_b).reshape(DEPTH, batch * 6, 1, D)
    cr, sr, cm, sa, sb = _rope_tables(positions)
    xf = x.reshape(n, D)
    for layer in range(DEPTH):
        mod = mod_all[layer]
        j = layer // 2
        mix_g = norm_mix_g[layer].reshape(1, D)
        ffn_g = norm_ffn_g[layer].reshape(1, D)
        rw = jnp.pad(moe_router_w[layer], ((0, 0), (0, LANES - N_EXPERTS)))
        rb = jnp.concatenate([moe_router_b[layer], jnp.full((LANES - N_EXPERTS,), NEG_BIG, F32)]).reshape(1, LANES)
        if layer % 2 == 0:
            w, wqb, wkn, wv, qg, kg, gbc, gbr, w_top, w_bot = _prep_even(
                hy_w_in[j], ml_gate_b[j], mla_w_qb[j], mla_w_kvb[j], mla_qk_q_g[j], mla_qk_k_g[j], hy_w_out[j])
            qk_pre, v_m, o_m, misc, q_a, k_a, v_a = _in_even(
                xf, mod, b_of, mix_g, w, mla_q_norm_g[j].reshape(1, -1), mla_kv_norm_g[j].reshape(1, -1),
                wqb, wkn, wv, qg, kg, cm, sa, sb)
            qk = _conv(qk_pre, ml_conv_w[j], ml_conv_b[j], seq)
            gcol = misc[:, :16].reshape(n, 2, 8).transpose(1, 0, 2)
            grow = gcol.transpose(0, 2, 1)
            hdir = _mlstm(qk, v_m, gcol, grow, gbc, gbr, batch, seq)
            o_mla = _attn(q_a, k_a, v_a, batch, seq)
            x1, hf, logits = _out_even(hdir, o_m, o_mla, ml_norm_g[j].reshape(1, -1), w_top, w_bot,
                                       xf, mod, b_of, ffn_g, rw, rb)
        else:
            dec = jnp.broadcast_to(jnp.stack([ret_decay_f[j], ret_decay_b[j]])[:, :, None], (2, RET_HEADS, LANES))
            q_r, k_r, v_r, g_r = _in_odd(xf, mod, b_of, mix_g, ret_w_in[j].astype(BF16), cr, sr)
            ydir = _retention(q_r, k_r, v_r, dec.astype(F32), batch, seq)
            x1, hf, logits = _out_odd(ydir, g_r, ret_gn_g[j].reshape(1, -1), ret_w_out[j].astype(BF16),
                                      xf, mod, b_of, ffn_g, rw, rb)
        xf = _moe(x1, hf, logits, mod, b_of, moe_w_gu[layer], moe_b_gu[layer], moe_w_down[layer],
                  moe_b_down[layer])
    return xf.reshape(batch, seq, D)
```

```python
import functools

import jax
import jax.numpy as jnp
from jax import lax
from jax.experimental import pallas as pl
from jax.experimental.pallas import tpu as pltpu

F32 = jnp.float32
BF16 = jnp.bfloat16
HIGHEST = lax.Precision.HIGHEST

D = 1024
DEPTH = 4
EPS = 1e-6
ROPE_BASE = 10000.0

ML_HEADS = 4
ML_DK = 128
ML_DV = 128
ML_CONV = 5
ML_CHUNK = 128

MLA_HEADS = 8
MLA_Q_LORA = 256
MLA_KV_LORA = 128
MLA_NOPE = 64
MLA_ROPE = 32
MLA_V = 64
MLA_QK = MLA_NOPE + MLA_ROPE
MLA_PAD = 128

RET_HEADS = 4
RET_DK = 256
RET_DV = 512
RET_CHUNK = 128

N_EXPERTS = 32
TOP_K = 4
EXPERT_FF = 1024
SWIGLU_LIMIT = 7.0
SWIGLU_ALPHA = 1.702
MOE_BLOCK = 512

LANES = 128
NEG_BIG = -1e30

ROW_TILE = 512
ATTN_Q_TILE = 256
ROUTE_TILE = 256
MOVE_TILE = 512
MOVE_UNROLL = 8
VMEM_LIMIT = 56 * 1024 * 1024


def _params(semantics, **kw):
    return pltpu.CompilerParams(dimension_semantics=semantics, vmem_limit_bytes=VMEM_LIMIT, **kw)


def _dot(a, b):
    return jnp.dot(a, b, preferred_element_type=F32)


def _dot_nt(a, b):
    return lax.dot_general(a, b, (((1,), (1,)), ((), ())), preferred_element_type=F32)


def _dot_tn(a, b):
    return lax.dot_general(a, b, (((0,), (0,)), ((), ())), preferred_element_type=F32)


def _log_sigmoid(x):
    return jnp.minimum(x, 0.0) - jnp.log1p(jnp.exp(-jnp.abs(x)))


def _rms(x, denom=None):
    n = x.shape[-1] if denom is None else denom
    return x * lax.rsqrt(jnp.sum(x * x, axis=-1, keepdims=True) / n + EPS)


def _norm_mod(x, g, sc, sh):
    return (_rms(x) * g) * (1.0 + sc) + sh


def _ada_kernel(c_ref, w_ref, b_ref, o_ref):
    c = c_ref[...]
    cs = c * jax.nn.sigmoid(c)
    o_ref[0] = jnp.dot(cs, w_ref[0], precision=HIGHEST, preferred_element_type=F32) + b_ref[0]


def _ada(c, ada_w, ada_b):
    b = c.shape[0]
    return pl.pallas_call(
        _ada_kernel,
        out_shape=jax.ShapeDtypeStruct((DEPTH, b, 6 * D), F32),
        grid=(DEPTH, 6),
        in_specs=[
            pl.BlockSpec((b, D), lambda l, j: (0, 0)),
            pl.BlockSpec((1, D, D), lambda l, j: (l, 0, j)),
            pl.BlockSpec((1, 1, D), lambda l, j: (l, 0, j)),
        ],
        out_specs=pl.BlockSpec((1, b, D), lambda l, j: (l, 0, j)),
        compiler_params=_params(("parallel", "parallel")),
        name="ada",
    )(c, ada_w, ada_b.reshape(DEPTH, 1, 6 * D))


def _rope_kernel(pos_ref, invr_ref, invm_ref, cr_ref, sr_ref, cm_ref, sa_ref, sb_ref):
    pos = pos_ref[...].astype(F32)
    ang_r = pos * invr_ref[...]
    cr_ref[...] = jnp.cos(ang_r)
    sr_ref[...] = jnp.sin(ang_r)
    ang_m = pos * invm_ref[...]
    lane = lax.broadcasted_iota(jnp.int32, ang_m.shape, 1)
    half = MLA_ROPE // 2
    s = jnp.sin(ang_m)
    cm_ref[...] = jnp.cos(ang_m)
    sa_ref[...] = jnp.where((lane >= MLA_NOPE) & (lane < MLA_NOPE + half), -s, 0.0)
    sb_ref[...] = jnp.where((lane >= MLA_NOPE + half) & (lane < MLA_QK), s, 0.0)


def _rope_tables(positions):
    n = positions.size
    inv_r = ROPE_BASE ** (-jnp.arange(0, RET_DK, 2, dtype=F32) / RET_DK)
    inv_m = ROPE_BASE ** (-jnp.arange(0, MLA_ROPE, 2, dtype=F32) / MLA_ROPE)
    half = MLA_ROPE // 2
    invm = jnp.zeros((LANES,), F32).at[MLA_NOPE:MLA_NOPE + half].set(inv_m).at[MLA_NOPE + half:MLA_QK].set(inv_m)
    t = ROW_TILE
    tab = jax.ShapeDtypeStruct((n, LANES), F32)
    row = pl.BlockSpec((t, LANES), lambda i: (i, 0))
    vec = pl.BlockSpec((1, LANES), lambda i: (0, 0))
    return pl.pallas_call(
        _rope_kernel,
        out_shape=(tab,) * 5,
        grid=(n // t,),
        in_specs=[pl.BlockSpec((t, 1), lambda i: (i, 0)), vec, vec],
        out_specs=(row,) * 5,
        compiler_params=_params(("parallel",)),
        name="rope",
    )(positions.reshape(n, 1), inv_r.reshape(1, LANES), invm.reshape(1, LANES))


EVEN_COLS = 4 * 512 + MLA_Q_LORA + 3 * LANES


def _in_even_kernel(x_ref, g_ref, sc_ref, sh_ref, w_ref, qng_ref, kvng_ref, wqb_ref, wkn_ref, wv_ref,
                    qg_ref, kg_ref, cm_ref, sa_ref, sb_ref,
                    qk_ref, v_ref, o_ref, misc_ref, q_out, k_out, v_out):
    h = _norm_mod(x_ref[...], g_ref[...], sc_ref[0], sh_ref[0]).astype(BF16)
    qk_ref[...] = _dot(h, w_ref[:, 0:1024])
    v_ref[...] = _dot(h, w_ref[:, 1024:1536]).astype(BF16)
    o_ref[...] = _dot(h, w_ref[:, 1536:2048])
    rest = _dot(h, w_ref[:, 2048:EVEN_COLS])
    q_a = rest[:, 0:256]
    kv_a = rest[:, 256:384]
    krp = rest[:, 384:512]
    misc_ref[...] = rest[:, 512:640]
    qn = (_rms(q_a) * qng_ref[...]).astype(BF16)
    kvn = (_rms(kv_a) * kvng_ref[...]).astype(BF16)
    q = _dot(qn, wqb_ref[...])
    kn = _dot(kvn, wkn_ref[...])
    v_out[...] = _dot(kvn, wv_ref[...]).astype(BF16)
    cm, sa, sb = cm_ref[...], sa_ref[...], sb_ref[...]
    half = MLA_ROPE // 2

    def rot(t):
        return t * cm + pltpu.roll(t, LANES - half, 1) * sa + pltpu.roll(t, half, 1) * sb

    for hd in range(MLA_HEADS):
        sl = slice(hd * MLA_PAD, (hd + 1) * MLA_PAD)
        qh = _rms(q[:, sl], MLA_QK) * qg_ref[...]
        q_out[:, sl] = rot(qh).astype(BF16)
        kh = _rms(kn[:, sl] + krp, MLA_QK) * kg_ref[...]
        k_out[:, sl] = rot(kh).astype(BF16)


def _in_even(x, mod, b_of, norm_g, w, qng, kvng, wqb, wkn, wv, qg, kg, cm, sa, sb):
    n = x.shape[0]
    t = ROW_TILE
    full = lambda shape: pl.BlockSpec(shape, lambda i: (0,) * len(shape))
    row = lambda c: pl.BlockSpec((t, c), lambda i: (i, 0))
    out_shape = (
        jax.ShapeDtypeStruct((n, 1024), F32),
        jax.ShapeDtypeStruct((n, 512), BF16),
        jax.ShapeDtypeStruct((n, 512), F32),
        jax.ShapeDtypeStruct((n, LANES), F32),
        jax.ShapeDtypeStruct((n, 1024), BF16),
        jax.ShapeDtypeStruct((n, 1024), BF16),
        jax.ShapeDtypeStruct((n, 512), BF16),
    )
    return pl.pallas_call(
        _in_even_kernel,
        out_shape=out_shape,
        grid=(n // t,),
        in_specs=[
            row(D), full((1, D)),
            pl.BlockSpec((1, 1, D), lambda i: (b_of(i, t) * 6 + 1, 0, 0)),
            pl.BlockSpec((1, 1, D), lambda i: (b_of(i, t) * 6 + 0, 0, 0)),
            full((D, EVEN_COLS)), full((1, 256)), full((1, 128)), full((256, 1024)), full((128, 1024)),
            full((128, 512)), full((1, 128)), full((1, 128)), row(LANES), row(LANES), row(LANES),
        ],
        out_specs=(row(1024), row(512), row(512), row(LANES), row(1024), row(1024), row(512)),
        compiler_params=_params(("parallel",)),
        name="in_even",
    )(x, norm_g, mod, mod, w, qng, kvng, wqb, wkn, wv, qg, kg, cm, sa, sb)


def _conv_kernel(seq_tiles, prev_ref, x_ref, next_ref, w_ref, b_ref, s_ref, o_ref, buf):
    i = pl.program_id(0)
    t = x_ref.shape[0]
    first = (i % seq_tiles) == 0
    last = (i % seq_tiles) == seq_tiles - 1
    buf[0:8, :] = jnp.where(first, 0.0, prev_ref[...])
    buf[8:8 + t, :] = x_ref[...]
    buf[8 + t:16 + t, :] = jnp.where(last, 0.0, next_ref[...])
    pad = ML_CONV // 2
    acc = b_ref[...] + w_ref[0:1, :] * buf[8 - pad:8 - pad + t, :]
    for k in range(1, ML_CONV):
        acc = acc + w_ref[k:k + 1, :] * buf[8 - pad + k:8 - pad + k + t, :]
    o_ref[...] = (acc * jax.nn.sigmoid(acc) * s_ref[...]).astype(BF16)


def _conv(qk_pre, conv_w, conv_b, seq):
    n, c = qk_pre.shape
    t = ROW_TILE
    nb8 = n // 8
    scale = jnp.concatenate([jnp.ones((512,), F32), jnp.full((512,), ML_DK ** -0.5, F32)]).reshape(1, c)
    return pl.pallas_call(
        functools.partial(_conv_kernel, seq // t),
        out_shape=jax.ShapeDtypeStruct((n, c), BF16),
        grid=(n // t,),
        in_specs=[
            pl.BlockSpec((8, c), lambda i: (jnp.maximum(i * (t // 8) - 1, 0), 0)),
            pl.BlockSpec((t, c), lambda i: (i, 0)),
            pl.BlockSpec((8, c), lambda i: (jnp.minimum((i + 1) * (t // 8), nb8 - 1), 0)),
            pl.BlockSpec((ML_CONV, c), lambda i: (0, 0)),
            pl.BlockSpec((1, c), lambda i: (0, 0)),
            pl.BlockSpec((1, c), lambda i: (0, 0)),
        ],
        out_specs=pl.BlockSpec((t, c), lambda i: (i, 0)),
        scratch_shapes=[pltpu.VMEM((t + 16, c), F32)],
        compiler_params=_params(("parallel",)),
        name="conv",
    )(qk_pre, qk_pre, qk_pre, conv_w, conv_b.reshape(1, c), scale)


def _mlstm_kernel(q_ref, k_ref, v_ref, gc_ref, gr_ref, bc_ref, br_ref, o_ref, c_ref, m_ref):
    d = pl.program_id(0)
    c = pl.program_id(2)
    L = ML_CHUNK

    @pl.when(c == 0)
    def _():
        c_ref[...] = jnp.zeros_like(c_ref)
        m_ref[...] = jnp.zeros_like(m_ref)

    fwd = d == 0
    row = lax.broadcasted_iota(jnp.int32, (L, L), 0)
    col = lax.broadcasted_iota(jnp.int32, (L, L), 1)
    rel = jnp.where(fwd, row - col, col - row)
    causal = rel >= 0
    causal_t = rel <= 0
    gc = gc_ref[0] + bc_ref[0]
    gr = gr_ref[0] + br_ref[0]
    i_col = gc[:, 0:ML_HEADS]
    f_col = _log_sigmoid(gc[:, ML_HEADS:2 * ML_HEADS])
    i_row = gr[0:ML_HEADS, :]
    f_row = _log_sigmoid(gr[ML_HEADS:2 * ML_HEADS, :])
    bcum_col = jnp.dot(causal.astype(F32), f_col, precision=HIGHEST, preferred_element_type=F32)
    bcum_row = jnp.dot(f_row, causal_t.astype(F32), precision=HIGHEST, preferred_element_type=F32)
    lane = lax.broadcasted_iota(jnp.int32, (L, ML_DV), 1)
    ones_col = jnp.where(lane == 0, 1.0, 0.0).astype(BF16)

    for h in range(ML_HEADS):
        sl = slice(h * ML_DK, (h + 1) * ML_DK)
        q = q_ref[:, sl]
        k = k_ref[:, sl]
        vext = jnp.concatenate([v_ref[:, sl], ones_col], axis=1)
        bc = bcum_col[:, h:h + 1]
        br = bcum_row[h:h + 1, :]
        ir = i_row[h:h + 1, :]
        ic = i_col[:, h:h + 1]
        m_prev = m_ref[h:h + 1, 0:1]
        log_d = jnp.where(causal, bc - br + ir, -jnp.inf)
        m_inter = bc + m_prev
        m_t = jnp.maximum(m_inter, jnp.max(log_d, axis=-1, keepdims=True))
        w = (_dot_nt(q, k) * jnp.exp(log_d - m_t)).astype(BF16)
        a_inter = jnp.exp(m_inter - m_t)
        tot = _dot(w, vext) + a_inter * _dot(q, c_ref[h].astype(BF16))
        den = tot[:, ML_DV:ML_DV + 1]
        o_ref[0, :, sl] = tot[:, 0:ML_DV] / jnp.maximum(jnp.abs(den), jnp.exp(-m_t))
        total = jnp.sum(f_row[h:h + 1, :], axis=-1, keepdims=True)
        g_row = total - br + ir
        m_new = jnp.maximum(total + m_prev, jnp.max(g_row, axis=-1, keepdims=True))
        a_state = jnp.exp(total + m_prev - m_new)
        kw = (k.astype(F32) * jnp.exp(total - bc + ic - m_new)).astype(BF16)
        c_ref[h] = a_state * c_ref[h] + _dot_tn(kw, vext)
        m_ref[h:h + 1, :] = jnp.broadcast_to(m_new, (1, LANES))


def _mlstm(qk, v, gcol, grow, bcol, brow, batch, seq):
    n = qk.shape[0]
    L = ML_CHUNK
    nc = seq // L

    def rb(d, b, c):
        return b * nc + c + d * (nc - 1 - 2 * c)

    return pl.pallas_call(
        _mlstm_kernel,
        out_shape=jax.ShapeDtypeStruct((2, n, ML_HEADS * ML_DV), F32),
        grid=(2, batch, nc),
        in_specs=[
            pl.BlockSpec((L, 512), lambda d, b, c: (rb(d, b, c), 0)),
            pl.BlockSpec((L, 512), lambda d, b, c: (rb(d, b, c), 1)),
            pl.BlockSpec((L, 512), lambda d, b, c: (rb(d, b, c), 0)),
            pl.BlockSpec((1, L, 8), lambda d, b, c: (d, rb(d, b, c), 0)),
            pl.BlockSpec((1, 8, L), lambda d, b, c: (d, 0, rb(d, b, c))),
            pl.BlockSpec((1, 1, 8), lambda d, b, c: (d, 0, 0)),
            pl.BlockSpec((1, 8, 1), lambda d, b, c: (d, 0, 0)),
        ],
        out_specs=pl.BlockSpec((1, L, 512), lambda d, b, c: (d, rb(d, b, c), 0)),
        scratch_shapes=[pltpu.VMEM((ML_HEADS, ML_DK, 2 * ML_DV), F32), pltpu.VMEM((8, LANES), F32)],
        compiler_params=_params(("parallel", "parallel", "arbitrary")),
        name="mlstm",
    )(qk, qk, v, gcol, grow, bcol, brow)


def _attn_kernel(q_ref, k_ref, v_ref, o_ref):
    outs = []
    for j in range(2):
        q = q_ref[:, j * MLA_PAD:(j + 1) * MLA_PAD]
        k = k_ref[:, j * MLA_PAD:(j + 1) * MLA_PAD]
        v = v_ref[:, j * MLA_V:(j + 1) * MLA_V]
        s = _dot_nt(q, k)
        p = jnp.exp(s - jnp.max(s, axis=-1, keepdims=True))
        l = jnp.sum(p, axis=-1, keepdims=True)
        outs.append(_dot(p.astype(BF16), v) / l)
    o_ref[...] = jnp.concatenate(outs, axis=1).astype(BF16)


def _attn(q, k, v, batch, seq):
    n = q.shape[0]
    t = ATTN_Q_TILE
    nq = seq // t
    return pl.pallas_call(
        _attn_kernel,
        out_shape=jax.ShapeDtypeStruct((n, MLA_HEADS * MLA_V), BF16),
        grid=(batch, MLA_HEADS // 2, nq),
        in_specs=[
            pl.BlockSpec((t, 2 * MLA_PAD), lambda b, h, i: (b * nq + i, h)),
            pl.BlockSpec((seq, 2 * MLA_PAD), lambda b, h, i: (b, h)),
            pl.BlockSpec((seq, 2 * MLA_V), lambda b, h, i: (b, h)),
        ],
        out_specs=pl.BlockSpec((t, 2 * MLA_V), lambda b, h, i: (b * nq + i, h)),
        compiler_params=_params(("parallel", "parallel", "arbitrary")),
        name="attn",
    )(q, k, v)


def _mixer_tail(y, x_ref, g1_ref, fg_ref, sc2_ref, sh2_ref, rw_ref, rb_ref, x_out, hf_out, lg_out):
    x1 = x_ref[...] + g1_ref[0] * y
    x_out[...] = x1
    hf = _norm_mod(x1, fg_ref[...], sc2_ref[0], sh2_ref[0])
    hf_out[...] = hf
    lg_out[...] = jnp.dot(hf, rw_ref[...], precision=HIGHEST, preferred_element_type=F32) + rb_ref[...]


def _tail_specs(n, t, b_of):
    full = lambda shape: pl.BlockSpec(shape, lambda i: (0,) * len(shape))
    mod = lambda j: pl.BlockSpec((1, 1, D), lambda i: (b_of(i, t) * 6 + j, 0, 0))
    row = lambda c: pl.BlockSpec((t, c), lambda i: (i, 0))
    in_specs = [row(D), mod(2), full((1, D)), mod(4), mod(3), full((D, LANES)), full((1, LANES))]
    out_shape = (jax.ShapeDtypeStruct((n, D), F32), jax.ShapeDtypeStruct((n, D), F32),
                 jax.ShapeDtypeStruct((n, LANES), F32))
    out_specs = (row(D), row(D), row(LANES))
    return in_specs, out_shape, out_specs


def _out_even_kernel(hd_ref, om_ref, oa_ref, mg_ref, wt_ref, wb_ref, *tail):
    hs = hd_ref[0] + hd_ref[1]
    gate = jax.nn.sigmoid(om_ref[...])
    parts = []
    for h in range(ML_HEADS):
        sl = slice(h * ML_DV, (h + 1) * ML_DV)
        parts.append(_rms(hs[:, sl]) * mg_ref[:, sl] * gate[:, sl])
    hml = jnp.concatenate(parts, axis=1).astype(BF16)
    y = _dot(hml, wt_ref[...]) + _dot(oa_ref[...], wb_ref[...])
    _mixer_tail(y, *tail)


def _out_even(hdir, o_m, o_mla, ml_g, w_top, w_bot, x, mod, b_of, ffn_g, rw, rb):
    n = x.shape[0]
    t = ROW_TILE
    full = lambda shape: pl.BlockSpec(shape, lambda i: (0,) * len(shape))
    row = lambda c: pl.BlockSpec((t, c), lambda i: (i, 0))
    tin, out_shape, out_specs = _tail_specs(n, t, b_of)
    return pl.pallas_call(
        _out_even_kernel,
        out_shape=out_shape,
        grid=(n // t,),
        in_specs=[pl.BlockSpec((2, t, 512), lambda i: (0, i, 0)), row(512), row(512), full((1, 512)),
                  full((512, D)), full((512, D))] + tin,
        out_specs=out_specs,
        compiler_params=_params(("parallel",)),
        name="out_even",
    )(hdir, o_m, o_mla, ml_g, w_top, w_bot, x, mod, ffn_g, mod, mod, rw, rb)


def _in_odd_kernel(x_ref, g_ref, sc_ref, sh_ref, w_ref, cr_ref, sr_ref, q_out, k_out, v_out, g_out):
    h = _norm_mod(x_ref[...], g_ref[...], sc_ref[0], sh_ref[0]).astype(BF16)
    cos, sin = cr_ref[...], sr_ref[...]
    hw = RET_DK // 2
    for idx, (dst, scale) in enumerate(((q_out, RET_DK ** -0.5), (k_out, 1.0))):
        z = _dot(h, w_ref[:, idx * 1024:(idx + 1) * 1024])
        for hd in range(RET_HEADS):
            x1 = z[:, hd * RET_DK:hd * RET_DK + hw]
            x2 = z[:, hd * RET_DK + hw:(hd + 1) * RET_DK]
            dst[:, hd * RET_DK:hd * RET_DK + hw] = ((x1 * cos - x2 * sin) * scale).astype(BF16)
            dst[:, hd * RET_DK + hw:(hd + 1) * RET_DK] = ((x2 * cos + x1 * sin) * scale).astype(BF16)
    v_out[...] = _dot(h, w_ref[:, 2048:4096]).astype(BF16)
    g_out[...] = _dot(h, w_ref[:, 4096:6144])


def _in_odd(x, mod, b_of, norm_g, w, cr, sr):
    n = x.shape[0]
    t = ROW_TILE
    full = lambda shape: pl.BlockSpec(shape, lambda i: (0,) * len(shape))
    row = lambda c: pl.BlockSpec((t, c), lambda i: (i, 0))
    out_shape = (
        jax.ShapeDtypeStruct((n, 1024), BF16), jax.ShapeDtypeStruct((n, 1024), BF16),
        jax.ShapeDtypeStruct((n, 2048), BF16), jax.ShapeDtypeStruct((n, 2048), F32),
    )
    return pl.pallas_call(
        _in_odd_kernel,
        out_shape=out_shape,
        grid=(n // t,),
        in_specs=[
            row(D), full((1, D)),
            pl.BlockSpec((1, 1, D), lambda i: (b_of(i, t) * 6 + 1, 0, 0)),
            pl.BlockSpec((1, 1, D), lambda i: (b_of(i, t) * 6 + 0, 0, 0)),
            full((D, 6144)), row(LANES), row(LANES),
        ],
        out_specs=(row(1024), row(1024), row(2048), row(2048)),
        compiler_params=_params(("parallel",)),
        name="in_odd",
    )(x, norm_g, mod, mod, w, cr, sr)


def _ret_kernel(q_ref, k_ref, v_ref, dec_ref, o_ref, s_ref):
    d = pl.program_id(0)
    c = pl.program_id(2)
    L = RET_CHUNK

    @pl.when(c == 0)
    def _():
        s_ref[...] = jnp.zeros_like(s_ref)

    fwd = d == 0
    row = lax.broadcasted_iota(jnp.int32, (L, L), 0)
    col = lax.broadcasted_iota(jnp.int32, (L, L), 1)
    rel = jnp.where(fwd, row - col, col - row).astype(F32)
    pos = lax.broadcasted_iota(jnp.int32, (L, 1), 0)
    exp_q = jnp.where(fwd, pos + 1, L - pos).astype(F32)
    exp_k = jnp.where(fwd, L - 1 - pos, pos).astype(F32)
    for h in range(RET_HEADS):
        lg = _log_sigmoid(dec_ref[0, h:h + 1, :])
        lg1 = lg[:, 0:1]
        decay_in = jnp.where(rel >= 0, jnp.exp(jnp.maximum(rel, 0.0) * lg), 0.0)
        q = q_ref[:, h * RET_DK:(h + 1) * RET_DK]
        k = k_ref[:, h * RET_DK:(h + 1) * RET_DK]
        v = v_ref[:, h * RET_DV:(h + 1) * RET_DV]
        sc = (_dot_nt(q, k) * decay_in).astype(BF16)
        o_ref[0, :, h * RET_DV:(h + 1) * RET_DV] = (
            _dot(sc, v) + jnp.exp(exp_q * lg1) * _dot(q, s_ref[h].astype(BF16)))
        kd = (k.astype(F32) * jnp.exp(exp_k * lg1)).astype(BF16)
        s_ref[h] = jnp.exp(L * lg1) * s_ref[h] + _dot_tn(kd, v)


def _retention(q, k, v, dec, batch, seq):
    n = q.shape[0]
    L = RET_CHUNK
    nc = seq // L

    def rb(d, b, c):
        return b * nc + c + d * (nc - 1 - 2 * c)

    return pl.pallas_call(
        _ret_kernel,
        out_shape=jax.ShapeDtypeStruct((2, n, RET_HEADS * RET_DV), F32),
        grid=(2, batch, nc),
        in_specs=[
            pl.BlockSpec((L, 1024), lambda d, b, c: (rb(d, b, c), 0)),
            pl.BlockSpec((L, 1024), lambda d, b, c: (rb(d, b, c), 0)),
            pl.BlockSpec((L, 2048), lambda d, b, c: (rb(d, b, c), 0)),
            pl.BlockSpec((1, RET_HEADS, LANES), lambda d, b, c: (d, 0, 0)),
        ],
        out_specs=pl.BlockSpec((1, L, 2048), lambda d, b, c: (d, rb(d, b, c), 0)),
        scratch_shapes=[pltpu.VMEM((RET_HEADS, RET_DK, RET_DV), F32)],
        compiler_params=_params(("parallel", "parallel", "arbitrary")),
        name="retention",
    )(q, k, v, dec)


def _out_odd_kernel(yd_ref, g_ref, gn_ref, w_ref, *tail):
    ys = yd_ref[0] + yd_ref[1]
    g = g_ref[...]
    gate = g * jax.nn.sigmoid(g)
    parts = []
    for h in range(RET_HEADS):
        sl = slice(h * RET_DV, (h + 1) * RET_DV)
        yh = ys[:, sl]
        yc = yh - jnp.mean(yh, axis=-1, keepdims=True)
        var = jnp.mean(yc * yc, axis=-1, keepdims=True)
        parts.append(yc * lax.rsqrt(var + EPS) * gn_ref[:, sl] * gate[:, sl])
    yn = jnp.concatenate(parts, axis=1).astype(BF16)
    _mixer_tail(_dot(yn, w_ref[...]), *tail)


def _out_odd(ydir, g, gn_g, w_out, x, mod, b_of, ffn_g, rw, rb):
    n = x.shape[0]
    t = ROW_TILE
    full = lambda shape: pl.BlockSpec(shape, lambda i: (0,) * len(shape))
    row = lambda c: pl.BlockSpec((t, c), lambda i: (i, 0))
    tin, out_shape, out_specs = _tail_specs(n, t, b_of)
    return pl.pallas_call(
        _out_odd_kernel,
        out_shape=out_shape,
        grid=(n // t,),
        in_specs=[pl.BlockSpec((2, t, 2048), lambda i: (0, i, 0)), row(2048), full((1, 2048)),
                  full((2048, D))] + tin,
        out_specs=out_specs,
        compiler_params=_params(("parallel",)),
        name="out_odd",
    )(ydir, g, gn_g, w_out, x, mod, ffn_g, mod, mod, rw, rb)


def _route_kernel(lg_ref, e_out, r_out, w_out, cnt_out, base_ref):
    i = pl.program_id(0)
    t = lg_ref.shape[0]

    @pl.when(i == 0)
    def _():
        base_ref[...] = jnp.zeros_like(base_ref)

    l = lg_ref[...]
    lane = lax.broadcasted_iota(jnp.int32, l.shape, 1)
    sel = jnp.zeros(l.shape, F32)
    vals, idxs, hots = [], [], []
    for _k in range(TOP_K):
        m = jnp.max(l, axis=-1, keepdims=True)
        idx = jnp.min(jnp.where(l == m, lane, LANES), axis=-1, keepdims=True)
        hot = lane == idx
        vals.append(m)
        idxs.append(idx)
        hots.append(hot)
        sel = sel + jnp.where(hot, 1.0, 0.0)
        l = jnp.where(hot, -jnp.inf, l)
    es = [jnp.exp(v - vals[0]) for v in vals]
    den = es[0] + es[1] + es[2] + es[3]
    row = lax.broadcasted_iota(jnp.int32, (t, t), 0)
    col = lax.broadcasted_iota(jnp.int32, (t, t), 1)
    before = jnp.where(row > col, 1.0, 0.0).astype(BF16)
    rank_mat = _dot(before, sel.astype(BF16)) + base_ref[...]
    e_acc = jnp.zeros(l.shape, jnp.int32)
    r_acc = jnp.zeros(l.shape, jnp.int32)
    w_acc = jnp.zeros(l.shape, F32)
    for k in range(TOP_K):
        rk = jnp.sum(jnp.where(hots[k], rank_mat, 0.0), axis=-1, keepdims=True).astype(jnp.int32)
        e_acc = jnp.where(lane == k, idxs[k], e_acc)
        r_acc = jnp.where(lane == k, rk, r_acc)
        w_acc = jnp.where(lane == k, es[k] / den, w_acc)
    e_out[...] = e_acc
    r_out[...] = r_acc
    w_out[...] = w_acc
    base_ref[...] = base_ref[...] + jnp.sum(sel, axis=0, keepdims=True)
    cnt_out[...] = base_ref[...]


def _route(logits):
    n = logits.shape[0]
    t = ROUTE_TILE
    row = pl.BlockSpec((t, LANES), lambda i: (i, 0))
    return pl.pallas_call(
        _route_kernel,
        out_shape=(jax.ShapeDtypeStruct((n, LANES), jnp.int32), jax.ShapeDtypeStruct((n, LANES), jnp.int32),
                   jax.ShapeDtypeStruct((n, LANES), F32), jax.ShapeDtypeStruct((1, LANES), F32)),
        grid=(n // t,),
        in_specs=[row],
        out_specs=(row, row, row, pl.BlockSpec((1, LANES), lambda i: (0, 0))),
        scratch_shapes=[pltpu.VMEM((1, LANES), F32)],
        compiler_params=_params(("arbitrary",)),
        name="route",
    )(logits)


def _dispatch_kernel(dest_ref, pad_ref, hf_ref, xs_ref, sem):
    t = hf_ref.shape[0]

    def row_copy(src_row, dst_row):
        return pltpu.make_async_copy(hf_ref.at[pl.ds(src_row, 1)], xs_ref.at[pl.ds(dst_row, 1)], sem)

    def wait_rows():
        pltpu.make_async_copy(hf_ref, xs_ref.at[pl.ds(0, t)], sem).wait()

    def issue(g, carry):
        for u in range(MOVE_UNROLL):
            tok = g * MOVE_UNROLL + u
            for k in range(TOP_K):
                row_copy(tok, dest_ref[0, 0, tok * TOP_K + k]).start()
        return carry

    lax.fori_loop(0, t // MOVE_UNROLL, issue, 0)
    for _ in range(TOP_K):
        wait_rows()

    @pl.when(pl.program_id(0) == 0)
    def _():
        per_iter = MOVE_UNROLL * TOP_K
        n_free = pad_ref.shape[-1]

        def pad_issue(g, carry):
            for u in range(per_iter):
                row_copy(0, pad_ref[0, 0, g * per_iter + u]).start()
            return carry

        lax.fori_loop(0, n_free // per_iter, pad_issue, 0)
        for _ in range(n_free // t):
            wait_rows()


def _dispatch(hf, dest, pad_dest, rows):
    n = hf.shape[0]
    t = MOVE_TILE
    n_free = pad_dest.shape[0]
    assert n_free % t == 0 and t % MOVE_UNROLL == 0
    return pl.pallas_call(
        _dispatch_kernel,
        out_shape=jax.ShapeDtypeStruct((rows, D), F32),
        grid=(n // t,),
        in_specs=[
            pl.BlockSpec((1, 1, t * TOP_K), lambda i: (i, 0, 0), memory_space=pltpu.SMEM),
            pl.BlockSpec((1, 1, n_free), lambda i: (0, 0, 0), memory_space=pltpu.SMEM),
            pl.BlockSpec((t, D), lambda i: (i, 0)),
        ],
        out_specs=pl.BlockSpec(memory_space=pl.ANY),
        scratch_shapes=[pltpu.SemaphoreType.DMA(())],
        compiler_params=_params(("arbitrary",), has_side_effects=True),
        name="dispatch",
    )(dest.reshape(n // t, 1, t * TOP_K), pad_dest.reshape(1, 1, n_free), hf)


def _expert_kernel(be_ref, nu_ref, xs_ref, wgu_ref, bgu_ref, wd_ref, bd_ref, ys_ref, wgu_bf, wd_bf):
    i = pl.program_id(0)
    fresh = jnp.logical_or(i == 0, be_ref[i] != be_ref[jnp.maximum(i - 1, 0)])

    @pl.when(fresh)
    def _():
        wgu_bf[...] = wgu_ref[0, 0].astype(BF16)
        wd_bf[...] = wd_ref[0, 0].astype(BF16)

    @pl.when(i < nu_ref[0])
    def _():
        gu = _dot(xs_ref[...].astype(BF16), wgu_bf[...]) + bgu_ref[0, 0]
        gate = jnp.minimum(gu[:, :EXPERT_FF], SWIGLU_LIMIT)
        up = jnp.clip(gu[:, EXPERT_FF:], -SWIGLU_LIMIT, SWIGLU_LIMIT)
        act = (up + 1.0) * gate * jax.nn.sigmoid(SWIGLU_ALPHA * gate)
        ys_ref[...] = _dot(act.astype(BF16), wd_bf[...]) + bd_ref[0, 0]

    @pl.when(i >= nu_ref[0])
    def _():
        ys_ref[...] = jnp.zeros_like(ys_ref)


def _experts(xs, nb, layer, blk_expert, n_used, w_gu, b_gu, w_down, b_down):
    grid_spec = pltpu.PrefetchScalarGridSpec(
        num_scalar_prefetch=2,
        grid=(nb,),
        in_specs=[
            pl.BlockSpec((MOE_BLOCK, D), lambda i, be, nu: (jnp.minimum(i, nu[0] - 1), 0)),
            pl.BlockSpec((1, 1, D, 2 * EXPERT_FF), lambda i, be, nu: (layer, be[i], 0, 0)),
            pl.BlockSpec((1, 1, 1, 2 * EXPERT_FF), lambda i, be, nu: (layer, be[i], 0, 0)),
            pl.BlockSpec((1, 1, EXPERT_FF, D), lambda i, be, nu: (layer, be[i], 0, 0)),
            pl.BlockSpec((1, 1, 1, D), lambda i, be, nu: (layer, be[i], 0, 0)),
        ],
        out_specs=pl.BlockSpec((MOE_BLOCK, D), lambda i, be, nu: (i, 0)),
        scratch_shapes=[pltpu.VMEM((D, 2 * EXPERT_FF), BF16), pltpu.VMEM((EXPERT_FF, D), BF16)],
    )
    return pl.pallas_call(
        _expert_kernel,
        out_shape=jax.ShapeDtypeStruct((nb * MOE_BLOCK, D), F32),
        grid_spec=grid_spec,
        compiler_params=_params(("arbitrary",)),
        name="experts",
    )(blk_expert, n_used, xs, w_gu, b_gu.reshape(DEPTH, N_EXPERTS, 1, 2 * EXPERT_FF), w_down,
      b_down.reshape(DEPTH, N_EXPERTS, 1, D))


def _combine_kernel(dest_ref, ys_ref, wt_ref, x_ref, g2_ref, o_ref, buf, sem):
    t = x_ref.shape[0]

    def issue(g, carry):
        for u in range(MOVE_UNROLL):
            tok = g * MOVE_UNROLL + u
            for k in range(TOP_K):
                pltpu.make_async_copy(ys_ref.at[pl.ds(dest_ref[0, 0, tok * TOP_K + k], 1)],
                                      buf.at[k, pl.ds(tok, 1)], sem).start()
        return carry

    lax.fori_loop(0, t // MOVE_UNROLL, issue, 0)
    for k in range(TOP_K):
        pltpu.make_async_copy(ys_ref.at[pl.ds(0, t)], buf.at[k], sem).wait()
    wt = wt_ref[...]
    acc = buf[0] * wt[:, 0:1]
    for k in range(1, TOP_K):
        acc = acc + buf[k] * wt[:, k:k + 1]
    o_ref[...] = x_ref[...] + g2_ref[0] * acc


def _combine(ys, dest, wts, x, mod, b_of):
    n = x.shape[0]
    t = MOVE_TILE
    return pl.pallas_call(
        _combine_kernel,
        out_shape=jax.ShapeDtypeStruct((n, D), F32),
        grid=(n // t,),
        in_specs=[
            pl.BlockSpec((1, 1, t * TOP_K), lambda i: (i, 0, 0), memory_space=pltpu.SMEM),
            pl.BlockSpec(memory_space=pl.ANY),
            pl.BlockSpec((t, LANES), lambda i: (i, 0)),
            pl.BlockSpec((t, D), lambda i: (i, 0)),
            pl.BlockSpec((1, 1, D), lambda i: (b_of(i, t) * 6 + 5, 0, 0)),
        ],
        out_specs=pl.BlockSpec((t, D), lambda i: (i, 0)),
        scratch_shapes=[pltpu.VMEM((TOP_K, t, D), F32), pltpu.SemaphoreType.DMA(())],
        compiler_params=_params(("arbitrary",)),
        name="combine",
    )(dest.reshape(n // t, 1, t * TOP_K), ys, wts, x, mod)


def _moe(x1, hf, logits, mod, b_of, layer, w_gu, b_gu, w_down, b_down):
    n = x1.shape[0]
    eidx, rank, wts, counts = _route(logits)
    experts = jnp.arange(N_EXPERTS, dtype=jnp.int32)
    cnt = counts[0, :N_EXPERTS].astype(jnp.int32)
    padded = (cnt + MOE_BLOCK - 1) // MOE_BLOCK * MOE_BLOCK
    pend = jnp.cumsum(padded)
    pstart = pend - padded
    e4 = eidx[:, :TOP_K]
    dest = rank[:, :TOP_K] + jnp.sum(jnp.where(e4[:, :, None] == experts, pstart, 0), axis=-1)
    nb = -(-(n * TOP_K) // MOE_BLOCK) + N_EXPERTS
    blk_start = jnp.arange(nb, dtype=jnp.int32) * MOE_BLOCK
    blk_expert = jnp.minimum(jnp.sum((pend[None, :] <= blk_start[:, None]).astype(jnp.int32), axis=1),
                             N_EXPERTS - 1)
    n_used = (pend[-1:] // MOE_BLOCK).astype(jnp.int32)
    n_free = nb * MOE_BLOCK - n * TOP_K
    free_end = jnp.cumsum(padded - cnt)
    seg_first = jnp.concatenate([pstart + cnt, pend[-1:]])
    seg_skip = jnp.concatenate([free_end - (padded - cnt), free_end[-1:]])
    jj = jnp.arange(n_free, dtype=jnp.int32)
    seg = jnp.sum((free_end[None, :] <= jj[:, None]).astype(jnp.int32), axis=1)
    hot = seg[:, None] == jnp.arange(N_EXPERTS + 1, dtype=jnp.int32)
    pad_dest = jj + jnp.sum(jnp.where(hot, seg_first - seg_skip, 0), axis=1)
    xs = _dispatch(hf, dest, pad_dest, nb * MOE_BLOCK)
    ys = _experts(xs, nb, layer, blk_expert, n_used, w_gu, b_gu, w_down, b_down)
    return _combine(ys, dest, wts, x1, mod, b_of)


def _prep_even(w_in, gate_b, w_qb, w_kvb, qk_q_g, qk_k_g, w_out):
    q_m, k_m, v_m, o_m, gates, q_a, kv_a, k_r = jnp.split(
        w_in, [512, 1024, 1536, 2048, 2064, 2320, 2448], axis=1)
    zeros = lambda c: jnp.zeros((D, c), F32)
    krp = jnp.concatenate([zeros(MLA_NOPE), k_r, zeros(MLA_PAD - MLA_QK)], axis=1)
    misc = jnp.concatenate([gates, zeros(LANES - 16)], axis=1)
    w = jnp.concatenate([q_m, k_m, v_m, o_m, q_a, kv_a, krp, misc], axis=1).astype(BF16)
    wqb = jnp.pad(w_qb.reshape(MLA_Q_LORA, MLA_HEADS, MLA_QK), ((0, 0), (0, 0), (0, MLA_PAD - MLA_QK)))
    wqb = wqb.reshape(MLA_Q_LORA, MLA_HEADS * MLA_PAD).astype(BF16)
    kvb = w_kvb.reshape(MLA_KV_LORA, MLA_HEADS, MLA_NOPE + MLA_V)
    wkn = jnp.pad(kvb[:, :, :MLA_NOPE], ((0, 0), (0, 0), (0, MLA_PAD - MLA_NOPE)))
    wkn = wkn.reshape(MLA_KV_LORA, MLA_HEADS * MLA_PAD).astype(BF16)
    wv = kvb[:, :, MLA_NOPE:].reshape(MLA_KV_LORA, MLA_HEADS * MLA_V).astype(BF16)
    qg = jnp.pad(qk_q_g * (MLA_QK ** -0.5), (0, MLA_PAD - MLA_QK)).reshape(1, MLA_PAD)
    kg = jnp.pad(qk_k_g, (0, MLA_PAD - MLA_QK)).reshape(1, MLA_PAD)
    gb = gate_b.reshape(2, 8)
    w_top = w_out[:512].astype(BF16)
    w_bot = w_out[512:].astype(BF16)
    return w, wqb, wkn, wv, qg, kg, gb.reshape(2, 1, 8), gb.reshape(2, 8, 1), w_top, w_bot


def kernel(x, c, positions, ada_w, ada_b, norm_mix_g, norm_ffn_g, hy_w_in, ml_conv_w, ml_conv_b, ml_gate_b, ml_norm_g, mla_q_norm_g, mla_kv_norm_g, mla_w_qb, mla_w_kvb, mla_qk_q_g, mla_qk_k_g, hy_w_out, ret_w_in, ret_decay_f, ret_decay_b, ret_gn_g, ret_w_out, moe_router_w, moe_router_b, moe_w_gu, moe_b_gu, moe_w_down, moe_b_down):
    batch, seq, _ = x.shape
    n = batch * seq

    def b_of(i, t):
        return (i * t) // seq

    mod_all = _ada(c, ada_w, ada_b).reshape(DEPTH, batch * 6, 1, D)
    cr, sr, cm, sa, sb = _rope_tables(positions)
    xf = x.reshape(n, D)
    for layer in range(DEPTH):
        mod = mod_all[layer]
        j = layer // 2
        mix_g = norm_mix_g[layer].reshape(1, D)
        ffn_g = norm_ffn_g[layer].reshape(1, D)
        rw = jnp.pad(moe_router_w[layer], ((0, 0), (0, LANES - N_EXPERTS)))
        rb = jnp.concatenate([moe_router_b[layer], jnp.full((LANES - N_EXPERTS,), NEG_BIG, F32)]).reshape(1, LANES)
        if layer % 2 == 0:
            w, wqb, wkn, wv, qg, kg, gbc, gbr, w_top, w_bot = _prep_even(
                hy_w_in[j], ml_gate_b[j], mla_w_qb[j], mla_w_kvb[j], mla_qk_q_g[j], mla_qk_k_g[j], hy_w_out[j])
            qk_pre, v_m, o_m, misc, q_a, k_a, v_a = _in_even(
                xf, mod, b_of, mix_g, w, mla_q_norm_g[j].reshape(1, -1), mla_kv_norm_g[j].reshape(1, -1),
                wqb, wkn, wv, qg, kg, cm, sa, sb)
            qk = _conv(qk_pre, ml_conv_w[j], ml_conv_b[j], seq)
            gcol = misc[:, :16].reshape(n, 2, 8).transpose(1, 0, 2)
            grow = gcol.transpose(0, 2, 1)
            hdir = _mlstm(qk, v_m, gcol, grow, gbc, gbr, batch, seq)
            o_mla = _attn(q_a, k_a, v_a, batch, seq)
            x1, hf, logits = _out_even(hdir, o_m, o_mla, ml_norm_g[j].reshape(1, -1), w_top, w_bot,
                                       xf, mod, b_of, ffn_g, rw, rb)
        else:
            dec = jnp.broadcast_to(jnp.stack([ret_decay_f[j], ret_decay_b[j]])[:, :, None], (2, RET_HEADS, LANES))
            q_r, k_r, v_r, g_r = _in_odd(xf, mod, b_of, mix_g, ret_w_in[j].astype(BF16), cr, sr)
            ydir = _retention(q_r, k_r, v_r, dec.astype(F32), batch, seq)
            x1, hf, logits = _out_odd(ydir, g_r, ret_gn_g[j].reshape(1, -1), ret_w_out[j].astype(BF16),
                                      xf, mod, b_of, ffn_g, rw, rb)
        xf = _moe(x1, hf, logits, mod, b_of, layer, moe_w_gu, moe_b_gu, moe_w_down, moe_b_down)
    return xf.reshape(batch, seq, D)
```

```python
import functools

import jax
import jax.numpy as jnp
from jax import lax
from jax.experimental import pallas as pl
from jax.experimental.pallas import tpu as pltpu

F32 = jnp.float32
BF16 = jnp.bfloat16
HIGHEST = lax.Precision.HIGHEST

D = 1024
DEPTH = 4
EPS = 1e-6
ROPE_BASE = 10000.0

ML_HEADS = 4
ML_DK = 128
ML_DV = 128
ML_CONV = 5
ML_CHUNK = 128

MLA_HEADS = 8
MLA_Q_LORA = 256
MLA_KV_LORA = 128
MLA_NOPE = 64
MLA_ROPE = 32
MLA_V = 64
MLA_QK = MLA_NOPE + MLA_ROPE
MLA_PAD = 128
MLA_W = MLA_HEADS * MLA_PAD

RET_HEADS = 4
RET_DK = 256
RET_DV = 512
RET_CHUNK = 128

N_EXPERTS = 32
TOP_K = 4
EXPERT_FF = 1024
SWIGLU_LIMIT = 7.0
SWIGLU_ALPHA = 1.702
MOE_BLOCK = 512

LANES = 128
NEG_BIG = -1e30

ROW_TILE = 512
ATTN_Q_TILE = 256
ROUTE_TILE = 256
MOVE_TILE = 512
MOVE_UNROLL = 8
VMEM_LIMIT = 56 * 1024 * 1024


def _params(semantics, **kw):
    return pltpu.CompilerParams(dimension_semantics=semantics, vmem_limit_bytes=VMEM_LIMIT, **kw)


def _dot(a, b):
    return jnp.dot(a, b, preferred_element_type=F32)


def _dot_nt(a, b):
    return lax.dot_general(a, b, (((1,), (1,)), ((), ())), preferred_element_type=F32)


def _dot_tn(a, b):
    return lax.dot_general(a, b, (((0,), (0,)), ((), ())), preferred_element_type=F32)


def _log_sigmoid(x):
    return jnp.minimum(x, 0.0) - jnp.log1p(jnp.exp(-jnp.abs(x)))


def _rms(x, denom=None):
    n = x.shape[-1] if denom is None else denom
    return x * lax.rsqrt(jnp.sum(x * x, axis=-1, keepdims=True) / n + EPS)


def _norm_mod(x, g, sc, sh):
    return (_rms(x) * g) * (1.0 + sc) + sh


def _ada_kernel(c_ref, w_ref, b_ref, o_ref):
    c = c_ref[...]
    cs = c * jax.nn.sigmoid(c)
    o_ref[0] = jnp.dot(cs, w_ref[0], precision=HIGHEST, preferred_element_type=F32) + b_ref[0]


def _ada(c, ada_w, ada_b):
    b = c.shape[0]
    return pl.pallas_call(
        _ada_kernel,
        out_shape=jax.ShapeDtypeStruct((DEPTH, b, 6 * D), F32),
        grid=(DEPTH, 6),
        in_specs=[
            pl.BlockSpec((b, D), lambda l, j: (0, 0)),
            pl.BlockSpec((1, D, D), lambda l, j: (l, 0, j)),
            pl.BlockSpec((1, 1, D), lambda l, j: (l, 0, j)),
        ],
        out_specs=pl.BlockSpec((1, b, D), lambda l, j: (l, 0, j)),
        compiler_params=_params(("parallel", "parallel")),
        name="ada",
    )(c, ada_w, ada_b.reshape(DEPTH, 1, 6 * D))


def _rope_kernel(pos_ref, invr_ref, invm_ref, cr_ref, sr_ref, cm_ref, sm_ref):
    pos = pos_ref[...].astype(F32)
    ang_r = pos * invr_ref[...]
    cr_ref[...] = jnp.cos(ang_r)
    sr_ref[...] = jnp.sin(ang_r)
    ang_m = pos * invm_ref[...]
    lane = lax.broadcasted_iota(jnp.int32, ang_m.shape, 1)
    s = jnp.sin(ang_m)
    cm_ref[...] = jnp.cos(ang_m)
    sm_ref[...] = jnp.where(lane < MLA_NOPE + MLA_ROPE // 2, -s, s)


def _rope_tables(positions):
    n = positions.size
    inv_r = ROPE_BASE ** (-jnp.arange(0, RET_DK, 2, dtype=F32) / RET_DK)
    inv_m = ROPE_BASE ** (-jnp.arange(0, MLA_ROPE, 2, dtype=F32) / MLA_ROPE)
    half = MLA_ROPE // 2
    invm = jnp.zeros((LANES,), F32).at[MLA_NOPE:MLA_NOPE + half].set(inv_m).at[MLA_NOPE + half:MLA_QK].set(inv_m)
    t = ROW_TILE
    tab = jax.ShapeDtypeStruct((n, LANES), F32)
    row = pl.BlockSpec((t, LANES), lambda i: (i, 0))
    vec = pl.BlockSpec((1, LANES), lambda i: (0, 0))
    return pl.pallas_call(
        _rope_kernel,
        out_shape=(tab,) * 4,
        grid=(n // t,),
        in_specs=[pl.BlockSpec((t, 1), lambda i: (i, 0)), vec, vec],
        out_specs=(row,) * 4,
        compiler_params=_params(("parallel",)),
        name="rope",
    )(positions.reshape(n, 1), inv_r.reshape(1, LANES), invm.reshape(1, LANES))


EVEN_COLS = 4 * 512 + MLA_Q_LORA + 4 * LANES


def _split_bf16(x):
    hi = x.astype(BF16)
    return hi, (x - hi.astype(F32)).astype(BF16)


def _head_rsqrt(x, ind_ref, spread_ref):
    hi, lo = _split_bf16(x * x)
    s = _dot(hi, ind_ref[...]) + _dot(lo, ind_ref[...])
    hi, lo = _split_bf16(lax.rsqrt(s * (1.0 / MLA_QK) + EPS))
    return _dot(hi, spread_ref[...]) + _dot(lo, spread_ref[...])


def _in_even_kernel(x_ref, g_ref, sc_ref, sh_ref, w_ref, qng_ref, kvng_ref, wqb_ref, wqs_ref, wkn_ref, wv_ref,
                    qg_ref, qgs_ref, kg_ref, kgs_ref, ind_ref, spread_ref, cm_ref, sm_ref,
                    qk_ref, v_ref, o_ref, misc_ref, q_out, k_out, v_out):
    h = _norm_mod(x_ref[...], g_ref[...], sc_ref[0], sh_ref[0]).astype(BF16)
    qk_ref[...] = _dot(h, w_ref[:, 0:1024])
    v_ref[...] = _dot(h, w_ref[:, 1024:1536]).astype(BF16)
    o_ref[...] = _dot(h, w_ref[:, 1536:2048])
    rest = _dot(h, w_ref[:, 2048:EVEN_COLS])
    q_a = rest[:, 0:256]
    kv_a = rest[:, 256:384]
    krp = rest[:, 384:512]
    krs = rest[:, 512:640]
    misc_ref[...] = rest[:, 640:768]
    qn = (_rms(q_a) * qng_ref[...]).astype(BF16)
    kvn = (_rms(kv_a) * kvng_ref[...]).astype(BF16)
    q = _dot(qn, wqb_ref[...])
    qs = _dot(qn, wqs_ref[...])
    k = _dot(kvn, wkn_ref[...]) + jnp.concatenate([krp] * MLA_HEADS, axis=1)
    v_out[...] = _dot(kvn, wv_ref[...]).astype(BF16)
    q_rs = _head_rsqrt(q, ind_ref, spread_ref)
    k_rs = _head_rsqrt(k, ind_ref, spread_ref)
    cm, sm = cm_ref[...], sm_ref[...]
    q_c, q_s = qg_ref[...] * cm, qgs_ref[...] * sm
    k_c, k_s = kg_ref[...] * cm, kgs_ref[...] * sm
    ks = krs * k_s
    for hd in range(MLA_HEADS):
        sl = slice(hd * MLA_PAD, (hd + 1) * MLA_PAD)
        q_out[:, sl] = ((q[:, sl] * q_c + qs[:, sl] * q_s) * q_rs[:, sl]).astype(BF16)
        k_out[:, sl] = ((k[:, sl] * k_c + ks) * k_rs[:, sl]).astype(BF16)


def _in_even(x, mod, b_of, norm_g, w, qng, kvng, wqb, wqs, wkn, wv, qg, qgs, kg, kgs, ind, spread, cm, sm):
    n = x.shape[0]
    t = ROW_TILE
    full = lambda shape: pl.BlockSpec(shape, lambda i: (0,) * len(shape))
    row = lambda c: pl.BlockSpec((t, c), lambda i: (i, 0))
    out_shape = (
        jax.ShapeDtypeStruct((n, 1024), F32),
        jax.ShapeDtypeStruct((n, 512), BF16),
        jax.ShapeDtypeStruct((n, 512), F32),
        jax.ShapeDtypeStruct((n, LANES), F32),
        jax.ShapeDtypeStruct((n, MLA_W), BF16),
        jax.ShapeDtypeStruct((n, MLA_W), BF16),
        jax.ShapeDtypeStruct((n, 512), BF16),
    )
    vec = full((1, LANES))
    return pl.pallas_call(
        _in_even_kernel,
        out_shape=out_shape,
        grid=(n // t,),
        in_specs=[
            row(D), full((1, D)),
            pl.BlockSpec((1, 1, D), lambda i: (b_of(i, t) * 6 + 1, 0, 0)),
            pl.BlockSpec((1, 1, D), lambda i: (b_of(i, t) * 6 + 0, 0, 0)),
            full((D, EVEN_COLS)), full((1, 256)), full((1, 128)), full((256, MLA_W)), full((256, MLA_W)),
            full((128, MLA_W)), full((128, 512)), vec, vec, vec, vec, full((MLA_W, LANES)), full((LANES, MLA_W)),
            row(LANES), row(LANES),
        ],
        out_specs=(row(1024), row(512), row(512), row(LANES), row(MLA_W), row(MLA_W), row(512)),
        compiler_params=_params(("parallel",)),
        name="in_even",
    )(x, norm_g, mod, mod, w, qng, kvng, wqb, wqs, wkn, wv, qg, qgs, kg, kgs, ind, spread, cm, sm)


def _conv_kernel(seq_tiles, prev_ref, x_ref, next_ref, w_ref, b_ref, s_ref, o_ref, buf):
    i = pl.program_id(0)
    t = x_ref.shape[0]
    first = (i % seq_tiles) == 0
    last = (i % seq_tiles) == seq_tiles - 1
    buf[0:8, :] = jnp.where(first, 0.0, prev_ref[...])
    buf[8:8 + t, :] = x_ref[...]
    buf[8 + t:16 + t, :] = jnp.where(last, 0.0, next_ref[...])
    pad = ML_CONV // 2
    acc = b_ref[...] + w_ref[0:1, :] * buf[8 - pad:8 - pad + t, :]
    for k in range(1, ML_CONV):
        acc = acc + w_ref[k:k + 1, :] * buf[8 - pad + k:8 - pad + k + t, :]
    o_ref[...] = (acc * jax.nn.sigmoid(acc) * s_ref[...]).astype(BF16)


def _conv(qk_pre, conv_w, conv_b, seq):
    n, c = qk_pre.shape
    t = ROW_TILE
    nb8 = n // 8
    scale = jnp.concatenate([jnp.ones((512,), F32), jnp.full((512,), ML_DK ** -0.5, F32)]).reshape(1, c)
    return pl.pallas_call(
        functools.partial(_conv_kernel, seq // t),
        out_shape=jax.ShapeDtypeStruct((n, c), BF16),
        grid=(n // t,),
        in_specs=[
            pl.BlockSpec((8, c), lambda i: (jnp.maximum(i * (t // 8) - 1, 0), 0)),
            pl.BlockSpec((t, c), lambda i: (i, 0)),
            pl.BlockSpec((8, c), lambda i: (jnp.minimum((i + 1) * (t // 8), nb8 - 1), 0)),
            pl.BlockSpec((ML_CONV, c), lambda i: (0, 0)),
            pl.BlockSpec((1, c), lambda i: (0, 0)),
            pl.BlockSpec((1, c), lambda i: (0, 0)),
        ],
        out_specs=pl.BlockSpec((t, c), lambda i: (i, 0)),
        scratch_shapes=[pltpu.VMEM((t + 16, c), F32)],
        compiler_params=_params(("parallel",)),
        name="conv",
    )(qk_pre, qk_pre, qk_pre, conv_w, conv_b.reshape(1, c), scale)


def _mlstm_kernel(qf_ref, kf_ref, vf_ref, qb_ref, kb_ref, vb_ref, gcf_ref, gcb_ref, grf_ref, grb_ref,
                  bc_ref, br_ref, of_ref, ob_ref, c_ref, m_ref):
    c = pl.program_id(1)
    L = ML_CHUNK

    @pl.when(c == 0)
    def _():
        c_ref[...] = jnp.zeros_like(c_ref)
        m_ref[...] = jnp.zeros_like(m_ref)

    row = lax.broadcasted_iota(jnp.int32, (L, L), 0)
    col = lax.broadcasted_iota(jnp.int32, (L, L), 1)
    lane = lax.broadcasted_iota(jnp.int32, (L, ML_DV), 1)
    ones_col = jnp.where(lane == 0, 1.0, 0.0).astype(BF16)
    dirs = ((qf_ref, kf_ref, vf_ref, gcf_ref, grf_ref, of_ref, row >= col, row <= col),
            (qb_ref, kb_ref, vb_ref, gcb_ref, grb_ref, ob_ref, row <= col, row >= col))
    for d, (q_ref, k_ref, v_ref, gc_ref, gr_ref, o_ref, causal, causal_t) in enumerate(dirs):
        gc = gc_ref[0] + bc_ref[d]
        gr = gr_ref[0] + br_ref[d]
        i_col = gc[:, 0:ML_HEADS]
        f_col = _log_sigmoid(gc[:, ML_HEADS:2 * ML_HEADS])
        i_row = gr[0:ML_HEADS, :]
        f_row = _log_sigmoid(gr[ML_HEADS:2 * ML_HEADS, :])
        bcum_col = jnp.dot(jnp.where(causal, 1.0, 0.0), f_col, precision=HIGHEST, preferred_element_type=F32)
        bcum_row = jnp.dot(f_row, jnp.where(causal_t, 1.0, 0.0), precision=HIGHEST, preferred_element_type=F32)
        for h in range(ML_HEADS):
            sl = slice(h * ML_DK, (h + 1) * ML_DK)
            st = d * ML_HEADS + h
            q = q_ref[:, sl]
            k = k_ref[:, sl]
            vext = jnp.concatenate([v_ref[:, sl], ones_col], axis=1)
            bc = bcum_col[:, h:h + 1]
            br = bcum_row[h:h + 1, :]
            ir = i_row[h:h + 1, :]
            ic = i_col[:, h:h + 1]
            m_prev = m_ref[st:st + 1, 0:1]
            log_d = jnp.where(causal, bc - br + ir, -jnp.inf)
            m_inter = bc + m_prev
            m_t = jnp.maximum(m_inter, jnp.max(log_d, axis=-1, keepdims=True))
            w = (_dot_nt(q, k) * jnp.exp(log_d - m_t)).astype(BF16)
            a_inter = jnp.exp(m_inter - m_t)
            tot = _dot(w, vext) + a_inter * _dot(q, c_ref[st].astype(BF16))
            den = tot[:, ML_DV:ML_DV + 1]
            o_ref[:, sl] = tot[:, 0:ML_DV] / jnp.maximum(jnp.abs(den), jnp.exp(-m_t))
            total = jnp.sum(f_row[h:h + 1, :], axis=-1, keepdims=True)
            g_row = total - br + ir
            m_new = jnp.maximum(total + m_prev, jnp.max(g_row, axis=-1, keepdims=True))
            a_state = jnp.exp(total + m_prev - m_new)
            kw = (k.astype(F32) * jnp.exp(total - bc + ic - m_new)).astype(BF16)
            c_ref[st] = a_state * c_ref[st] + _dot_tn(kw, vext)
            m_ref[st:st + 1, :] = jnp.broadcast_to(m_new, (1, LANES))


def _mlstm(qk, v, gcol, grow, bcol, brow, batch, seq):
    n = qk.shape[0]
    L = ML_CHUNK
    nc = seq // L
    fw = lambda b, c: b * nc + c
    bw = lambda b, c: b * nc + nc - 1 - c
    out = jax.ShapeDtypeStruct((n, ML_HEADS * ML_DV), F32)
    return pl.pallas_call(
        _mlstm_kernel,
        out_shape=(out, out),
        grid=(batch, nc),
        in_specs=[
            pl.BlockSpec((L, 512), lambda b, c: (fw(b, c), 0)),
            pl.BlockSpec((L, 512), lambda b, c: (fw(b, c), 1)),
            pl.BlockSpec((L, 512), lambda b, c: (fw(b, c), 0)),
            pl.BlockSpec((L, 512), lambda b, c: (bw(b, c), 0)),
            pl.BlockSpec((L, 512), lambda b, c: (bw(b, c), 1)),
            pl.BlockSpec((L, 512), lambda b, c: (bw(b, c), 0)),
            pl.BlockSpec((1, L, 8), lambda b, c: (0, fw(b, c), 0)),
            pl.BlockSpec((1, L, 8), lambda b, c: (1, bw(b, c), 0)),
            pl.BlockSpec((1, 8, L), lambda b, c: (0, 0, fw(b, c))),
            pl.BlockSpec((1, 8, L), lambda b, c: (1, 0, bw(b, c))),
            pl.BlockSpec((2, 1, 8), lambda b, c: (0, 0, 0)),
            pl.BlockSpec((2, 8, 1), lambda b, c: (0, 0, 0)),
        ],
        out_specs=(pl.BlockSpec((L, 512), lambda b, c: (fw(b, c), 0)),
                   pl.BlockSpec((L, 512), lambda b, c: (bw(b, c), 0))),
        scratch_shapes=[pltpu.VMEM((2 * ML_HEADS, ML_DK, 2 * ML_DV), F32), pltpu.VMEM((8, LANES), F32)],
        compiler_params=_params(("parallel", "arbitrary")),
        name="mlstm",
    )(qk, qk, v, qk, qk, v, gcol, gcol, grow, grow, bcol, brow)


def _attn_kernel(q_ref, k_ref, v_ref, o_ref):
    outs = []
    for j in range(2):
        q = q_ref[:, j * MLA_PAD:(j + 1) * MLA_PAD]
        k = k_ref[:, j * MLA_PAD:(j + 1) * MLA_PAD]
        v = v_ref[:, j * MLA_V:(j + 1) * MLA_V]
        s = _dot_nt(q, k)
        p = jnp.exp(s - jnp.max(s, axis=-1, keepdims=True))
        l = jnp.sum(p, axis=-1, keepdims=True)
        outs.append(_dot(p.astype(BF16), v) / l)
    o_ref[...] = jnp.concatenate(outs, axis=1).astype(BF16)


def _attn(q, k, v, batch, seq):
    n = q.shape[0]
    t = min(ATTN_Q_TILE, seq)
    nq = seq // t
    return pl.pallas_call(
        _attn_kernel,
        out_shape=jax.ShapeDtypeStruct((n, MLA_HEADS * MLA_V), BF16),
        grid=(batch, MLA_HEADS // 2, nq),
        in_specs=[
            pl.BlockSpec((t, 2 * MLA_PAD), lambda b, h, i: (b * nq + i, h)),
            pl.BlockSpec((seq, 2 * MLA_PAD), lambda b, h, i: (b, h)),
            pl.BlockSpec((seq, 2 * MLA_V), lambda b, h, i: (b, h)),
        ],
        out_specs=pl.BlockSpec((t, 2 * MLA_V), lambda b, h, i: (b * nq + i, h)),
        compiler_params=_params(("parallel", "parallel", "arbitrary")),
        name="attn",
    )(q, k, v)


def _mixer_tail(y, x_ref, g1_ref, fg_ref, sc2_ref, sh2_ref, rw_ref, rb_ref, x_out, hf_out, lg_out):
    x1 = x_ref[...] + g1_ref[0] * y
    x_out[...] = x1
    hf = _norm_mod(x1, fg_ref[...], sc2_ref[0], sh2_ref[0])
    hf_out[...] = hf
    lg_out[...] = jnp.dot(hf, rw_ref[...], precision=HIGHEST, preferred_element_type=F32) + rb_ref[...]


def _tail_specs(n, t, b_of):
    full = lambda shape: pl.BlockSpec(shape, lambda i: (0,) * len(shape))
    mod = lambda j: pl.BlockSpec((1, 1, D), lambda i: (b_of(i, t) * 6 + j, 0, 0))
    row = lambda c: pl.BlockSpec((t, c), lambda i: (i, 0))
    in_specs = [row(D), mod(2), full((1, D)), mod(4), mod(3), full((D, LANES)), full((1, LANES))]
    out_shape = (jax.ShapeDtypeStruct((n, D), F32), jax.ShapeDtypeStruct((n, D), F32),
                 jax.ShapeDtypeStruct((n, LANES), F32))
    out_specs = (row(D), row(D), row(LANES))
    return in_specs, out_shape, out_specs


def _out_even_kernel(hf_ref, hb_ref, om_ref, oa_ref, mg_ref, wt_ref, wb_ref, *tail):
    hs = hf_ref[...] + hb_ref[...]
    gate = jax.nn.sigmoid(om_ref[...])
    parts = []
    for h in range(ML_HEADS):
        sl = slice(h * ML_DV, (h + 1) * ML_DV)
        parts.append(_rms(hs[:, sl]) * mg_ref[:, sl] * gate[:, sl])
    hml = jnp.concatenate(parts, axis=1).astype(BF16)
    y = _dot(hml, wt_ref[...]) + _dot(oa_ref[...], wb_ref[...])
    _mixer_tail(y, *tail)


def _out_even(h_f, h_b, o_m, o_mla, ml_g, w_top, w_bot, x, mod, b_of, ffn_g, rw, rb):
    n = x.shape[0]
    t = ROW_TILE
    full = lambda shape: pl.BlockSpec(shape, lambda i: (0,) * len(shape))
    row = lambda c: pl.BlockSpec((t, c), lambda i: (i, 0))
    tin, out_shape, out_specs = _tail_specs(n, t, b_of)
    return pl.pallas_call(
        _out_even_kernel,
        out_shape=out_shape,
        grid=(n // t,),
        in_specs=[row(512), row(512), row(512), row(512), full((1, 512)), full((512, D)), full((512, D))] + tin,
        out_specs=out_specs,
        compiler_params=_params(("parallel",)),
        name="out_even",
    )(h_f, h_b, o_m, o_mla, ml_g, w_top, w_bot, x, mod, ffn_g, mod, mod, rw, rb)


def _in_odd_kernel(x_ref, g_ref, sc_ref, sh_ref, w_ref, cr_ref, sr_ref, q_out, k_out, v_out, g_out):
    h = _norm_mod(x_ref[...], g_ref[...], sc_ref[0], sh_ref[0]).astype(BF16)
    cos, sin = cr_ref[...], sr_ref[...]
    hw = RET_DK // 2
    for idx, (dst, scale) in enumerate(((q_out, RET_DK ** -0.5), (k_out, 1.0))):
        z = _dot(h, w_ref[:, idx * 1024:(idx + 1) * 1024])
        for hd in range(RET_HEADS):
            x1 = z[:, hd * RET_DK:hd * RET_DK + hw]
            x2 = z[:, hd * RET_DK + hw:(hd + 1) * RET_DK]
            dst[:, hd * RET_DK:hd * RET_DK + hw] = ((x1 * cos - x2 * sin) * scale).astype(BF16)
            dst[:, hd * RET_DK + hw:(hd + 1) * RET_DK] = ((x2 * cos + x1 * sin) * scale).astype(BF16)
    v_out[...] = _dot(h, w_ref[:, 2048:4096]).astype(BF16)
    g_out[...] = _dot(h, w_ref[:, 4096:6144])


def _in_odd(x, mod, b_of, norm_g, w, cr, sr):
    n = x.shape[0]
    t = ROW_TILE
    full = lambda shape: pl.BlockSpec(shape, lambda i: (0,) * len(shape))
    row = lambda c: pl.BlockSpec((t, c), lambda i: (i, 0))
    out_shape = (
        jax.ShapeDtypeStruct((n, 1024), BF16), jax.ShapeDtypeStruct((n, 1024), BF16),
        jax.ShapeDtypeStruct((n, 2048), BF16), jax.ShapeDtypeStruct((n, 2048), F32),
    )
    return pl.pallas_call(
        _in_odd_kernel,
        out_shape=out_shape,
        grid=(n // t,),
        in_specs=[
            row(D), full((1, D)),
            pl.BlockSpec((1, 1, D), lambda i: (b_of(i, t) * 6 + 1, 0, 0)),
            pl.BlockSpec((1, 1, D), lambda i: (b_of(i, t) * 6 + 0, 0, 0)),
            full((D, 6144)), row(LANES), row(LANES),
        ],
        out_specs=(row(1024), row(1024), row(2048), row(2048)),
        compiler_params=_params(("parallel",)),
        name="in_odd",
    )(x, norm_g, mod, mod, w, cr, sr)


def _ret_kernel(qf_ref, kf_ref, vf_ref, qb_ref, kb_ref, vb_ref, dec_ref, of_ref, ob_ref, s_ref):
    c = pl.program_id(1)
    L = RET_CHUNK

    @pl.when(c == 0)
    def _():
        s_ref[...] = jnp.zeros_like(s_ref)

    row = lax.broadcasted_iota(jnp.int32, (L, L), 0)
    col = lax.broadcasted_iota(jnp.int32, (L, L), 1)
    pos = lax.broadcasted_iota(jnp.int32, (L, 1), 0)
    dirs = ((qf_ref, kf_ref, vf_ref, of_ref, row - col, pos + 1, L - 1 - pos),
            (qb_ref, kb_ref, vb_ref, ob_ref, col - row, L - pos, pos))
    for d, (q_ref, k_ref, v_ref, o_ref, rel_i, exp_q_i, exp_k_i) in enumerate(dirs):
        rel = rel_i.astype(F32)
        exp_q = exp_q_i.astype(F32)
        exp_k = exp_k_i.astype(F32)
        for h in range(RET_HEADS):
            st = d * RET_HEADS + h
            lg = _log_sigmoid(dec_ref[d, h:h + 1, :])
            lg1 = lg[:, 0:1]
            decay_in = jnp.where(rel >= 0, jnp.exp(jnp.maximum(rel, 0.0) * lg), 0.0)
            q = q_ref[:, h * RET_DK:(h + 1) * RET_DK]
            k = k_ref[:, h * RET_DK:(h + 1) * RET_DK]
            v = v_ref[:, h * RET_DV:(h + 1) * RET_DV]
            sc = (_dot_nt(q, k) * decay_in).astype(BF16)
            o_ref[:, h * RET_DV:(h + 1) * RET_DV] = (
                _dot(sc, v) + jnp.exp(exp_q * lg1) * _dot(q, s_ref[st].astype(BF16)))
            kd = (k.astype(F32) * jnp.exp(exp_k * lg1)).astype(BF16)
            s_ref[st] = jnp.exp(L * lg1) * s_ref[st] + _dot_tn(kd, v)


def _retention(q, k, v, dec, batch, seq):
    n = q.shape[0]
    L = RET_CHUNK
    nc = seq // L
    fw = lambda b, c: (b * nc + c, 0)
    bw = lambda b, c: (b * nc + nc - 1 - c, 0)
    out = jax.ShapeDtypeStruct((n, RET_HEADS * RET_DV), F32)
    return pl.pallas_call(
        _ret_kernel,
        out_shape=(out, out),
        grid=(batch, nc),
        in_specs=[
            pl.BlockSpec((L, 1024), fw), pl.BlockSpec((L, 1024), fw), pl.BlockSpec((L, 2048), fw),
            pl.BlockSpec((L, 1024), bw), pl.BlockSpec((L, 1024), bw), pl.BlockSpec((L, 2048), bw),
            pl.BlockSpec((2, RET_HEADS, LANES), lambda b, c: (0, 0, 0)),
        ],
        out_specs=(pl.BlockSpec((L, 2048), fw), pl.BlockSpec((L, 2048), bw)),
        scratch_shapes=[pltpu.VMEM((2 * RET_HEADS, RET_DK, RET_DV), F32)],
        compiler_params=_params(("parallel", "arbitrary")),
        name="retention",
    )(q, k, v, q, k, v, dec)


def _out_odd_kernel(yf_ref, yb_ref, g_ref, gn_ref, w_ref, *tail):
    ys = yf_ref[...] + yb_ref[...]
    g = g_ref[...]
    gate = g * jax.nn.sigmoid(g)
    parts = []
    for h in range(RET_HEADS):
        sl = slice(h * RET_DV, (h + 1) * RET_DV)
        yh = ys[:, sl]
        yc = yh - jnp.mean(yh, axis=-1, keepdims=True)
        var = jnp.mean(yc * yc, axis=-1, keepdims=True)
        parts.append(yc * lax.rsqrt(var + EPS) * gn_ref[:, sl] * gate[:, sl])
    yn = jnp.concatenate(parts, axis=1).astype(BF16)
    _mixer_tail(_dot(yn, w_ref[...]), *tail)


def _out_odd(y_f, y_b, g, gn_g, w_out, x, mod, b_of, ffn_g, rw, rb):
    n = x.shape[0]
    t = ROW_TILE
    full = lambda shape: pl.BlockSpec(shape, lambda i: (0,) * len(shape))
    row = lambda c: pl.BlockSpec((t, c), lambda i: (i, 0))
    tin, out_shape, out_specs = _tail_specs(n, t, b_of)
    return pl.pallas_call(
        _out_odd_kernel,
        out_shape=out_shape,
        grid=(n // t,),
        in_specs=[row(2048), row(2048), row(2048), full((1, 2048)), full((2048, D))] + tin,
        out_specs=out_specs,
        compiler_params=_params(("parallel",)),
        name="out_odd",
    )(y_f, y_b, g, gn_g, w_out, x, mod, ffn_g, mod, mod, rw, rb)


def _route_kernel(lg_ref, e_out, r_out, w_out, cnt_out, base_ref):
    i = pl.program_id(0)
    t = lg_ref.shape[0]

    @pl.when(i == 0)
    def _():
        base_ref[...] = jnp.zeros_like(base_ref)

    l = lg_ref[...]
    lane = lax.broadcasted_iota(jnp.int32, l.shape, 1)
    sel = jnp.zeros(l.shape, F32)
    vals, idxs, hots = [], [], []
    for _k in range(TOP_K):
        m = jnp.max(l, axis=-1, keepdims=True)
        idx = jnp.min(jnp.where(l == m, lane, LANES), axis=-1, keepdims=True)
        hot = lane == idx
        vals.append(m)
        idxs.append(idx)
        hots.append(hot)
        sel = sel + jnp.where(hot, 1.0, 0.0)
        l = jnp.where(hot, -jnp.inf, l)
    es = [jnp.exp(v - vals[0]) for v in vals]
    den = es[0] + es[1] + es[2] + es[3]
    row = lax.broadcasted_iota(jnp.int32, (t, t), 0)
    col = lax.broadcasted_iota(jnp.int32, (t, t), 1)
    before = jnp.where(row > col, 1.0, 0.0).astype(BF16)
    rank_mat = _dot(before, sel.astype(BF16)) + base_ref[...]
    e_acc = jnp.zeros(l.shape, jnp.int32)
    r_acc = jnp.zeros(l.shape, jnp.int32)
    w_acc = jnp.zeros(l.shape, F32)
    for k in range(TOP_K):
        rk = jnp.sum(jnp.where(hots[k], rank_mat, 0.0), axis=-1, keepdims=True).astype(jnp.int32)
        e_acc = jnp.where(lane == k, idxs[k], e_acc)
        r_acc = jnp.where(lane == k, rk, r_acc)
        w_acc = jnp.where(lane == k, es[k] / den, w_acc)
    e_out[...] = e_acc
    r_out[...] = r_acc
    w_out[...] = w_acc
    base_ref[...] = base_ref[...] + jnp.sum(sel, axis=0, keepdims=True)
    cnt_out[...] = base_ref[...]


def _route(logits):
    n = logits.shape[0]
    t = ROUTE_TILE
    row = pl.BlockSpec((t, LANES), lambda i: (i, 0))
    return pl.pallas_call(
        _route_kernel,
        out_shape=(jax.ShapeDtypeStruct((n, LANES), jnp.int32), jax.ShapeDtypeStruct((n, LANES), jnp.int32),
                   jax.ShapeDtypeStruct((n, LANES), F32), jax.ShapeDtypeStruct((1, LANES), F32)),
        grid=(n // t,),
        in_specs=[row],
        out_specs=(row, row, row, pl.BlockSpec((1, LANES), lambda i: (0, 0))),
        scratch_shapes=[pltpu.VMEM((1, LANES), F32)],
        compiler_params=_params(("arbitrary",)),
        name="route",
    )(logits)


def _dispatch_kernel(dest_ref, pad_ref, hf_ref, xs_ref, sem):
    t = hf_ref.shape[0]

    def row_copy(src_row, dst_row):
        return pltpu.make_async_copy(hf_ref.at[pl.ds(src_row, 1)], xs_ref.at[pl.ds(dst_row, 1)], sem)

    def wait_rows():
        pltpu.make_async_copy(hf_ref, xs_ref.at[pl.ds(0, t)], sem).wait()

    def issue(g, carry):
        for u in range(MOVE_UNROLL):
            tok = g * MOVE_UNROLL + u
            for k in range(TOP_K):
                row_copy(tok, dest_ref[0, 0, tok * TOP_K + k]).start()
        return carry

    lax.fori_loop(0, t // MOVE_UNROLL, issue, 0)
    for _ in range(TOP_K):
        wait_rows()

    @pl.when(pl.program_id(0) == 0)
    def _():
        per_iter = MOVE_UNROLL * TOP_K
        n_free = pad_ref.shape[-1]

        def pad_issue(g, carry):
            for u in range(per_iter):
                row_copy(0, pad_ref[0, 0, g * per_iter + u]).start()
            return carry

        lax.fori_loop(0, n_free // per_iter, pad_issue, 0)
        for _ in range(n_free // t):
            wait_rows()


def _dispatch(hf, dest, pad_dest, rows):
    n = hf.shape[0]
    t = MOVE_TILE
    n_free = pad_dest.shape[0]
    assert n_free % t == 0 and t % MOVE_UNROLL == 0
    return pl.pallas_call(
        _dispatch_kernel,
        out_shape=jax.ShapeDtypeStruct((rows, D), F32),
        grid=(n // t,),
        in_specs=[
            pl.BlockSpec((1, 1, t * TOP_K), lambda i: (i, 0, 0), memory_space=pltpu.SMEM),
            pl.BlockSpec((1, 1, n_free), lambda i: (0, 0, 0), memory_space=pltpu.SMEM),
            pl.BlockSpec((t, D), lambda i: (i, 0)),
        ],
        out_specs=pl.BlockSpec(memory_space=pl.ANY),
        scratch_shapes=[pltpu.SemaphoreType.DMA(())],
        compiler_params=_params(("arbitrary",), has_side_effects=True),
        name="dispatch",
    )(dest.reshape(n // t, 1, t * TOP_K), pad_dest.reshape(1, 1, n_free), hf)


def _expert_kernel(be_ref, nu_ref, xs_ref, wgu_ref, bgu_ref, wd_ref, bd_ref, ys_ref, wgu_bf, wd_bf):
    i = pl.program_id(0)
    fresh = jnp.logical_or(i == 0, be_ref[i] != be_ref[jnp.maximum(i - 1, 0)])

    @pl.when(fresh)
    def _():
        wgu_bf[...] = wgu_ref[0, 0].astype(BF16)
        wd_bf[...] = wd_ref[0, 0].astype(BF16)

    @pl.when(i < nu_ref[0])
    def _():
        gu = _dot(xs_ref[...].astype(BF16), wgu_bf[...]) + bgu_ref[0, 0]
        gate = jnp.minimum(gu[:, :EXPERT_FF], SWIGLU_LIMIT)
        up = jnp.clip(gu[:, EXPERT_FF:], -SWIGLU_LIMIT, SWIGLU_LIMIT)
        act = (up + 1.0) * gate * jax.nn.sigmoid(SWIGLU_ALPHA * gate)
        ys_ref[...] = _dot(act.astype(BF16), wd_bf[...]) + bd_ref[0, 0]

    @pl.when(i >= nu_ref[0])
    def _():
        ys_ref[...] = jnp.zeros_like(ys_ref)


def _experts(xs, nb, layer, blk_expert, n_used, w_gu, b_gu, w_down, b_down):
    grid_spec = pltpu.PrefetchScalarGridSpec(
        num_scalar_prefetch=2,
        grid=(nb,),
        in_specs=[
            pl.BlockSpec((MOE_BLOCK, D), lambda i, be, nu: (jnp.minimum(i, nu[0] - 1), 0)),
            pl.BlockSpec((1, 1, D, 2 * EXPERT_FF), lambda i, be, nu: (layer, be[i], 0, 0)),
            pl.BlockSpec((1, 1, 1, 2 * EXPERT_FF), lambda i, be, nu: (layer, be[i], 0, 0)),
            pl.BlockSpec((1, 1, EXPERT_FF, D), lambda i, be, nu: (layer, be[i], 0, 0)),
            pl.BlockSpec((1, 1, 1, D), lambda i, be, nu: (layer, be[i], 0, 0)),
        ],
        out_specs=pl.BlockSpec((MOE_BLOCK, D), lambda i, be, nu: (i, 0)),
        scratch_shapes=[pltpu.VMEM((D, 2 * EXPERT_FF), BF16), pltpu.VMEM((EXPERT_FF, D), BF16)],
    )
    return pl.pallas_call(
        _expert_kernel,
        out_shape=jax.ShapeDtypeStruct((nb * MOE_BLOCK, D), F32),
        grid_spec=grid_spec,
        compiler_params=_params(("arbitrary",)),
        name="experts",
    )(blk_expert, n_used, xs, w_gu, b_gu.reshape(DEPTH, N_EXPERTS, 1, 2 * EXPERT_FF), w_down,
      b_down.reshape(DEPTH, N_EXPERTS, 1, D))


def _combine_kernel(dest_ref, ys_ref, wt_ref, x_ref, g2_ref, o_ref, buf, sem):
    t = x_ref.shape[0]

    def issue(g, carry):
        for u in range(MOVE_UNROLL):
            tok = g * MOVE_UNROLL + u
            for k in range(TOP_K):
                pltpu.make_async_copy(ys_ref.at[pl.ds(dest_ref[0, 0, tok * TOP_K + k], 1)],
                                      buf.at[k, pl.ds(tok, 1)], sem).start()
        return carry

    lax.fori_loop(0, t // MOVE_UNROLL, issue, 0)
    for k in range(TOP_K):
        pltpu.make_async_copy(ys_ref.at[pl.ds(0, t)], buf.at[k], sem).wait()
    wt = wt_ref[...]
    acc = buf[0] * wt[:, 0:1]
    for k in range(1, TOP_K):
        acc = acc + buf[k] * wt[:, k:k + 1]
    o_ref[...] = x_ref[...] + g2_ref[0] * acc


def _combine(ys, dest, wts, x, mod, b_of):
    n = x.shape[0]
    t = MOVE_TILE
    return pl.pallas_call(
        _combine_kernel,
        out_shape=jax.ShapeDtypeStruct((n, D), F32),
        grid=(n // t,),
        in_specs=[
            pl.BlockSpec((1, 1, t * TOP_K), lambda i: (i, 0, 0), memory_space=pltpu.SMEM),
            pl.BlockSpec(memory_space=pl.ANY),
            pl.BlockSpec((t, LANES), lambda i: (i, 0)),
            pl.BlockSpec((t, D), lambda i: (i, 0)),
            pl.BlockSpec((1, 1, D), lambda i: (b_of(i, t) * 6 + 5, 0, 0)),
        ],
        out_specs=pl.BlockSpec((t, D), lambda i: (i, 0)),
        scratch_shapes=[pltpu.VMEM((TOP_K, t, D), F32), pltpu.SemaphoreType.DMA(())],
        compiler_params=_params(("arbitrary",)),
        name="combine",
    )(dest.reshape(n // t, 1, t * TOP_K), ys, wts, x, mod)


def _moe(x1, hf, logits, mod, b_of, layer, w_gu, b_gu, w_down, b_down):
    n = x1.shape[0]
    eidx, rank, wts, counts = _route(logits)
    experts = jnp.arange(N_EXPERTS, dtype=jnp.int32)
    cnt = counts[0, :N_EXPERTS].astype(jnp.int32)
    padded = (cnt + MOE_BLOCK - 1) // MOE_BLOCK * MOE_BLOCK
    pend = jnp.cumsum(padded)
    pstart = pend - padded
    e4 = eidx[:, :TOP_K]
    dest = rank[:, :TOP_K] + jnp.sum(jnp.where(e4[:, :, None] == experts, pstart, 0), axis=-1)
    nb = -(-(n * TOP_K) // MOE_BLOCK) + N_EXPERTS
    blk_start = jnp.arange(nb, dtype=jnp.int32) * MOE_BLOCK
    blk_expert = jnp.minimum(jnp.sum((pend[None, :] <= blk_start[:, None]).astype(jnp.int32), axis=1),
                             N_EXPERTS - 1)
    n_used = (pend[-1:] // MOE_BLOCK).astype(jnp.int32)
    n_free = nb * MOE_BLOCK - n * TOP_K
    free_end = jnp.cumsum(padded - cnt)
    seg_first = jnp.concatenate([pstart + cnt, pend[-1:]])
    seg_skip = jnp.concatenate([free_end - (padded - cnt), free_end[-1:]])
    jj = jnp.arange(n_free, dtype=jnp.int32)
    seg = jnp.sum((free_end[None, :] <= jj[:, None]).astype(jnp.int32), axis=1)
    hot = seg[:, None] == jnp.arange(N_EXPERTS + 1, dtype=jnp.int32)
    pad_dest = jj + jnp.sum(jnp.where(hot, seg_first - seg_skip, 0), axis=1)
    xs = _dispatch(hf, dest, pad_dest, nb * MOE_BLOCK)
    ys = _experts(xs, nb, layer, blk_expert, n_used, w_gu, b_gu, w_down, b_down)
    return _combine(ys, dest, wts, x1, mod, b_of)


def _head_tile(t, partner):
    half = MLA_ROPE // 2
    lead = t.shape[:-1]
    tail = jnp.zeros(lead + (MLA_PAD - MLA_QK,), t.dtype)
    if not partner:
        return jnp.concatenate([t, tail], axis=-1)
    return jnp.concatenate([jnp.zeros(lead + (MLA_NOPE,), t.dtype), t[..., MLA_NOPE + half:],
                            t[..., MLA_NOPE:MLA_NOPE + half], tail], axis=-1)


def _prep_even(w_in, gate_b, w_qb, w_kvb, qk_q_g, qk_k_g, w_out):
    q_m, k_m, v_m, o_m, gates, q_a, kv_a, k_r = jnp.split(
        w_in, [512, 1024, 1536, 2048, 2064, 2320, 2448], axis=1)
    zeros = lambda c: jnp.zeros((D, c), F32)
    k_r96 = jnp.concatenate([zeros(MLA_NOPE), k_r], axis=1)
    misc = jnp.concatenate([gates, zeros(LANES - 16)], axis=1)
    w = jnp.concatenate([q_m, k_m, v_m, o_m, q_a, kv_a, _head_tile(k_r96, False), _head_tile(k_r96, True), misc],
                        axis=1).astype(BF16)
    qb = w_qb.reshape(MLA_Q_LORA, MLA_HEADS, MLA_QK)
    wqb = _head_tile(qb, False).reshape(MLA_Q_LORA, MLA_W).astype(BF16)
    wqs = _head_tile(qb, True).reshape(MLA_Q_LORA, MLA_W).astype(BF16)
    kvb = w_kvb.reshape(MLA_KV_LORA, MLA_HEADS, MLA_NOPE + MLA_V)
    wkn = jnp.pad(kvb[:, :, :MLA_NOPE], ((0, 0), (0, 0), (0, MLA_PAD - MLA_NOPE)))
    wkn = wkn.reshape(MLA_KV_LORA, MLA_W).astype(BF16)
    wv = kvb[:, :, MLA_NOPE:].reshape(MLA_KV_LORA, MLA_HEADS * MLA_V).astype(BF16)
    qg96 = qk_q_g * (MLA_QK ** -0.5)
    qg = _head_tile(qg96, False).reshape(1, MLA_PAD)
    qgs = _head_tile(qg96, True).reshape(1, MLA_PAD)
    kg = _head_tile(qk_k_g, False).reshape(1, MLA_PAD)
    kgs = _head_tile(qk_k_g, True).reshape(1, MLA_PAD)
    gb = gate_b.reshape(2, 8)
    w_top = w_out[:512].astype(BF16)
    w_bot = w_out[512:].astype(BF16)
    return w, wqb, wqs, wkn, wv, qg, qgs, kg, kgs, gb.reshape(2, 1, 8), gb.reshape(2, 8, 1), w_top, w_bot


def kernel(x, c, positions, ada_w, ada_b, norm_mix_g, norm_ffn_g, hy_w_in, ml_conv_w, ml_conv_b, ml_gate_b, ml_norm_g, mla_q_norm_g, mla_kv_norm_g, mla_w_qb, mla_w_kvb, mla_qk_q_g, mla_qk_k_g, hy_w_out, ret_w_in, ret_decay_f, ret_decay_b, ret_gn_g, ret_w_out, moe_router_w, moe_router_b, moe_w_gu, moe_b_gu, moe_w_down, moe_b_down):
    batch, seq, _ = x.shape
    n = batch * seq

    def b_of(i, t):
        return (i * t) // seq

    mod_all = _ada(c, ada_w, ada_b).reshape(DEPTH, batch * 6, 1, D)
    cr, sr, cm, sm = _rope_tables(positions)
    head_of_lane = jnp.arange(MLA_W, dtype=jnp.int32) // MLA_PAD
    ind = (head_of_lane[:, None] == jnp.arange(LANES, dtype=jnp.int32)[None, :]).astype(BF16)
    spread = ind.T
    xf = x.reshape(n, D)
    for layer in range(DEPTH):
        mod = mod_all[layer]
        j = layer // 2
        mix_g = norm_mix_g[layer].reshape(1, D)
        ffn_g = norm_ffn_g[layer].reshape(1, D)
        rw = jnp.pad(moe_router_w[layer], ((0, 0), (0, LANES - N_EXPERTS)))
        rb = jnp.concatenate([moe_router_b[layer], jnp.full((LANES - N_EXPERTS,), NEG_BIG, F32)]).reshape(1, LANES)
        if layer % 2 == 0:
            w, wqb, wqs, wkn, wv, qg, qgs, kg, kgs, gbc, gbr, w_top, w_bot = _prep_even(
                hy_w_in[j], ml_gate_b[j], mla_w_qb[j], mla_w_kvb[j], mla_qk_q_g[j], mla_qk_k_g[j], hy_w_out[j])
            qk_pre, v_m, o_m, misc, q_a, k_a, v_a = _in_even(
                xf, mod, b_of, mix_g, w, mla_q_norm_g[j].reshape(1, -1), mla_kv_norm_g[j].reshape(1, -1),
                wqb, wqs, wkn, wv, qg, qgs, kg, kgs, ind, spread, cm, sm)
            qk = _conv(qk_pre, ml_conv_w[j], ml_conv_b[j], seq)
            gcol = misc[:, :16].reshape(n, 2, 8).transpose(1, 0, 2)
            grow = gcol.transpose(0, 2, 1)
            h_f, h_b = _mlstm(qk, v_m, gcol, grow, gbc, gbr, batch, seq)
            o_mla = _attn(q_a, k_a, v_a, batch, seq)
            x1, hf, logits = _out_even(h_f, h_b, o_m, o_mla, ml_norm_g[j].reshape(1, -1), w_top, w_bot,
                                       xf, mod, b_of, ffn_g, rw, rb)
        else:
            dec = jnp.broadcast_to(jnp.stack([ret_decay_f[j], ret_decay_b[j]])[:, :, None], (2, RET_HEADS, LANES))
            q_r, k_r, v_r, g_r = _in_odd(xf, mod, b_of, mix_g, ret_w_in[j].astype(BF16), cr, sr)
            y_f, y_b = _retention(q_r, k_r, v_r, dec.astype(F32), batch, seq)
            x1, hf, logits = _out_odd(y_f, y_b, g_r, ret_gn_g[j].reshape(1, -1), ret_w_out[j].astype(BF16),
                                      xf, mod, b_of, ffn_g, rw, rb)
        xf = _moe(x1, hf, logits, mod, b_of, layer, moe_w_gu, moe_b_gu, moe_w_down, moe_b_down)
    return xf.reshape(batch, seq, D)
```

```python
import functools

import jax
import jax.numpy as jnp
from jax import lax
from jax.experimental import pallas as pl
from jax.experimental.pallas import tpu as pltpu

F32 = jnp.float32
BF16 = jnp.bfloat16
HIGHEST = lax.Precision.HIGHEST

D = 1024
DEPTH = 4
EPS = 1e-6
ROPE_BASE = 10000.0

ML_HEADS = 4
ML_DK = 128
ML_DV = 128
ML_CONV = 5
ML_CHUNK = 256

MLA_HEADS = 8
MLA_Q_LORA = 256
MLA_KV_LORA = 128
MLA_NOPE = 64
MLA_ROPE = 32
MLA_V = 64
MLA_QK = MLA_NOPE + MLA_ROPE
MLA_PAD = 128
MLA_W = MLA_HEADS * MLA_PAD

RET_HEADS = 4
RET_DK = 256
RET_DV = 512
RET_CHUNK = 256

N_EXPERTS = 32
TOP_K = 4
EXPERT_FF = 1024
SWIGLU_LIMIT = 7.0
SWIGLU_ALPHA = 1.702
MOE_BLOCK = 512

LANES = 128
NEG_BIG = -1e30

ROW_TILE = 512
ATTN_Q_TILE = 256
ROUTE_TILE = 256
COMBINE_TILE = 256
MOVE_UNROLL = 8
VMEM_LIMIT = 56 * 1024 * 1024


def _params(semantics, **kw):
    return pltpu.CompilerParams(dimension_semantics=semantics, vmem_limit_bytes=VMEM_LIMIT, **kw)


def _dot(a, b):
    return jnp.dot(a, b, preferred_element_type=F32)


def _dot_nt(a, b):
    return lax.dot_general(a, b, (((1,), (1,)), ((), ())), preferred_element_type=F32)


def _dot_tn(a, b):
    return lax.dot_general(a, b, (((0,), (0,)), ((), ())), preferred_element_type=F32)


def _log_sigmoid(x):
    return jnp.minimum(x, 0.0) - jnp.log1p(jnp.exp(-jnp.abs(x)))


def _rms(x, denom=None):
    n = x.shape[-1] if denom is None else denom
    return x * lax.rsqrt(jnp.sum(x * x, axis=-1, keepdims=True) / n + EPS)


def _norm_mod(x, g, sc, sh):
    return (_rms(x) * g) * (1.0 + sc) + sh


def _ada_kernel(c_ref, w_ref, b_ref, o_ref):
    c = c_ref[...]
    cs = c * jax.nn.sigmoid(c)
    o_ref[0] = jnp.dot(cs, w_ref[0], precision=HIGHEST, preferred_element_type=F32) + b_ref[0]


def _ada(c, ada_w, ada_b):
    b = c.shape[0]
    return pl.pallas_call(
        _ada_kernel,
        out_shape=jax.ShapeDtypeStruct((DEPTH, b, 6 * D), F32),
        grid=(DEPTH, 6),
        in_specs=[
            pl.BlockSpec((b, D), lambda l, j: (0, 0)),
            pl.BlockSpec((1, D, D), lambda l, j: (l, 0, j)),
            pl.BlockSpec((1, 1, D), lambda l, j: (l, 0, j)),
        ],
        out_specs=pl.BlockSpec((1, b, D), lambda l, j: (l, 0, j)),
        compiler_params=_params(("parallel", "parallel")),
        name="ada",
    )(c, ada_w, ada_b.reshape(DEPTH, 1, 6 * D))


def _rope_kernel(pos_ref, invr_ref, invm_ref, cr_ref, sr_ref, cm_ref, sm_ref):
    pos = pos_ref[...].astype(F32)
    ang_r = pos * invr_ref[...]
    cr_ref[...] = jnp.cos(ang_r)
    sr_ref[...] = jnp.sin(ang_r)
    ang_m = pos * invm_ref[...]
    lane = lax.broadcasted_iota(jnp.int32, ang_m.shape, 1)
    s = jnp.sin(ang_m)
    cm_ref[...] = jnp.cos(ang_m)
    sm_ref[...] = jnp.where(lane < MLA_NOPE + MLA_ROPE // 2, -s, s)


def _rope_tables(positions):
    n = positions.size
    inv_r = ROPE_BASE ** (-jnp.arange(0, RET_DK, 2, dtype=F32) / RET_DK)
    inv_m = ROPE_BASE ** (-jnp.arange(0, MLA_ROPE, 2, dtype=F32) / MLA_ROPE)
    half = MLA_ROPE // 2
    invm = jnp.zeros((LANES,), F32).at[MLA_NOPE:MLA_NOPE + half].set(inv_m).at[MLA_NOPE + half:MLA_QK].set(inv_m)
    t = ROW_TILE
    tab = jax.ShapeDtypeStruct((n, LANES), F32)
    row = pl.BlockSpec((t, LANES), lambda i: (i, 0))
    vec = pl.BlockSpec((1, LANES), lambda i: (0, 0))
    return pl.pallas_call(
        _rope_kernel,
        out_shape=(tab,) * 4,
        grid=(n // t,),
        in_specs=[pl.BlockSpec((t, 1), lambda i: (i, 0)), vec, vec],
        out_specs=(row,) * 4,
        compiler_params=_params(("parallel",)),
        name="rope",
    )(positions.reshape(n, 1), inv_r.reshape(1, LANES), invm.reshape(1, LANES))


EVEN_COLS = 4 * 512 + MLA_Q_LORA + 4 * LANES


def _split_bf16(x):
    hi = x.astype(BF16)
    return hi, (x - hi.astype(F32)).astype(BF16)


def _head_rsqrt(x, ind_ref, spread_ref):
    hi, lo = _split_bf16(x * x)
    s = _dot(hi, ind_ref[...]) + _dot(lo, ind_ref[...])
    hi, lo = _split_bf16(lax.rsqrt(s * (1.0 / MLA_QK) + EPS))
    return _dot(hi, spread_ref[...]) + _dot(lo, spread_ref[...])


def _in_even_kernel(x_ref, g_ref, sc_ref, sh_ref, w_ref, qng_ref, kvng_ref, wqb_ref, wqs_ref, wkn_ref, wv_ref,
                    qg_ref, qgs_ref, kg_ref, kgs_ref, ind_ref, spread_ref, cm_ref, sm_ref,
                    qk_ref, v_ref, o_ref, misc_ref, q_out, k_out, v_out):
    h = _norm_mod(x_ref[...], g_ref[...], sc_ref[0], sh_ref[0]).astype(BF16)
    qk_ref[...] = _dot(h, w_ref[:, 0:1024])
    v_ref[...] = _dot(h, w_ref[:, 1024:1536]).astype(BF16)
    o_ref[...] = _dot(h, w_ref[:, 1536:2048])
    rest = _dot(h, w_ref[:, 2048:EVEN_COLS])
    q_a = rest[:, 0:256]
    kv_a = rest[:, 256:384]
    krp = rest[:, 384:512]
    krs = rest[:, 512:640]
    misc_ref[...] = rest[:, 640:768]
    qn = (_rms(q_a) * qng_ref[...]).astype(BF16)
    kvn = (_rms(kv_a) * kvng_ref[...]).astype(BF16)
    q = _dot(qn, wqb_ref[...])
    qs = _dot(qn, wqs_ref[...])
    k = _dot(kvn, wkn_ref[...]) + jnp.concatenate([krp] * MLA_HEADS, axis=1)
    v_out[...] = _dot(kvn, wv_ref[...]).astype(BF16)
    q_rs = _head_rsqrt(q, ind_ref, spread_ref)
    k_rs = _head_rsqrt(k, ind_ref, spread_ref)
    cm, sm = cm_ref[...], sm_ref[...]
    q_c, q_s = qg_ref[...] * cm, qgs_ref[...] * sm
    k_c, k_s = kg_ref[...] * cm, kgs_ref[...] * sm
    ks = krs * k_s
    for hd in range(MLA_HEADS):
        sl = slice(hd * MLA_PAD, (hd + 1) * MLA_PAD)
        q_out[:, sl] = ((q[:, sl] * q_c + qs[:, sl] * q_s) * q_rs[:, sl]).astype(BF16)
        k_out[:, sl] = ((k[:, sl] * k_c + ks) * k_rs[:, sl]).astype(BF16)


def _in_even(x, mod, b_of, norm_g, w, qng, kvng, wqb, wqs, wkn, wv, qg, qgs, kg, kgs, ind, spread, cm, sm):
    n = x.shape[0]
    t = ROW_TILE
    full = lambda shape: pl.BlockSpec(shape, lambda i: (0,) * len(shape))
    row = lambda c: pl.BlockSpec((t, c), lambda i: (i, 0))
    out_shape = (
        jax.ShapeDtypeStruct((n, 1024), F32),
        jax.ShapeDtypeStruct((n, 512), BF16),
        jax.ShapeDtypeStruct((n, 512), F32),
        jax.ShapeDtypeStruct((n, LANES), F32),
        jax.ShapeDtypeStruct((n, MLA_W), BF16),
        jax.ShapeDtypeStruct((n, MLA_W), BF16),
        jax.ShapeDtypeStruct((n, 512), BF16),
    )
    vec = full((1, LANES))
    return pl.pallas_call(
        _in_even_kernel,
        out_shape=out_shape,
        grid=(n // t,),
        in_specs=[
            row(D), full((1, D)),
            pl.BlockSpec((1, 1, D), lambda i: (b_of(i, t) * 6 + 1, 0, 0)),
            pl.BlockSpec((1, 1, D), lambda i: (b_of(i, t) * 6 + 0, 0, 0)),
            full((D, EVEN_COLS)), full((1, 256)), full((1, 128)), full((256, MLA_W)), full((256, MLA_W)),
            full((128, MLA_W)), full((128, 512)), vec, vec, vec, vec, full((MLA_W, LANES)), full((LANES, MLA_W)),
            row(LANES), row(LANES),
        ],
        out_specs=(row(1024), row(512), row(512), row(LANES), row(MLA_W), row(MLA_W), row(512)),
        compiler_params=_params(("parallel",)),
        name="in_even",
    )(x, norm_g, mod, mod, w, qng, kvng, wqb, wqs, wkn, wv, qg, qgs, kg, kgs, ind, spread, cm, sm)


def _conv_kernel(seq_tiles, prev_ref, x_ref, next_ref, w_ref, b_ref, s_ref, o_ref, buf):
    i = pl.program_id(0)
    t = x_ref.shape[0]
    first = (i % seq_tiles) == 0
    last = (i % seq_tiles) == seq_tiles - 1
    buf[0:8, :] = jnp.where(first, 0.0, prev_ref[...])
    buf[8:8 + t, :] = x_ref[...]
    buf[8 + t:16 + t, :] = jnp.where(last, 0.0, next_ref[...])
    pad = ML_CONV // 2
    acc = b_ref[...] + w_ref[0:1, :] * buf[8 - pad:8 - pad + t, :]
    for k in range(1, ML_CONV):
        acc = acc + w_ref[k:k + 1, :] * buf[8 - pad + k:8 - pad + k + t, :]
    o_ref[...] = (acc * jax.nn.sigmoid(acc) * s_ref[...]).astype(BF16)


def _conv(qk_pre, conv_w, conv_b, seq):
    n, c = qk_pre.shape
    t = ROW_TILE
    nb8 = n // 8
    scale = jnp.concatenate([jnp.ones((512,), F32), jnp.full((512,), ML_DK ** -0.5, F32)]).reshape(1, c)
    return pl.pallas_call(
        functools.partial(_conv_kernel, seq // t),
        out_shape=jax.ShapeDtypeStruct((n, c), BF16),
        grid=(n // t,),
        in_specs=[
            pl.BlockSpec((8, c), lambda i: (jnp.maximum(i * (t // 8) - 1, 0), 0)),
            pl.BlockSpec((t, c), lambda i: (i, 0)),
            pl.BlockSpec((8, c), lambda i: (jnp.minimum((i + 1) * (t // 8), nb8 - 1), 0)),
            pl.BlockSpec((ML_CONV, c), lambda i: (0, 0)),
            pl.BlockSpec((1, c), lambda i: (0, 0)),
            pl.BlockSpec((1, c), lambda i: (0, 0)),
        ],
        out_specs=pl.BlockSpec((t, c), lambda i: (i, 0)),
        scratch_shapes=[pltpu.VMEM((t + 16, c), F32)],
        compiler_params=_params(("parallel",)),
        name="conv",
    )(qk_pre, qk_pre, qk_pre, conv_w, conv_b.reshape(1, c), scale)


def _mlstm_kernel(qf_ref, kf_ref, vf_ref, qb_ref, kb_ref, vb_ref, gcf_ref, gcb_ref, grf_ref, grb_ref,
                  bc_ref, br_ref, of_ref, ob_ref, c_ref, m_ref):
    c = pl.program_id(1)
    L = ML_CHUNK

    @pl.when(c == 0)
    def _():
        c_ref[...] = jnp.zeros_like(c_ref)
        m_ref[...] = jnp.zeros_like(m_ref)

    row = lax.broadcasted_iota(jnp.int32, (L, L), 0)
    col = lax.broadcasted_iota(jnp.int32, (L, L), 1)
    lane = lax.broadcasted_iota(jnp.int32, (L, ML_DV), 1)
    ones_col = jnp.where(lane == 0, 1.0, 0.0).astype(BF16)
    dirs = ((qf_ref, kf_ref, vf_ref, gcf_ref, grf_ref, of_ref, row >= col, row <= col),
            (qb_ref, kb_ref, vb_ref, gcb_ref, grb_ref, ob_ref, row <= col, row >= col))
    for d, (q_ref, k_ref, v_ref, gc_ref, gr_ref, o_ref, causal, causal_t) in enumerate(dirs):
        gc = gc_ref[0] + bc_ref[d]
        gr = gr_ref[0] + br_ref[d]
        i_col = gc[:, 0:ML_HEADS]
        f_col = _log_sigmoid(gc[:, ML_HEADS:2 * ML_HEADS])
        i_row = gr[0:ML_HEADS, :]
        f_row = _log_sigmoid(gr[ML_HEADS:2 * ML_HEADS, :])
        bcum_col = jnp.dot(jnp.where(causal, 1.0, 0.0), f_col, precision=HIGHEST, preferred_element_type=F32)
        bcum_row = jnp.dot(f_row, jnp.where(causal_t, 1.0, 0.0), precision=HIGHEST, preferred_element_type=F32)
        for h in range(ML_HEADS):
            sl = slice(h * ML_DK, (h + 1) * ML_DK)
            st = d * ML_HEADS + h
            q = q_ref[:, sl]
            k = k_ref[:, sl]
            vext = jnp.concatenate([v_ref[:, sl], ones_col], axis=1)
            bc = bcum_col[:, h:h + 1]
            br = bcum_row[h:h + 1, :]
            ir = i_row[h:h + 1, :]
            ic = i_col[:, h:h + 1]
            m_prev = m_ref[st:st + 1, 0:1]
            log_d = jnp.where(causal, bc - br + ir, -jnp.inf)
            m_inter = bc + m_prev
            m_t = jnp.maximum(m_inter, jnp.max(log_d, axis=-1, keepdims=True))
            w = (_dot_nt(q, k) * jnp.exp(log_d - m_t)).astype(BF16)
            a_inter = jnp.exp(m_inter - m_t)
            tot = _dot(w, vext) + a_inter * _dot(q, c_ref[st].astype(BF16))
            den = tot[:, ML_DV:ML_DV + 1]
            o_ref[:, sl] = tot[:, 0:ML_DV] / jnp.maximum(jnp.abs(den), jnp.exp(-m_t))
            total = jnp.sum(f_row[h:h + 1, :], axis=-1, keepdims=True)
            g_row = total - br + ir
            m_new = jnp.maximum(total + m_prev, jnp.max(g_row, axis=-1, keepdims=True))
            a_state = jnp.exp(total + m_prev - m_new)
            kw = (k.astype(F32) * jnp.exp(total - bc + ic - m_new)).astype(BF16)
            c_ref[st] = a_state * c_ref[st] + _dot_tn(kw, vext)
            m_ref[st:st + 1, :] = jnp.broadcast_to(m_new, (1, LANES))


def _mlstm(qk, v, gcol, grow, bcol, brow, batch, seq):
    n = qk.shape[0]
    L = ML_CHUNK
    nc = seq // L
    fw = lambda b, c: b * nc + c
    bw = lambda b, c: b * nc + nc - 1 - c
    out = jax.ShapeDtypeStruct((n, ML_HEADS * ML_DV), F32)
    return pl.pallas_call(
        _mlstm_kernel,
        out_shape=(out, out),
        grid=(batch, nc),
        in_specs=[
            pl.BlockSpec((L, 512), lambda b, c: (fw(b, c), 0)),
            pl.BlockSpec((L, 512), lambda b, c: (fw(b, c), 1)),
            pl.BlockSpec((L, 512), lambda b, c: (fw(b, c), 0)),
            pl.BlockSpec((L, 512), lambda b, c: (bw(b, c), 0)),
            pl.BlockSpec((L, 512), lambda b, c: (bw(b, c), 1)),
            pl.BlockSpec((L, 512), lambda b, c: (bw(b, c), 0)),
            pl.BlockSpec((1, L, 8), lambda b, c: (0, fw(b, c), 0)),
            pl.BlockSpec((1, L, 8), lambda b, c: (1, bw(b, c), 0)),
            pl.BlockSpec((1, 8, L), lambda b, c: (0, 0, fw(b, c))),
            pl.BlockSpec((1, 8, L), lambda b, c: (1, 0, bw(b, c))),
            pl.BlockSpec((2, 1, 8), lambda b, c: (0, 0, 0)),
            pl.BlockSpec((2, 8, 1), lambda b, c: (0, 0, 0)),
        ],
        out_specs=(pl.BlockSpec((L, 512), lambda b, c: (fw(b, c), 0)),
                   pl.BlockSpec((L, 512), lambda b, c: (bw(b, c), 0))),
        scratch_shapes=[pltpu.VMEM((2 * ML_HEADS, ML_DK, 2 * ML_DV), F32), pltpu.VMEM((8, LANES), F32)],
        compiler_params=_params(("parallel", "arbitrary")),
        name="mlstm",
    )(qk, qk, v, qk, qk, v, gcol, gcol, grow, grow, bcol, brow)


def _attn_kernel(q_ref, k_ref, v_ref, o_ref):
    outs = []
    for j in range(2):
        q = q_ref[:, j * MLA_PAD:(j + 1) * MLA_PAD]
        k = k_ref[:, j * MLA_PAD:(j + 1) * MLA_PAD]
        v = v_ref[:, j * MLA_V:(j + 1) * MLA_V]
        s = _dot_nt(q, k)
        p = jnp.exp(s - jnp.max(s, axis=-1, keepdims=True))
        l = jnp.sum(p, axis=-1, keepdims=True)
        outs.append(_dot(p.astype(BF16), v) / l)
    o_ref[...] = jnp.concatenate(outs, axis=1).astype(BF16)


def _attn(q, k, v, batch, seq):
    n = q.shape[0]
    t = min(ATTN_Q_TILE, seq)
    nq = seq // t
    return pl.pallas_call(
        _attn_kernel,
        out_shape=jax.ShapeDtypeStruct((n, MLA_HEADS * MLA_V), BF16),
        grid=(batch, MLA_HEADS // 2, nq),
        in_specs=[
            pl.BlockSpec((t, 2 * MLA_PAD), lambda b, h, i: (b * nq + i, h)),
            pl.BlockSpec((seq, 2 * MLA_PAD), lambda b, h, i: (b, h)),
            pl.BlockSpec((seq, 2 * MLA_V), lambda b, h, i: (b, h)),
        ],
        out_specs=pl.BlockSpec((t, 2 * MLA_V), lambda b, h, i: (b * nq + i, h)),
        compiler_params=_params(("parallel", "parallel", "arbitrary")),
        name="attn",
    )(q, k, v)


def _mixer_tail(y, x_ref, g1_ref, fg_ref, sc2_ref, sh2_ref, rw_ref, rb_ref, x_out, hf_out, lg_out):
    x1 = x_ref[...] + g1_ref[0] * y
    x_out[...] = x1
    hf = _norm_mod(x1, fg_ref[...], sc2_ref[0], sh2_ref[0])
    hf_out[...] = hf
    lg_out[...] = jnp.dot(hf, rw_ref[...], precision=HIGHEST, preferred_element_type=F32) + rb_ref[...]


def _tail_specs(n, t, b_of):
    full = lambda shape: pl.BlockSpec(shape, lambda i: (0,) * len(shape))
    mod = lambda j: pl.BlockSpec((1, 1, D), lambda i: (b_of(i, t) * 6 + j, 0, 0))
    row = lambda c: pl.BlockSpec((t, c), lambda i: (i, 0))
    in_specs = [row(D), mod(2), full((1, D)), mod(4), mod(3), full((D, LANES)), full((1, LANES))]
    out_shape = (jax.ShapeDtypeStruct((n, D), F32), jax.ShapeDtypeStruct((n, D), F32),
                 jax.ShapeDtypeStruct((n, LANES), F32))
    out_specs = (row(D), row(D), row(LANES))
    return in_specs, out_shape, out_specs


def _out_even_kernel(hf_ref, hb_ref, om_ref, oa_ref, mg_ref, wt_ref, wb_ref, *tail):
    hs = hf_ref[...] + hb_ref[...]
    gate = jax.nn.sigmoid(om_ref[...])
    parts = []
    for h in range(ML_HEADS):
        sl = slice(h * ML_DV, (h + 1) * ML_DV)
        parts.append(_rms(hs[:, sl]) * mg_ref[:, sl] * gate[:, sl])
    hml = jnp.concatenate(parts, axis=1).astype(BF16)
    y = _dot(hml, wt_ref[...]) + _dot(oa_ref[...], wb_ref[...])
    _mixer_tail(y, *tail)


def _out_even(h_f, h_b, o_m, o_mla, ml_g, w_top, w_bot, x, mod, b_of, ffn_g, rw, rb):
    n = x.shape[0]
    t = ROW_TILE
    full = lambda shape: pl.BlockSpec(shape, lambda i: (0,) * len(shape))
    row = lambda c: pl.BlockSpec((t, c), lambda i: (i, 0))
    tin, out_shape, out_specs = _tail_specs(n, t, b_of)
    return pl.pallas_call(
        _out_even_kernel,
        out_shape=out_shape,
        grid=(n // t,),
        in_specs=[row(512), row(512), row(512), row(512), full((1, 512)), full((512, D)), full((512, D))] + tin,
        out_specs=out_specs,
        compiler_params=_params(("parallel",)),
        name="out_even",
    )(h_f, h_b, o_m, o_mla, ml_g, w_top, w_bot, x, mod, ffn_g, mod, mod, rw, rb)


def _in_odd_kernel(x_ref, g_ref, sc_ref, sh_ref, w_ref, cr_ref, sr_ref, q_out, k_out, v_out, g_out):
    h = _norm_mod(x_ref[...], g_ref[...], sc_ref[0], sh_ref[0]).astype(BF16)
    cos, sin = cr_ref[...], sr_ref[...]
    hw = RET_DK // 2
    for idx, (dst, scale) in enumerate(((q_out, RET_DK ** -0.5), (k_out, 1.0))):
        z = _dot(h, w_ref[:, idx * 1024:(idx + 1) * 1024])
        for hd in range(RET_HEADS):
            x1 = z[:, hd * RET_DK:hd * RET_DK + hw]
            x2 = z[:, hd * RET_DK + hw:(hd + 1) * RET_DK]
            dst[:, hd * RET_DK:hd * RET_DK + hw] = ((x1 * cos - x2 * sin) * scale).astype(BF16)
            dst[:, hd * RET_DK + hw:(hd + 1) * RET_DK] = ((x2 * cos + x1 * sin) * scale).astype(BF16)
    v_out[...] = _dot(h, w_ref[:, 2048:4096]).astype(BF16)
    g_out[...] = _dot(h, w_ref[:, 4096:6144])


def _in_odd(x, mod, b_of, norm_g, w, cr, sr):
    n = x.shape[0]
    t = ROW_TILE
    full = lambda shape: pl.BlockSpec(shape, lambda i: (0,) * len(shape))
    row = lambda c: pl.BlockSpec((t, c), lambda i: (i, 0))
    out_shape = (
        jax.ShapeDtypeStruct((n, 1024), BF16), jax.ShapeDtypeStruct((n, 1024), BF16),
        jax.ShapeDtypeStruct((n, 2048), BF16), jax.ShapeDtypeStruct((n, 2048), F32),
    )
    return pl.pallas_call(
        _in_odd_kernel,
        out_shape=out_shape,
        grid=(n // t,),
        in_specs=[
            row(D), full((1, D)),
            pl.BlockSpec((1, 1, D), lambda i: (b_of(i, t) * 6 + 1, 0, 0)),
            pl.BlockSpec((1, 1, D), lambda i: (b_of(i, t) * 6 + 0, 0, 0)),
            full((D, 6144)), row(LANES), row(LANES),
        ],
        out_specs=(row(1024), row(1024), row(2048), row(2048)),
        compiler_params=_params(("parallel",)),
        name="in_odd",
    )(x, norm_g, mod, mod, w, cr, sr)


def _ret_kernel(qf_ref, kf_ref, vf_ref, qb_ref, kb_ref, vb_ref, dec_ref, of_ref, ob_ref, s_ref):
    c = pl.program_id(1)
    L = RET_CHUNK

    @pl.when(c == 0)
    def _():
        s_ref[...] = jnp.zeros_like(s_ref)

    row = lax.broadcasted_iota(jnp.int32, (L, L), 0)
    col = lax.broadcasted_iota(jnp.int32, (L, L), 1)
    pos = lax.broadcasted_iota(jnp.int32, (L, 1), 0)
    dirs = ((qf_ref, kf_ref, vf_ref, of_ref, row - col, pos + 1, L - 1 - pos),
            (qb_ref, kb_ref, vb_ref, ob_ref, col - row, L - pos, pos))
    for d, (q_ref, k_ref, v_ref, o_ref, rel_i, exp_q_i, exp_k_i) in enumerate(dirs):
        rel = rel_i.astype(F32)
        exp_q = exp_q_i.astype(F32)
        exp_k = exp_k_i.astype(F32)
        for h in range(RET_HEADS):
            st = d * RET_HEADS + h
            lg = _log_sigmoid(dec_ref[d, h:h + 1, :])
            lg1 = lg[:, 0:1]
            decay_in = jnp.where(rel >= 0, jnp.exp(jnp.maximum(rel, 0.0) * lg1), 0.0)
            q = q_ref[:, h * RET_DK:(h + 1) * RET_DK]
            k = k_ref[:, h * RET_DK:(h + 1) * RET_DK]
            v = v_ref[:, h * RET_DV:(h + 1) * RET_DV]
            sc = (_dot_nt(q, k) * decay_in).astype(BF16)
            o_ref[:, h * RET_DV:(h + 1) * RET_DV] = (
                _dot(sc, v) + jnp.exp(exp_q * lg1) * _dot(q, s_ref[st].astype(BF16)))
            kd = (k.astype(F32) * jnp.exp(exp_k * lg1)).astype(BF16)
            s_ref[st] = jnp.exp(L * lg1) * s_ref[st] + _dot_tn(kd, v)


def _retention(q, k, v, dec, batch, seq):
    n = q.shape[0]
    L = RET_CHUNK
    nc = seq // L
    fw = lambda b, c: (b * nc + c, 0)
    bw = lambda b, c: (b * nc + nc - 1 - c, 0)
    out = jax.ShapeDtypeStruct((n, RET_HEADS * RET_DV), F32)
    return pl.pallas_call(
        _ret_kernel,
        out_shape=(out, out),
        grid=(batch, nc),
        in_specs=[
            pl.BlockSpec((L, 1024), fw), pl.BlockSpec((L, 1024), fw), pl.BlockSpec((L, 2048), fw),
            pl.BlockSpec((L, 1024), bw), pl.BlockSpec((L, 1024), bw), pl.BlockSpec((L, 2048), bw),
            pl.BlockSpec((2, RET_HEADS, LANES), lambda b, c: (0, 0, 0)),
        ],
        out_specs=(pl.BlockSpec((L, 2048), fw), pl.BlockSpec((L, 2048), bw)),
        scratch_shapes=[pltpu.VMEM((2 * RET_HEADS, RET_DK, RET_DV), F32)],
        compiler_params=_params(("parallel", "arbitrary")),
        name="retention",
    )(q, k, v, q, k, v, dec)


def _out_odd_kernel(yf_ref, yb_ref, g_ref, gn_ref, w_ref, *tail):
    ys = yf_ref[...] + yb_ref[...]
    g = g_ref[...]
    gate = g * jax.nn.sigmoid(g)
    parts = []
    for h in range(RET_HEADS):
        sl = slice(h * RET_DV, (h + 1) * RET_DV)
        yh = ys[:, sl]
        yc = yh - jnp.mean(yh, axis=-1, keepdims=True)
        var = jnp.mean(yc * yc, axis=-1, keepdims=True)
        parts.append(yc * lax.rsqrt(var + EPS) * gn_ref[:, sl] * gate[:, sl])
    yn = jnp.concatenate(parts, axis=1).astype(BF16)
    _mixer_tail(_dot(yn, w_ref[...]), *tail)


def _out_odd(y_f, y_b, g, gn_g, w_out, x, mod, b_of, ffn_g, rw, rb):
    n = x.shape[0]
    t = ROW_TILE
    full = lambda shape: pl.BlockSpec(shape, lambda i: (0,) * len(shape))
    row = lambda c: pl.BlockSpec((t, c), lambda i: (i, 0))
    tin, out_shape, out_specs = _tail_specs(n, t, b_of)
    return pl.pallas_call(
        _out_odd_kernel,
        out_shape=out_shape,
        grid=(n // t,),
        in_specs=[row(2048), row(2048), row(2048), full((1, 2048)), full((2048, D))] + tin,
        out_specs=out_specs,
        compiler_params=_params(("parallel",)),
        name="out_odd",
    )(y_f, y_b, g, gn_g, w_out, x, mod, ffn_g, mod, mod, rw, rb)


def _route_kernel(lg_ref, e_out, r_out, w_out, cnt_out, base_ref):
    i = pl.program_id(0)
    t = lg_ref.shape[0]

    @pl.when(i == 0)
    def _():
        base_ref[...] = jnp.zeros_like(base_ref)

    l = lg_ref[...]
    lane = lax.broadcasted_iota(jnp.int32, l.shape, 1)
    sel = jnp.zeros(l.shape, F32)
    vals, idxs, hots = [], [], []
    for _k in range(TOP_K):
        m = jnp.max(l, axis=-1, keepdims=True)
        idx = jnp.min(jnp.where(l == m, lane, LANES), axis=-1, keepdims=True)
        hot = lane == idx
        vals.append(m)
        idxs.append(idx)
        hots.append(hot)
        sel = sel + jnp.where(hot, 1.0, 0.0)
        l = jnp.where(hot, -jnp.inf, l)
    es = [jnp.exp(v - vals[0]) for v in vals]
    den = es[0] + es[1] + es[2] + es[3]
    row = lax.broadcasted_iota(jnp.int32, (t, t), 0)
    col = lax.broadcasted_iota(jnp.int32, (t, t), 1)
    before = jnp.where(row > col, 1.0, 0.0).astype(BF16)
    rank_mat = _dot(before, sel.astype(BF16)) + base_ref[...]
    e_acc = jnp.zeros(l.shape, jnp.int32)
    r_acc = jnp.zeros(l.shape, jnp.int32)
    w_acc = jnp.zeros(l.shape, F32)
    for k in range(TOP_K):
        rk = jnp.sum(jnp.where(hots[k], rank_mat, 0.0), axis=-1, keepdims=True).astype(jnp.int32)
        e_acc = jnp.where(lane == k, idxs[k], e_acc)
        r_acc = jnp.where(lane == k, rk, r_acc)
        w_acc = jnp.where(lane == k, es[k] / den, w_acc)
    e_out[...] = e_acc
    r_out[...] = r_acc
    w_out[...] = w_acc
    base_ref[...] = base_ref[...] + jnp.sum(sel, axis=0, keepdims=True)
    cnt_out[...] = base_ref[...]


def _route(logits):
    n = logits.shape[0]
    t = ROUTE_TILE
    row = pl.BlockSpec((t, LANES), lambda i: (i, 0))
    return pl.pallas_call(
        _route_kernel,
        out_shape=(jax.ShapeDtypeStruct((n, LANES), jnp.int32), jax.ShapeDtypeStruct((n, LANES), jnp.int32),
                   jax.ShapeDtypeStruct((n, LANES), F32), jax.ShapeDtypeStruct((1, LANES), F32)),
        grid=(n // t,),
        in_specs=[row],
        out_specs=(row, row, row, pl.BlockSpec((1, LANES), lambda i: (0, 0))),
        scratch_shapes=[pltpu.VMEM((1, LANES), F32)],
        compiler_params=_params(("arbitrary",)),
        name="route",
    )(logits)


def _expert_kernel(be_ref, nu_ref, tok0_ref, tok_ref, slot_ref, slotl_ref, hf_ref, wgu_ref, bgu_ref, wd_ref,
                   bd_ref, yu_ref, xa, xb, ya, yb, wgu_bf, wd_bf, gsem, ssem):
    i = pl.program_id(0)
    last = pl.num_programs(0) - 1
    rows = MOE_BLOCK

    def gather_row(idx_ref, r, xdst):
        return pltpu.make_async_copy(hf_ref.at[pl.ds(idx_ref[0, 0, r], 1)], xdst.at[pl.ds(r, 1)], gsem)

    def scatter_row(idx_ref, r, ysrc):
        return pltpu.make_async_copy(ysrc.at[pl.ds(r, 1)], yu_ref.at[pl.ds(idx_ref[0, 0, r], 1)], ssem)

    def wait_gather(xdst):
        pltpu.make_async_copy(hf_ref.at[pl.ds(0, rows)], xdst, gsem).wait()

    def wait_scatter(ysrc):
        pltpu.make_async_copy(ysrc, yu_ref.at[pl.ds(0, rows)], ssem).wait()

    def looped(row_copy, idx_ref, buf):
        def body(g, carry):
            for u in range(MOVE_UNROLL * TOP_K):
                row_copy(idx_ref, g * (MOVE_UNROLL * TOP_K) + u, buf).start()
            return carry
        lax.fori_loop(0, rows // (MOVE_UNROLL * TOP_K), body, 0)

    @pl.when(i == 0)
    def _():
        yb[...] = jnp.zeros_like(yb)
        looped(gather_row, tok0_ref, xa)
        wait_gather(xa)

    fresh = jnp.logical_or(i == 0, be_ref[i] != be_ref[jnp.maximum(i - 1, 0)])

    @pl.when(fresh)
    def _():
        wgu_bf[...] = wgu_ref[0, 0].astype(BF16)
        wd_bf[...] = wd_ref[0, 0].astype(BF16)

    def step(xc, yc, xn, yn):
        @pl.when(i < nu_ref[0])
        def _():
            for r in range(rows):
                scatter_row(slot_ref, r, yn).start()
                gather_row(tok_ref, r, xn).start()
            gu = _dot(xc[...].astype(BF16), wgu_bf[...]) + bgu_ref[0, 0]
            gate = jnp.minimum(gu[:, :EXPERT_FF], SWIGLU_LIMIT)
            up = jnp.clip(gu[:, EXPERT_FF:], -SWIGLU_LIMIT, SWIGLU_LIMIT)
            act = (up + 1.0) * gate * jax.nn.sigmoid(SWIGLU_ALPHA * gate)
            yc[...] = _dot(act.astype(BF16), wd_bf[...]) + bd_ref[0, 0]

        @pl.when(i >= nu_ref[0])
        def _():
            looped(scatter_row, slot_ref, yn)
            looped(gather_row, tok_ref, xn)
            yc[...] = jnp.zeros_like(yc)

        wait_scatter(yn)
        wait_gather(xn)

        @pl.when(i == last)
        def _():
            looped(scatter_row, slotl_ref, yc)
            wait_scatter(yc)

    @pl.when(i % 2 == 0)
    def _():
        step(xa, ya, xb, yb)

    @pl.when(i % 2 == 1)
    def _():
        step(xb, yb, xa, ya)


def _experts(hf, nb, layer, blk_expert, n_used, tok_of_row, slot_of_row, out_rows, w_gu, b_gu, w_down, b_down):
    idx = lambda f: pl.BlockSpec((1, 1, MOE_BLOCK), lambda i, be, nu: (f(i), 0, 0), memory_space=pltpu.SMEM)
    grid_spec = pltpu.PrefetchScalarGridSpec(
        num_scalar_prefetch=2,
        grid=(nb,),
        in_specs=[
            idx(lambda i: 0),
            idx(lambda i: jnp.minimum(i + 1, nb - 1)),
            idx(lambda i: jnp.maximum(i - 1, 0)),
            idx(lambda i: nb - 1),
            pl.BlockSpec(memory_space=pl.ANY),
            pl.BlockSpec((1, 1, D, 2 * EXPERT_FF), lambda i, be, nu: (layer, be[i], 0, 0)),
            pl.BlockSpec((1, 1, 1, 2 * EXPERT_FF), lambda i, be, nu: (layer, be[i], 0, 0)),
            pl.BlockSpec((1, 1, EXPERT_FF, D), lambda i, be, nu: (layer, be[i], 0, 0)),
            pl.BlockSpec((1, 1, 1, D), lambda i, be, nu: (layer, be[i], 0, 0)),
        ],
        out_specs=pl.BlockSpec(memory_space=pl.ANY),
        scratch_shapes=[pltpu.VMEM((MOE_BLOCK, D), F32), pltpu.VMEM((MOE_BLOCK, D), F32),
                        pltpu.VMEM((MOE_BLOCK, D), F32), pltpu.VMEM((MOE_BLOCK, D), F32),
                        pltpu.VMEM((D, 2 * EXPERT_FF), BF16), pltpu.VMEM((EXPERT_FF, D), BF16),
                        pltpu.SemaphoreType.DMA(()), pltpu.SemaphoreType.DMA(())],
    )
    tok3 = tok_of_row.reshape(nb, 1, MOE_BLOCK)
    slot3 = slot_of_row.reshape(nb, 1, MOE_BLOCK)
    return pl.pallas_call(
        _expert_kernel,
        out_shape=jax.ShapeDtypeStruct((out_rows, D), F32),
        grid_spec=grid_spec,
        compiler_params=_params(("arbitrary",), has_side_effects=True),
        name="experts",
    )(blk_expert, n_used, tok3, tok3, slot3, slot3, hf, w_gu, b_gu.reshape(DEPTH, N_EXPERTS, 1, 2 * EXPERT_FF),
      w_down, b_down.reshape(DEPTH, N_EXPERTS, 1, D))


def _combine_kernel(yu_ref, wt_ref, x_ref, g2_ref, o_ref):
    t = x_ref.shape[0]
    wt = wt_ref[...]
    acc = yu_ref[0:t, :] * wt[:, 0:1]
    for k in range(1, TOP_K):
        acc = acc + yu_ref[k * t:(k + 1) * t, :] * wt[:, k:k + 1]
    o_ref[...] = x_ref[...] + g2_ref[0] * acc


def _combine(yu, wts, x, mod, b_of):
    n = x.shape[0]
    t = COMBINE_TILE
    return pl.pallas_call(
        _combine_kernel,
        out_shape=jax.ShapeDtypeStruct((n, D), F32),
        grid=(n // t,),
        in_specs=[
            pl.BlockSpec((TOP_K * t, D), lambda i: (i, 0)),
            pl.BlockSpec((t, LANES), lambda i: (i, 0)),
            pl.BlockSpec((t, D), lambda i: (i, 0)),
            pl.BlockSpec((1, 1, D), lambda i: (b_of(i, t) * 6 + 5, 0, 0)),
        ],
        out_specs=pl.BlockSpec((t, D), lambda i: (i, 0)),
        compiler_params=_params(("parallel",)),
        name="combine",
    )(yu, wts, x, mod)


def _moe(x1, hf, logits, mod, b_of, layer, w_gu, b_gu, w_down, b_down):
    n = x1.shape[0]
    eidx, rank, wts, counts = _route(logits)
    experts = jnp.arange(N_EXPERTS, dtype=jnp.int32)
    cnt = counts[0, :N_EXPERTS].astype(jnp.int32)
    padded = (cnt + MOE_BLOCK - 1) // MOE_BLOCK * MOE_BLOCK
    pend = jnp.cumsum(padded)
    pstart = pend - padded
    e4 = eidx[:, :TOP_K]
    dest = rank[:, :TOP_K] + jnp.sum(jnp.where(e4[:, :, None] == experts, pstart, 0), axis=-1)
    nb = -(-(n * TOP_K) // MOE_BLOCK) + N_EXPERTS
    blk_start = jnp.arange(nb, dtype=jnp.int32) * MOE_BLOCK
    blk_expert = jnp.minimum(jnp.sum((pend[None, :] <= blk_start[:, None]).astype(jnp.int32), axis=1),
                             N_EXPERTS - 1)
    n_used = (pend[-1:] // MOE_BLOCK).astype(jnp.int32)
    n_free = nb * MOE_BLOCK - n * TOP_K
    free_end = jnp.cumsum(padded - cnt)
    seg_first = jnp.concatenate([pstart + cnt, pend[-1:]])
    seg_skip = jnp.concatenate([free_end - (padded - cnt), free_end[-1:]])
    jj = jnp.arange(n_free, dtype=jnp.int32)
    seg = jnp.sum((free_end[None, :] <= jj[:, None]).astype(jnp.int32), axis=1)
    hot = seg[:, None] == jnp.arange(N_EXPERTS + 1, dtype=jnp.int32)
    pad_dest = jj + jnp.sum(jnp.where(hot, seg_first - seg_skip, 0), axis=1)
    t = COMBINE_TILE
    tok = jnp.arange(n, dtype=jnp.int32)[:, None]
    kk = jnp.arange(TOP_K, dtype=jnp.int32)[None, :]
    slot = (tok // t) * (TOP_K * t) + kk * t + tok % t
    all_rows = jnp.concatenate([dest.reshape(-1), pad_dest])
    all_slots = jnp.concatenate([slot.reshape(-1), n * TOP_K + jj])
    slot_of_row = jnp.zeros((nb * MOE_BLOCK,), jnp.int32).at[all_rows].set(all_slots, unique_indices=True)
    in_tile = slot_of_row % (TOP_K * t)
    tok_of_row = jnp.where(slot_of_row < n * TOP_K, slot_of_row // (TOP_K * t) * t + in_tile % t, 0)
    yu = _experts(hf, nb, layer, blk_expert, n_used, tok_of_row, slot_of_row, n * TOP_K + n_free,
                  w_gu, b_gu, w_down, b_down)
    return _combine(yu, wts, x1, mod, b_of)


def _head_tile(t, partner):
    half = MLA_ROPE // 2
    lead = t.shape[:-1]
    tail = jnp.zeros(lead + (MLA_PAD - MLA_QK,), t.dtype)
    if not partner:
        return jnp.concatenate([t, tail], axis=-1)
    return jnp.concatenate([jnp.zeros(lead + (MLA_NOPE,), t.dtype), t[..., MLA_NOPE + half:],
                            t[..., MLA_NOPE:MLA_NOPE + half], tail], axis=-1)


def _prep_even(w_in, gate_b, w_qb, w_kvb, qk_q_g, qk_k_g, w_out):
    q_m, k_m, v_m, o_m, gates, q_a, kv_a, k_r = jnp.split(
        w_in, [512, 1024, 1536, 2048, 2064, 2320, 2448], axis=1)
    zeros = lambda c: jnp.zeros((D, c), F32)
    k_r96 = jnp.concatenate([zeros(MLA_NOPE), k_r], axis=1)
    misc = jnp.concatenate([gates, zeros(LANES - 16)], axis=1)
    w = jnp.concatenate([q_m, k_m, v_m, o_m, q_a, kv_a, _head_tile(k_r96, False), _head_tile(k_r96, True), misc],
                        axis=1).astype(BF16)
    qb = w_qb.reshape(MLA_Q_LORA, MLA_HEADS, MLA_QK)
    wqb = _head_tile(qb, False).reshape(MLA_Q_LORA, MLA_W).astype(BF16)
    wqs = _head_tile(qb, True).reshape(MLA_Q_LORA, MLA_W).astype(BF16)
    kvb = w_kvb.reshape(MLA_KV_LORA, MLA_HEADS, MLA_NOPE + MLA_V)
    wkn = jnp.pad(kvb[:, :, :MLA_NOPE], ((0, 0), (0, 0), (0, MLA_PAD - MLA_NOPE)))
    wkn = wkn.reshape(MLA_KV_LORA, MLA_W).astype(BF16)
    wv = kvb[:, :, MLA_NOPE:].reshape(MLA_KV_LORA, MLA_HEADS * MLA_V).astype(BF16)
    qg96 = qk_q_g * (MLA_QK ** -0.5)
    qg = _head_tile(qg96, False).reshape(1, MLA_PAD)
    qgs = _head_tile(qg96, True).reshape(1, MLA_PAD)
    kg = _head_tile(qk_k_g, False).reshape(1, MLA_PAD)
    kgs = _head_tile(qk_k_g, True).reshape(1, MLA_PAD)
    gb = gate_b.reshape(2, 8)
    w_top = w_out[:512].astype(BF16)
    w_bot = w_out[512:].astype(BF16)
    return w, wqb, wqs, wkn, wv, qg, qgs, kg, kgs, gb.reshape(2, 1, 8), gb.reshape(2, 8, 1), w_top, w_bot


def kernel(x, c, positions, ada_w, ada_b, norm_mix_g, norm_ffn_g, hy_w_in, ml_conv_w, ml_conv_b, ml_gate_b, ml_norm_g, mla_q_norm_g, mla_kv_norm_g, mla_w_qb, mla_w_kvb, mla_qk_q_g, mla_qk_k_g, hy_w_out, ret_w_in, ret_decay_f, ret_decay_b, ret_gn_g, ret_w_out, moe_router_w, moe_router_b, moe_w_gu, moe_b_gu, moe_w_down, moe_b_down):
    batch, seq, _ = x.shape
    n = batch * seq

    def b_of(i, t):
        return (i * t) // seq

    mod_all = _ada(c, ada_w, ada_b).reshape(DEPTH, batch * 6, 1, D)
    cr, sr, cm, sm = _rope_tables(positions)
    head_of_lane = jnp.arange(MLA_W, dtype=jnp.int32) // MLA_PAD
    ind = (head_of_lane[:, None] == jnp.arange(LANES, dtype=jnp.int32)[None, :]).astype(BF16)
    spread = ind.T
    xf = x.reshape(n, D)
    for layer in range(DEPTH):
        mod = mod_all[layer]
        j = layer // 2
        mix_g = norm_mix_g[layer].reshape(1, D)
        ffn_g = norm_ffn_g[layer].reshape(1, D)
        rw = jnp.pad(moe_router_w[layer], ((0, 0), (0, LANES - N_EXPERTS)))
        rb = jnp.concatenate([moe_router_b[layer], jnp.full((LANES - N_EXPERTS,), NEG_BIG, F32)]).reshape(1, LANES)
        if layer % 2 == 0:
            w, wqb, wqs, wkn, wv, qg, qgs, kg, kgs, gbc, gbr, w_top, w_bot = _prep_even(
                hy_w_in[j], ml_gate_b[j], mla_w_qb[j], mla_w_kvb[j], mla_qk_q_g[j], mla_qk_k_g[j], hy_w_out[j])
            qk_pre, v_m, o_m, misc, q_a, k_a, v_a = _in_even(
                xf, mod, b_of, mix_g, w, mla_q_norm_g[j].reshape(1, -1), mla_kv_norm_g[j].reshape(1, -1),
                wqb, wqs, wkn, wv, qg, qgs, kg, kgs, ind, spread, cm, sm)
            qk = _conv(qk_pre, ml_conv_w[j], ml_conv_b[j], seq)
            gcol = misc[:, :16].reshape(n, 2, 8).transpose(1, 0, 2)
            grow = gcol.transpose(0, 2, 1)
            h_f, h_b = _mlstm(qk, v_m, gcol, grow, gbc, gbr, batch, seq)
            o_mla = _attn(q_a, k_a, v_a, batch, seq)
            x1, hf, logits = _out_even(h_f, h_b, o_m, o_mla, ml_norm_g[j].reshape(1, -1), w_top, w_bot,
                                       xf, mod, b_of, ffn_g, rw, rb)
        else:
            dec = jnp.broadcast_to(jnp.stack([ret_decay_f[j], ret_decay_b[j]])[:, :, None], (2, RET_HEADS, LANES))
            q_r, k_r, v_r, g_r = _in_odd(xf, mod, b_of, mix_g, ret_w_in[j].astype(BF16), cr, sr)
            y_f, y_b = _retention(q_r, k_r, v_r, dec.astype(F32), batch, seq)
            x1, hf, logits = _out_odd(y_f, y_b, g_r, ret_gn_g[j].reshape(1, -1), ret_w_out[j].astype(BF16),
                                      xf, mod, b_of, ffn_g, rw, rb)
        xf = _moe(x1, hf, logits, mod, b_of, layer, moe_w_gu, moe_b_gu, moe_w_down, moe_b_down)
    return xf.reshape(batch, seq, D)
```

```python
import functools

import jax
import jax.numpy as jnp
from jax import lax
from jax.experimental import pallas as pl
from jax.experimental.pallas import tpu as pltpu

F32 = jnp.float32
BF16 = jnp.bfloat16
HIGHEST = lax.Precision.HIGHEST

D = 1024
DEPTH = 4
EPS = 1e-6
ROPE_BASE = 10000.0

ML_HEADS = 4
ML_DK = 128
ML_DV = 128
ML_CONV = 5
ML_CHUNK = 256

MLA_HEADS = 8
MLA_Q_LORA = 256
MLA_KV_LORA = 128
MLA_NOPE = 64
MLA_ROPE = 32
MLA_V = 64
MLA_QK = MLA_NOPE + MLA_ROPE
MLA_PAD = 128
MLA_W = MLA_HEADS * MLA_PAD

RET_HEADS = 4
RET_DK = 256
RET_DV = 512
RET_CHUNK = 256

N_EXPERTS = 32
TOP_K = 4
EXPERT_FF = 1024
SWIGLU_LIMIT = 7.0
SWIGLU_ALPHA = 1.702
MOE_BLOCK = 512

LANES = 128
NEG_BIG = -1e30
LOG2_E = 1.4426950408889634

ROW_TILE = 512
ATTN_Q_TILE = 256
ROUTE_TILE = 256
MOVE_TILE = 512
MOVE_UNROLL = 8
VMEM_LIMIT = 56 * 1024 * 1024


def _params(semantics, **kw):
    return pltpu.CompilerParams(dimension_semantics=semantics, vmem_limit_bytes=VMEM_LIMIT, **kw)


def _dot(a, b):
    return jnp.dot(a, b, preferred_element_type=F32)


def _dot_nt(a, b):
    return lax.dot_general(a, b, (((1,), (1,)), ((), ())), preferred_element_type=F32)


def _dot_tn(a, b):
    return lax.dot_general(a, b, (((0,), (0,)), ((), ())), preferred_element_type=F32)


def _log_sigmoid(x):
    return jnp.minimum(x, 0.0) - jnp.log1p(jnp.exp(-jnp.abs(x)))


def _rms(x, denom=None):
    n = x.shape[-1] if denom is None else denom
    return x * lax.rsqrt(jnp.sum(x * x, axis=-1, keepdims=True) / n + EPS)


def _norm_mod(x, g, sc, sh):
    return (_rms(x) * g) * (1.0 + sc) + sh


def _ada_kernel(c_ref, w_ref, b_ref, o_ref):
    c = c_ref[...]
    cs = c * jax.nn.sigmoid(c)
    o_ref[0] = jnp.dot(cs, w_ref[0], precision=HIGHEST, preferred_element_type=F32) + b_ref[0]


def _ada(c, ada_w, ada_b):
    b = c.shape[0]
    return pl.pallas_call(
        _ada_kernel,
        out_shape=jax.ShapeDtypeStruct((DEPTH, b, 6 * D), F32),
        grid=(DEPTH, 6),
        in_specs=[
            pl.BlockSpec((b, D), lambda l, j: (0, 0)),
            pl.BlockSpec((1, D, D), lambda l, j: (l, 0, j)),
            pl.BlockSpec((1, 1, D), lambda l, j: (l, 0, j)),
        ],
        out_specs=pl.BlockSpec((1, b, D), lambda l, j: (l, 0, j)),
        compiler_params=_params(("parallel", "parallel")),
        name="ada",
    )(c, ada_w, ada_b.reshape(DEPTH, 1, 6 * D))


def _rope_kernel(pos_ref, invr_ref, invm_ref, cr_ref, sr_ref, cm_ref, sm_ref):
    pos = pos_ref[...].astype(F32)
    ang_r = pos * invr_ref[...]
    cr_ref[...] = jnp.cos(ang_r)
    sr_ref[...] = jnp.sin(ang_r)
    ang_m = pos * invm_ref[...]
    lane = lax.broadcasted_iota(jnp.int32, ang_m.shape, 1)
    s = jnp.sin(ang_m)
    cm_ref[...] = jnp.cos(ang_m)
    sm_ref[...] = jnp.where(lane < MLA_NOPE + MLA_ROPE // 2, -s, s)


def _rope_tables(positions):
    n = positions.size
    inv_r = ROPE_BASE ** (-jnp.arange(0, RET_DK, 2, dtype=F32) / RET_DK)
    inv_m = ROPE_BASE ** (-jnp.arange(0, MLA_ROPE, 2, dtype=F32) / MLA_ROPE)
    half = MLA_ROPE // 2
    invm = jnp.zeros((LANES,), F32).at[MLA_NOPE:MLA_NOPE + half].set(inv_m).at[MLA_NOPE + half:MLA_QK].set(inv_m)
    t = ROW_TILE
    tab = jax.ShapeDtypeStruct((n, LANES), F32)
    row = pl.BlockSpec((t, LANES), lambda i: (i, 0))
    vec = pl.BlockSpec((1, LANES), lambda i: (0, 0))
    return pl.pallas_call(
        _rope_kernel,
        out_shape=(tab,) * 4,
        grid=(n // t,),
        in_specs=[pl.BlockSpec((t, 1), lambda i: (i, 0)), vec, vec],
        out_specs=(row,) * 4,
        compiler_params=_params(("parallel",)),
        name="rope",
    )(positions.reshape(n, 1), inv_r.reshape(1, LANES), invm.reshape(1, LANES))


EVEN_COLS = 4 * 512 + MLA_Q_LORA + 4 * LANES


def _split_bf16(x):
    hi = x.astype(BF16)
    return hi, (x - hi.astype(F32)).astype(BF16)


def _head_rsqrt(x, ind_ref, spread_ref):
    hi, lo = _split_bf16(x * x)
    s = _dot(hi, ind_ref[...]) + _dot(lo, ind_ref[...])
    hi, lo = _split_bf16(lax.rsqrt(s * (1.0 / MLA_QK) + EPS))
    return _dot(hi, spread_ref[...]) + _dot(lo, spread_ref[...])


def _in_even_kernel(x_ref, g_ref, sc_ref, sh_ref, w_ref, qng_ref, kvng_ref, wqb_ref, wqs_ref, wkn_ref, wv_ref,
                    qg_ref, qgs_ref, kg_ref, kgs_ref, ind_ref, spread_ref, cm_ref, sm_ref,
                    qk_ref, v_ref, o_ref, misc_ref, q_out, k_out, v_out):
    h = _norm_mod(x_ref[...], g_ref[...], sc_ref[0], sh_ref[0]).astype(BF16)
    qk_ref[...] = _dot(h, w_ref[:, 0:1024])
    v_ref[...] = _dot(h, w_ref[:, 1024:1536]).astype(BF16)
    o_ref[...] = _dot(h, w_ref[:, 1536:2048])
    rest = _dot(h, w_ref[:, 2048:EVEN_COLS])
    q_a = rest[:, 0:256]
    kv_a = rest[:, 256:384]
    krp = rest[:, 384:512]
    krs = rest[:, 512:640]
    misc_ref[...] = rest[:, 640:768]
    qn = (_rms(q_a) * qng_ref[...]).astype(BF16)
    kvn = (_rms(kv_a) * kvng_ref[...]).astype(BF16)
    q = _dot(qn, wqb_ref[...])
    qs = _dot(qn, wqs_ref[...])
    k = _dot(kvn, wkn_ref[...]) + jnp.concatenate([krp] * MLA_HEADS, axis=1)
    v_out[...] = _dot(kvn, wv_ref[...]).astype(BF16)
    q_rs = _head_rsqrt(q, ind_ref, spread_ref)
    k_rs = _head_rsqrt(k, ind_ref, spread_ref)
    cm, sm = cm_ref[...], sm_ref[...]
    q_c, q_s = qg_ref[...] * cm, qgs_ref[...] * sm
    k_c, k_s = kg_ref[...] * cm, kgs_ref[...] * sm
    ks = krs * k_s
    for hd in range(MLA_HEADS):
        sl = slice(hd * MLA_PAD, (hd + 1) * MLA_PAD)
        q_out[:, sl] = ((q[:, sl] * q_c + qs[:, sl] * q_s) * q_rs[:, sl]).astype(BF16)
        k_out[:, sl] = ((k[:, sl] * k_c + ks) * k_rs[:, sl]).astype(BF16)


def _in_even(x, mod, b_of, norm_g, w, qng, kvng, wqb, wqs, wkn, wv, qg, qgs, kg, kgs, ind, spread, cm, sm):
    n = x.shape[0]
    t = ROW_TILE
    full = lambda shape: pl.BlockSpec(shape, lambda i: (0,) * len(shape))
    row = lambda c: pl.BlockSpec((t, c), lambda i: (i, 0))
    out_shape = (
        jax.ShapeDtypeStruct((n, 1024), F32),
        jax.ShapeDtypeStruct((n, 512), BF16),
        jax.ShapeDtypeStruct((n, 512), F32),
        jax.ShapeDtypeStruct((n, LANES), F32),
        jax.ShapeDtypeStruct((n, MLA_W), BF16),
        jax.ShapeDtypeStruct((n, MLA_W), BF16),
        jax.ShapeDtypeStruct((n, 512), BF16),
    )
    vec = full((1, LANES))
    return pl.pallas_call(
        _in_even_kernel,
        out_shape=out_shape,
        grid=(n // t,),
        in_specs=[
            row(D), full((1, D)),
            pl.BlockSpec((1, 1, D), lambda i: (b_of(i, t) * 6 + 1, 0, 0)),
            pl.BlockSpec((1, 1, D), lambda i: (b_of(i, t) * 6 + 0, 0, 0)),
            full((D, EVEN_COLS)), full((1, 256)), full((1, 128)), full((256, MLA_W)), full((256, MLA_W)),
            full((128, MLA_W)), full((128, 512)), vec, vec, vec, vec, full((MLA_W, LANES)), full((LANES, MLA_W)),
            row(LANES), row(LANES),
        ],
        out_specs=(row(1024), row(512), row(512), row(LANES), row(MLA_W), row(MLA_W), row(512)),
        compiler_params=_params(("parallel",)),
        name="in_even",
    )(x, norm_g, mod, mod, w, qng, kvng, wqb, wqs, wkn, wv, qg, qgs, kg, kgs, ind, spread, cm, sm)


def _conv_kernel(seq_tiles, prev_ref, x_ref, next_ref, w_ref, b_ref, s_ref, o_ref, buf):
    i = pl.program_id(0)
    t = x_ref.shape[0]
    first = (i % seq_tiles) == 0
    last = (i % seq_tiles) == seq_tiles - 1
    buf[0:8, :] = jnp.where(first, 0.0, prev_ref[...])
    buf[8:8 + t, :] = x_ref[...]
    buf[8 + t:16 + t, :] = jnp.where(last, 0.0, next_ref[...])
    pad = ML_CONV // 2
    acc = b_ref[...] + w_ref[0:1, :] * buf[8 - pad:8 - pad + t, :]
    for k in range(1, ML_CONV):
        acc = acc + w_ref[k:k + 1, :] * buf[8 - pad + k:8 - pad + k + t, :]
    o_ref[...] = (acc * jax.nn.sigmoid(acc) * s_ref[...]).astype(BF16)


def _conv(qk_pre, conv_w, conv_b, seq):
    n, c = qk_pre.shape
    t = ROW_TILE
    nb8 = n // 8
    scale = jnp.concatenate([jnp.ones((512,), F32), jnp.full((512,), ML_DK ** -0.5, F32)]).reshape(1, c)
    return pl.pallas_call(
        functools.partial(_conv_kernel, seq // t),
        out_shape=jax.ShapeDtypeStruct((n, c), BF16),
        grid=(n // t,),
        in_specs=[
            pl.BlockSpec((8, c), lambda i: (jnp.maximum(i * (t // 8) - 1, 0), 0)),
            pl.BlockSpec((t, c), lambda i: (i, 0)),
            pl.BlockSpec((8, c), lambda i: (jnp.minimum((i + 1) * (t // 8), nb8 - 1), 0)),
            pl.BlockSpec((ML_CONV, c), lambda i: (0, 0)),
            pl.BlockSpec((1, c), lambda i: (0, 0)),
            pl.BlockSpec((1, c), lambda i: (0, 0)),
        ],
        out_specs=pl.BlockSpec((t, c), lambda i: (i, 0)),
        scratch_shapes=[pltpu.VMEM((t + 16, c), F32)],
        compiler_params=_params(("parallel",)),
        name="conv",
    )(qk_pre, qk_pre, qk_pre, conv_w, conv_b.reshape(1, c), scale)


def _mlstm_kernel(qf_ref, kf_ref, vf_ref, qb_ref, kb_ref, vb_ref, gcf_ref, gcb_ref, grf_ref, grb_ref,
                  bc_ref, br_ref, of_ref, ob_ref, c_ref, m_ref):
    c = pl.program_id(1)
    L = ML_CHUNK

    @pl.when(c == 0)
    def _():
        c_ref[...] = jnp.zeros_like(c_ref)
        m_ref[...] = jnp.zeros_like(m_ref)

    row = lax.broadcasted_iota(jnp.int32, (L, L), 0)
    col = lax.broadcasted_iota(jnp.int32, (L, L), 1)
    lane = lax.broadcasted_iota(jnp.int32, (L, ML_DV), 1)
    ones_col = jnp.where(lane == 0, 1.0, 0.0).astype(BF16)
    dirs = ((qf_ref, kf_ref, vf_ref, gcf_ref, grf_ref, of_ref, row >= col, row <= col),
            (qb_ref, kb_ref, vb_ref, gcb_ref, grb_ref, ob_ref, row <= col, row >= col))
    for d, (q_ref, k_ref, v_ref, gc_ref, gr_ref, o_ref, causal, causal_t) in enumerate(dirs):
        gc = gc_ref[0] + bc_ref[d]
        gr = gr_ref[0] + br_ref[d]
        i_col = gc[:, 0:ML_HEADS]
        f_col = _log_sigmoid(gc[:, ML_HEADS:2 * ML_HEADS])
        i_row = gr[0:ML_HEADS, :]
        f_row = _log_sigmoid(gr[ML_HEADS:2 * ML_HEADS, :])
        bcum_col = jnp.dot(jnp.where(causal, 1.0, 0.0), f_col, precision=HIGHEST, preferred_element_type=F32)
        bcum_row = jnp.dot(f_row, jnp.where(causal_t, 1.0, 0.0), precision=HIGHEST, preferred_element_type=F32)
        for h in range(ML_HEADS):
            sl = slice(h * ML_DK, (h + 1) * ML_DK)
            st = d * ML_HEADS + h
            q = q_ref[:, sl]
            k = k_ref[:, sl]
            vext = jnp.concatenate([v_ref[:, sl], ones_col], axis=1)
            bc = bcum_col[:, h:h + 1]
            br = bcum_row[h:h + 1, :]
            ir = i_row[h:h + 1, :]
            ic = i_col[:, h:h + 1]
            m_prev = m_ref[st:st + 1, 0:1]
            log_d = jnp.where(causal, bc - br + ir, -jnp.inf)
            m_inter = bc + m_prev
            m_t = jnp.maximum(m_inter, jnp.max(log_d, axis=-1, keepdims=True))
            w = (_dot_nt(q, k) * jnp.exp(log_d - m_t)).astype(BF16)
            a_inter = jnp.exp(m_inter - m_t)
            tot = _dot(w, vext) + a_inter * _dot(q, c_ref[st].astype(BF16))
            den = tot[:, ML_DV:ML_DV + 1]
            o_ref[:, sl] = tot[:, 0:ML_DV] / jnp.maximum(jnp.abs(den), jnp.exp(-m_t))
            total = jnp.sum(f_row[h:h + 1, :], axis=-1, keepdims=True)
            g_row = total - br + ir
            m_new = jnp.maximum(total + m_prev, jnp.max(g_row, axis=-1, keepdims=True))
            a_state = jnp.exp(total + m_prev - m_new)
            kw = (k.astype(F32) * jnp.exp(total - bc + ic - m_new)).astype(BF16)
            c_ref[st] = a_state * c_ref[st] + _dot_tn(kw, vext)
            m_ref[st:st + 1, :] = jnp.broadcast_to(m_new, (1, LANES))


def _mlstm(qk, v, gcol, grow, bcol, brow, batch, seq):
    n = qk.shape[0]
    L = ML_CHUNK
    nc = seq // L
    fw = lambda b, c: b * nc + c
    bw = lambda b, c: b * nc + nc - 1 - c
    out = jax.ShapeDtypeStruct((n, ML_HEADS * ML_DV), F32)
    return pl.pallas_call(
        _mlstm_kernel,
        out_shape=(out, out),
        grid=(batch, nc),
        in_specs=[
            pl.BlockSpec((L, 512), lambda b, c: (fw(b, c), 0)),
            pl.BlockSpec((L, 512), lambda b, c: (fw(b, c), 1)),
            pl.BlockSpec((L, 512), lambda b, c: (fw(b, c), 0)),
            pl.BlockSpec((L, 512), lambda b, c: (bw(b, c), 0)),
            pl.BlockSpec((L, 512), lambda b, c: (bw(b, c), 1)),
            pl.BlockSpec((L, 512), lambda b, c: (bw(b, c), 0)),
            pl.BlockSpec((1, L, 8), lambda b, c: (0, fw(b, c), 0)),
            pl.BlockSpec((1, L, 8), lambda b, c: (1, bw(b, c), 0)),
            pl.BlockSpec((1, 8, L), lambda b, c: (0, 0, fw(b, c))),
            pl.BlockSpec((1, 8, L), lambda b, c: (1, 0, bw(b, c))),
            pl.BlockSpec((2, 1, 8), lambda b, c: (0, 0, 0)),
            pl.BlockSpec((2, 8, 1), lambda b, c: (0, 0, 0)),
        ],
        out_specs=(pl.BlockSpec((L, 512), lambda b, c: (fw(b, c), 0)),
                   pl.BlockSpec((L, 512), lambda b, c: (bw(b, c), 0))),
        scratch_shapes=[pltpu.VMEM((2 * ML_HEADS, ML_DK, 2 * ML_DV), F32), pltpu.VMEM((8, LANES), F32)],
        compiler_params=_params(("parallel", "arbitrary")),
        name="mlstm",
    )(qk, qk, v, qk, qk, v, gcol, gcol, grow, grow, bcol, brow)


def _attn_kernel(q_ref, k_ref, v_ref, o_ref):
    outs = []
    for j in range(2):
        q = q_ref[:, j * MLA_PAD:(j + 1) * MLA_PAD]
        k = k_ref[:, j * MLA_PAD:(j + 1) * MLA_PAD]
        v = v_ref[:, j * MLA_V:(j + 1) * MLA_V]
        s = _dot_nt(q, k)
        p = jnp.exp2(s - jnp.max(s, axis=-1, keepdims=True))
        l = jnp.sum(p, axis=-1, keepdims=True)
        outs.append(_dot(p.astype(BF16), v) / l)
    o_ref[...] = jnp.concatenate(outs, axis=1).astype(BF16)


def _attn(q, k, v, batch, seq):
    n = q.shape[0]
    t = min(ATTN_Q_TILE, seq)
    nq = seq // t
    return pl.pallas_call(
        _attn_kernel,
        out_shape=jax.ShapeDtypeStruct((n, MLA_HEADS * MLA_V), BF16),
        grid=(batch, MLA_HEADS // 2, nq),
        in_specs=[
            pl.BlockSpec((t, 2 * MLA_PAD), lambda b, h, i: (b * nq + i, h)),
            pl.BlockSpec((seq, 2 * MLA_PAD), lambda b, h, i: (b, h)),
            pl.BlockSpec((seq, 2 * MLA_V), lambda b, h, i: (b, h)),
        ],
        out_specs=pl.BlockSpec((t, 2 * MLA_V), lambda b, h, i: (b * nq + i, h)),
        compiler_params=_params(("parallel", "parallel", "arbitrary")),
        name="attn",
    )(q, k, v)


def _mixer_tail(y, x_ref, g1_ref, fg_ref, sc2_ref, sh2_ref, rwh_ref, rwl_ref, rb_ref, x_out, hf_out, lg_out):
    x1 = x_ref[...] + g1_ref[0] * y
    x_out[...] = x1
    hf = _norm_mod(x1, fg_ref[...], sc2_ref[0], sh2_ref[0])
    hf_out[...] = hf
    hi, lo = _split_bf16(hf)
    lg_out[...] = _dot(hi, rwh_ref[...]) + (_dot(lo, rwh_ref[...]) + _dot(hi, rwl_ref[...])) + rb_ref[...]


def _tail_specs(n, t, b_of):
    full = lambda shape: pl.BlockSpec(shape, lambda i: (0,) * len(shape))
    mod = lambda j: pl.BlockSpec((1, 1, D), lambda i: (b_of(i, t) * 6 + j, 0, 0))
    row = lambda c: pl.BlockSpec((t, c), lambda i: (i, 0))
    in_specs = [row(D), mod(2), full((1, D)), mod(4), mod(3), full((D, LANES)), full((D, LANES)), full((1, LANES))]
    out_shape = (jax.ShapeDtypeStruct((n, D), F32), jax.ShapeDtypeStruct((n, D), F32),
                 jax.ShapeDtypeStruct((n, LANES), F32))
    out_specs = (row(D), row(D), row(LANES))
    return in_specs, out_shape, out_specs


def _out_even_kernel(hf_ref, hb_ref, om_ref, oa_ref, mg_ref, wt_ref, wb_ref, *tail):
    hs = hf_ref[...] + hb_ref[...]
    gate = jax.nn.sigmoid(om_ref[...])
    parts = []
    for h in range(ML_HEADS):
        sl = slice(h * ML_DV, (h + 1) * ML_DV)
        parts.append(_rms(hs[:, sl]) * mg_ref[:, sl] * gate[:, sl])
    hml = jnp.concatenate(parts, axis=1).astype(BF16)
    y = _dot(hml, wt_ref[...]) + _dot(oa_ref[...], wb_ref[...])
    _mixer_tail(y, *tail)


def _out_even(h_f, h_b, o_m, o_mla, ml_g, w_top, w_bot, x, mod, b_of, ffn_g, rw, rb):
    n = x.shape[0]
    t = ROW_TILE
    full = lambda shape: pl.BlockSpec(shape, lambda i: (0,) * len(shape))
    row = lambda c: pl.BlockSpec((t, c), lambda i: (i, 0))
    tin, out_shape, out_specs = _tail_specs(n, t, b_of)
    return pl.pallas_call(
        _out_even_kernel,
        out_shape=out_shape,
        grid=(n // t,),
        in_specs=[row(512), row(512), row(512), row(512), full((1, 512)), full((512, D)), full((512, D))] + tin,
        out_specs=out_specs,
        compiler_params=_params(("parallel",)),
        name="out_even",
    )(h_f, h_b, o_m, o_mla, ml_g, w_top, w_bot, x, mod, ffn_g, mod, mod, *rw, rb)


def _in_odd_kernel(x_ref, g_ref, sc_ref, sh_ref, w_ref, cr_ref, sr_ref, q_out, k_out, v_out, g_out):
    h = _norm_mod(x_ref[...], g_ref[...], sc_ref[0], sh_ref[0]).astype(BF16)
    cos, sin = cr_ref[...], sr_ref[...]
    hw = RET_DK // 2
    for idx, (dst, scale) in enumerate(((q_out, RET_DK ** -0.5), (k_out, 1.0))):
        z = _dot(h, w_ref[:, idx * 1024:(idx + 1) * 1024])
        for hd in range(RET_HEADS):
            x1 = z[:, hd * RET_DK:hd * RET_DK + hw]
            x2 = z[:, hd * RET_DK + hw:(hd + 1) * RET_DK]
            dst[:, hd * RET_DK:hd * RET_DK + hw] = ((x1 * cos - x2 * sin) * scale).astype(BF16)
            dst[:, hd * RET_DK + hw:(hd + 1) * RET_DK] = ((x2 * cos + x1 * sin) * scale).astype(BF16)
    v_out[...] = _dot(h, w_ref[:, 2048:4096]).astype(BF16)
    g_out[...] = _dot(h, w_ref[:, 4096:6144])


def _in_odd(x, mod, b_of, norm_g, w, cr, sr):
    n = x.shape[0]
    t = ROW_TILE
    full = lambda shape: pl.BlockSpec(shape, lambda i: (0,) * len(shape))
    row = lambda c: pl.BlockSpec((t, c), lambda i: (i, 0))
    out_shape = (
        jax.ShapeDtypeStruct((n, 1024), BF16), jax.ShapeDtypeStruct((n, 1024), BF16),
        jax.ShapeDtypeStruct((n, 2048), BF16), jax.ShapeDtypeStruct((n, 2048), F32),
    )
    return pl.pallas_call(
        _in_odd_kernel,
        out_shape=out_shape,
        grid=(n // t,),
        in_specs=[
            row(D), full((1, D)),
            pl.BlockSpec((1, 1, D), lambda i: (b_of(i, t) * 6 + 1, 0, 0)),
            pl.BlockSpec((1, 1, D), lambda i: (b_of(i, t) * 6 + 0, 0, 0)),
            full((D, 6144)), row(LANES), row(LANES),
        ],
        out_specs=(row(1024), row(1024), row(2048), row(2048)),
        compiler_params=_params(("parallel",)),
        name="in_odd",
    )(x, norm_g, mod, mod, w, cr, sr)


def _ret_kernel(qf_ref, kf_ref, vf_ref, qb_ref, kb_ref, vb_ref, dec_ref, of_ref, ob_ref, s_ref):
    c = pl.program_id(1)
    L = RET_CHUNK

    @pl.when(c == 0)
    def _():
        s_ref[...] = jnp.zeros_like(s_ref)

    row = lax.broadcasted_iota(jnp.int32, (L, L), 0)
    col = lax.broadcasted_iota(jnp.int32, (L, L), 1)
    pos = lax.broadcasted_iota(jnp.int32, (L, 1), 0)
    dirs = ((qf_ref, kf_ref, vf_ref, of_ref, row - col, pos + 1, L - 1 - pos),
            (qb_ref, kb_ref, vb_ref, ob_ref, col - row, L - pos, pos))
    for d, (q_ref, k_ref, v_ref, o_ref, rel_i, exp_q_i, exp_k_i) in enumerate(dirs):
        rel = rel_i.astype(F32)
        exp_q = exp_q_i.astype(F32)
        exp_k = exp_k_i.astype(F32)
        for h in range(RET_HEADS):
            st = d * RET_HEADS + h
            lg = _log_sigmoid(dec_ref[d, h:h + 1, :])
            lg1 = lg[:, 0:1]
            decay_in = jnp.where(rel >= 0, jnp.exp(jnp.maximum(rel, 0.0) * lg1), 0.0)
            q = q_ref[:, h * RET_DK:(h + 1) * RET_DK]
            k = k_ref[:, h * RET_DK:(h + 1) * RET_DK]
            v = v_ref[:, h * RET_DV:(h + 1) * RET_DV]
            sc = (_dot_nt(q, k) * decay_in).astype(BF16)
            o_ref[:, h * RET_DV:(h + 1) * RET_DV] = (
                _dot(sc, v) + jnp.exp(exp_q * lg1) * _dot(q, s_ref[st].astype(BF16)))
            kd = (k.astype(F32) * jnp.exp(exp_k * lg1)).astype(BF16)
            s_ref[st] = jnp.exp(L * lg1) * s_ref[st] + _dot_tn(kd, v)


def _retention(q, k, v, dec, batch, seq):
    n = q.shape[0]
    L = RET_CHUNK
    nc = seq // L
    fw = lambda b, c: (b * nc + c, 0)
    bw = lambda b, c: (b * nc + nc - 1 - c, 0)
    out = jax.ShapeDtypeStruct((n, RET_HEADS * RET_DV), F32)
    return pl.pallas_call(
        _ret_kernel,
        out_shape=(out, out),
        grid=(batch, nc),
        in_specs=[
            pl.BlockSpec((L, 1024), fw), pl.BlockSpec((L, 1024), fw), pl.BlockSpec((L, 2048), fw),
            pl.BlockSpec((L, 1024), bw), pl.BlockSpec((L, 1024), bw), pl.BlockSpec((L, 2048), bw),
            pl.BlockSpec((2, RET_HEADS, LANES), lambda b, c: (0, 0, 0)),
        ],
        out_specs=(pl.BlockSpec((L, 2048), fw), pl.BlockSpec((L, 2048), bw)),
        scratch_shapes=[pltpu.VMEM((2 * RET_HEADS, RET_DK, RET_DV), F32)],
        compiler_params=_params(("parallel", "arbitrary")),
        name="retention",
    )(q, k, v, q, k, v, dec)


def _out_odd_kernel(yf_ref, yb_ref, g_ref, gn_ref, w_ref, *tail):
    ys = yf_ref[...] + yb_ref[...]
    g = g_ref[...]
    gate = g * jax.nn.sigmoid(g)
    parts = []
    for h in range(RET_HEADS):
        sl = slice(h * RET_DV, (h + 1) * RET_DV)
        yh = ys[:, sl]
        yc = yh - jnp.mean(yh, axis=-1, keepdims=True)
        var = jnp.mean(yc * yc, axis=-1, keepdims=True)
        parts.append(yc * lax.rsqrt(var + EPS) * gn_ref[:, sl] * gate[:, sl])
    yn = jnp.concatenate(parts, axis=1).astype(BF16)
    _mixer_tail(_dot(yn, w_ref[...]), *tail)


def _out_odd(y_f, y_b, g, gn_g, w_out, x, mod, b_of, ffn_g, rw, rb):
    n = x.shape[0]
    t = ROW_TILE
    full = lambda shape: pl.BlockSpec(shape, lambda i: (0,) * len(shape))
    row = lambda c: pl.BlockSpec((t, c), lambda i: (i, 0))
    tin, out_shape, out_specs = _tail_specs(n, t, b_of)
    return pl.pallas_call(
        _out_odd_kernel,
        out_shape=out_shape,
        grid=(n // t,),
        in_specs=[row(2048), row(2048), row(2048), full((1, 2048)), full((2048, D))] + tin,
        out_specs=out_specs,
        compiler_params=_params(("parallel",)),
        name="out_odd",
    )(y_f, y_b, g, gn_g, w_out, x, mod, ffn_g, mod, mod, *rw, rb)


def _route_kernel(lg_ref, e_out, r_out, w_out, cnt_out, base_ref):
    i = pl.program_id(0)
    t = lg_ref.shape[0]

    @pl.when(i == 0)
    def _():
        base_ref[...] = jnp.zeros_like(base_ref)

    l = lg_ref[...]
    lane = lax.broadcasted_iota(jnp.int32, l.shape, 1)
    sel = jnp.zeros(l.shape, F32)
    vals, idxs, hots = [], [], []
    for _k in range(TOP_K):
        m = jnp.max(l, axis=-1, keepdims=True)
        idx = jnp.min(jnp.where(l == m, lane, LANES), axis=-1, keepdims=True)
        hot = lane == idx
        vals.append(m)
        idxs.append(idx)
        hots.append(hot)
        sel = sel + jnp.where(hot, 1.0, 0.0)
        l = jnp.where(hot, -jnp.inf, l)
    es = [jnp.exp(v - vals[0]) for v in vals]
    den = es[0] + es[1] + es[2] + es[3]
    row = lax.broadcasted_iota(jnp.int32, (t, t), 0)
    col = lax.broadcasted_iota(jnp.int32, (t, t), 1)
    before = jnp.where(row > col, 1.0, 0.0).astype(BF16)
    rank_mat = _dot(before, sel.astype(BF16)) + base_ref[...]
    e_acc = jnp.zeros(l.shape, jnp.int32)
    r_acc = jnp.zeros(l.shape, jnp.int32)
    w_acc = jnp.zeros(l.shape, F32)
    for k in range(TOP_K):
        rk = jnp.sum(jnp.where(hots[k], rank_mat, 0.0), axis=-1, keepdims=True).astype(jnp.int32)
        e_acc = jnp.where(lane == k, idxs[k], e_acc)
        r_acc = jnp.where(lane == k, rk, r_acc)
        w_acc = jnp.where(lane == k, es[k] / den, w_acc)
    e_out[...] = e_acc
    r_out[...] = r_acc
    w_out[...] = w_acc
    base_ref[...] = base_ref[...] + jnp.sum(sel, axis=0, keepdims=True)
    cnt_out[...] = base_ref[...]


def _route(logits):
    n = logits.shape[0]
    t = ROUTE_TILE
    row = pl.BlockSpec((t, LANES), lambda i: (i, 0))
    return pl.pallas_call(
        _route_kernel,
        out_shape=(jax.ShapeDtypeStruct((n, LANES), jnp.int32), jax.ShapeDtypeStruct((n, LANES), jnp.int32),
                   jax.ShapeDtypeStruct((n, LANES), F32), jax.ShapeDtypeStruct((1, LANES), F32)),
        grid=(n // t,),
        in_specs=[row],
        out_specs=(row, row, row, pl.BlockSpec((1, LANES), lambda i: (0, 0))),
        scratch_shapes=[pltpu.VMEM((1, LANES), F32)],
        compiler_params=_params(("arbitrary",)),
        name="route",
    )(logits)


def _dispatch_kernel(dest_ref, pad_ref, hf_ref, xs_ref, sem):
    t = hf_ref.shape[0]

    def row_copy(src_row, dst_row):
        return pltpu.make_async_copy(hf_ref.at[pl.ds(src_row, 1)], xs_ref.at[pl.ds(dst_row, 1)], sem)

    def wait_rows():
        pltpu.make_async_copy(hf_ref, xs_ref.at[pl.ds(0, t)], sem).wait()

    def issue(g, carry):
        for u in range(MOVE_UNROLL):
            tok = g * MOVE_UNROLL + u
            for k in range(TOP_K):
                row_copy(tok, dest_ref[0, 0, tok * TOP_K + k]).start()
        return carry

    lax.fori_loop(0, t // MOVE_UNROLL, issue, 0)
    for _ in range(TOP_K):
        wait_rows()

    @pl.when(pl.program_id(0) == 0)
    def _():
        per_iter = MOVE_UNROLL * TOP_K
        n_free = pad_ref.shape[-1]

        def pad_issue(g, carry):
            for u in range(per_iter):
                row_copy(0, pad_ref[0, 0, g * per_iter + u]).start()
            return carry

        lax.fori_loop(0, n_free // per_iter, pad_issue, 0)
        for _ in range(n_free // t):
            wait_rows()


def _dispatch(hf, dest, pad_dest, rows):
    n = hf.shape[0]
    t = MOVE_TILE
    n_free = pad_dest.shape[0]
    assert n_free % t == 0 and t % MOVE_UNROLL == 0
    return pl.pallas_call(
        _dispatch_kernel,
        out_shape=jax.ShapeDtypeStruct((rows, D), F32),
        grid=(n // t,),
        in_specs=[
            pl.BlockSpec((1, 1, t * TOP_K), lambda i: (i, 0, 0), memory_space=pltpu.SMEM),
            pl.BlockSpec((1, 1, n_free), lambda i: (0, 0, 0), memory_space=pltpu.SMEM),
            pl.BlockSpec((t, D), lambda i: (i, 0)),
        ],
        out_specs=pl.BlockSpec(memory_space=pl.ANY),
        scratch_shapes=[pltpu.SemaphoreType.DMA(())],
        compiler_params=_params(("arbitrary",), has_side_effects=True),
        name="dispatch",
    )(dest.reshape(n // t, 1, t * TOP_K), pad_dest.reshape(1, 1, n_free), hf)


def _expert_kernel(be_ref, nu_ref, xs_ref, wgu_ref, bgu_ref, wd_ref, bd_ref, ys_ref, wgu_bf, wd_bf):
    i = pl.program_id(0)
    fresh = jnp.logical_or(i == 0, be_ref[i] != be_ref[jnp.maximum(i - 1, 0)])

    @pl.when(fresh)
    def _():
        wgu_bf[...] = wgu_ref[0, 0].astype(BF16)
        wd_bf[...] = wd_ref[0, 0].astype(BF16)

    @pl.when(i < nu_ref[0])
    def _():
        gu = _dot(xs_ref[...].astype(BF16), wgu_bf[...]) + bgu_ref[0, 0]
        gate = jnp.minimum(gu[:, :EXPERT_FF], SWIGLU_LIMIT)
        up = jnp.clip(gu[:, EXPERT_FF:], -SWIGLU_LIMIT, SWIGLU_LIMIT)
        act = (up + 1.0) * gate * jax.nn.sigmoid(SWIGLU_ALPHA * gate)
        ys_ref[...] = _dot(act.astype(BF16), wd_bf[...]) + bd_ref[0, 0]

    @pl.when(i >= nu_ref[0])
    def _():
        ys_ref[...] = jnp.zeros_like(ys_ref)


def _experts(xs, nb, layer, blk_expert, n_used, w_gu, b_gu, w_down, b_down):
    grid_spec = pltpu.PrefetchScalarGridSpec(
        num_scalar_prefetch=2,
        grid=(nb,),
        in_specs=[
            pl.BlockSpec((MOE_BLOCK, D), lambda i, be, nu: (jnp.minimum(i, nu[0] - 1), 0)),
            pl.BlockSpec((1, 1, D, 2 * EXPERT_FF), lambda i, be, nu: (layer, be[i], 0, 0)),
            pl.BlockSpec((1, 1, 1, 2 * EXPERT_FF), lambda i, be, nu: (layer, be[i], 0, 0)),
            pl.BlockSpec((1, 1, EXPERT_FF, D), lambda i, be, nu: (layer, be[i], 0, 0)),
            pl.BlockSpec((1, 1, 1, D), lambda i, be, nu: (layer, be[i], 0, 0)),
        ],
        out_specs=pl.BlockSpec((MOE_BLOCK, D), lambda i, be, nu: (i, 0)),
        scratch_shapes=[pltpu.VMEM((D, 2 * EXPERT_FF), BF16), pltpu.VMEM((EXPERT_FF, D), BF16)],
    )
    return pl.pallas_call(
        _expert_kernel,
        out_shape=jax.ShapeDtypeStruct((nb * MOE_BLOCK, D), F32),
        grid_spec=grid_spec,
        compiler_params=_params(("arbitrary",)),
        name="experts",
    )(blk_expert, n_used, xs, w_gu, b_gu.reshape(DEPTH, N_EXPERTS, 1, 2 * EXPERT_FF), w_down,
      b_down.reshape(DEPTH, N_EXPERTS, 1, D))


def _combine_kernel(dest0_ref, dest_ref, ys_ref, wt_ref, x_ref, g2_ref, o_ref, buf_a, buf_b, sem_a, sem_b):
    i = pl.program_id(0)
    t = x_ref.shape[0]

    def start_rows(idx_ref, tok, buf, sem):
        for k in range(TOP_K):
            pltpu.make_async_copy(ys_ref.at[pl.ds(idx_ref[0, 0, tok * TOP_K + k], 1)],
                                  buf.at[k, pl.ds(tok, 1)], sem).start()

    def wait_tile(buf, sem):
        for k in range(TOP_K):
            pltpu.make_async_copy(ys_ref.at[pl.ds(0, t)], buf.at[k], sem).wait()

    @pl.when(i == 0)
    def _():
        def issue(g, carry):
            for u in range(MOVE_UNROLL):
                start_rows(dest0_ref, g * MOVE_UNROLL + u, buf_a, sem_a)
            return carry
        lax.fori_loop(0, t // MOVE_UNROLL, issue, 0)
        wait_tile(buf_a, sem_a)

    g2 = g2_ref[0]

    def step(cur, nxt, sem_nxt):
        def body(g, carry):
            for u in range(MOVE_UNROLL):
                start_rows(dest_ref, g * MOVE_UNROLL + u, nxt, sem_nxt)
            rows = pl.ds(pl.multiple_of(g * MOVE_UNROLL, MOVE_UNROLL), MOVE_UNROLL)
            wt = wt_ref[rows, :]
            acc = cur[0, rows, :] * wt[:, 0:1]
            for k in range(1, TOP_K):
                acc = acc + cur[k, rows, :] * wt[:, k:k + 1]
            o_ref[rows, :] = x_ref[rows, :] + g2 * acc
            return carry
        lax.fori_loop(0, t // MOVE_UNROLL, body, 0)
        wait_tile(nxt, sem_nxt)

    @pl.when(i % 2 == 0)
    def _():
        step(buf_a, buf_b, sem_b)

    @pl.when(i % 2 == 1)
    def _():
        step(buf_b, buf_a, sem_a)


def _combine(ys, dest, wts, x, mod, b_of):
    n = x.shape[0]
    t = MOVE_TILE
    nt = n // t
    dest3 = dest.reshape(nt, 1, t * TOP_K)
    idx = lambda f: pl.BlockSpec((1, 1, t * TOP_K), lambda i: (f(i), 0, 0), memory_space=pltpu.SMEM)
    return pl.pallas_call(
        _combine_kernel,
        out_shape=jax.ShapeDtypeStruct((n, D), F32),
        grid=(nt,),
        in_specs=[
            idx(lambda i: 0),
            idx(lambda i: jnp.minimum(i + 1, nt - 1)),
            pl.BlockSpec(memory_space=pl.ANY),
            pl.BlockSpec((t, LANES), lambda i: (i, 0)),
            pl.BlockSpec((t, D), lambda i: (i, 0)),
            pl.BlockSpec((1, 1, D), lambda i: (b_of(i, t) * 6 + 5, 0, 0)),
        ],
        out_specs=pl.BlockSpec((t, D), lambda i: (i, 0)),
        scratch_shapes=[pltpu.VMEM((TOP_K, t, D), F32), pltpu.VMEM((TOP_K, t, D), F32),
                        pltpu.SemaphoreType.DMA(()), pltpu.SemaphoreType.DMA(())],
        compiler_params=_params(("arbitrary",)),
        name="combine",
    )(dest3, dest3, ys, wts, x, mod)


def _moe(x1, hf, logits, mod, b_of, layer, w_gu, b_gu, w_down, b_down):
    n = x1.shape[0]
    eidx, rank, wts, counts = _route(logits)
    experts = jnp.arange(N_EXPERTS, dtype=jnp.int32)
    cnt = counts[0, :N_EXPERTS].astype(jnp.int32)
    padded = (cnt + MOE_BLOCK - 1) // MOE_BLOCK * MOE_BLOCK
    pend = jnp.cumsum(padded)
    pstart = pend - padded
    e4 = eidx[:, :TOP_K]
    dest = rank[:, :TOP_K] + jnp.sum(jnp.where(e4[:, :, None] == experts, pstart, 0), axis=-1)
    nb = -(-(n * TOP_K) // MOE_BLOCK) + N_EXPERTS
    blk_start = jnp.arange(nb, dtype=jnp.int32) * MOE_BLOCK
    blk_expert = jnp.minimum(jnp.sum((pend[None, :] <= blk_start[:, None]).astype(jnp.int32), axis=1),
                             N_EXPERTS - 1)
    n_used = (pend[-1:] // MOE_BLOCK).astype(jnp.int32)
    n_free = nb * MOE_BLOCK - n * TOP_K
    free_end = jnp.cumsum(padded - cnt)
    seg_first = jnp.concatenate([pstart + cnt, pend[-1:]])
    seg_skip = jnp.concatenate([free_end - (padded - cnt), free_end[-1:]])
    jj = jnp.arange(n_free, dtype=jnp.int32)
    seg = jnp.sum((free_end[None, :] <= jj[:, None]).astype(jnp.int32), axis=1)
    hot = seg[:, None] == jnp.arange(N_EXPERTS + 1, dtype=jnp.int32)
    pad_dest = jj + jnp.sum(jnp.where(hot, seg_first - seg_skip, 0), axis=1)
    xs = _dispatch(hf, dest, pad_dest, nb * MOE_BLOCK)
    ys = _experts(xs, nb, layer, blk_expert, n_used, w_gu, b_gu, w_down, b_down)
    return _combine(ys, dest, wts, x1, mod, b_of)


def _head_tile(t, partner):
    half = MLA_ROPE // 2
    lead = t.shape[:-1]
    tail = jnp.zeros(lead + (MLA_PAD - MLA_QK,), t.dtype)
    if not partner:
        return jnp.concatenate([t, tail], axis=-1)
    return jnp.concatenate([jnp.zeros(lead + (MLA_NOPE,), t.dtype), t[..., MLA_NOPE + half:],
                            t[..., MLA_NOPE:MLA_NOPE + half], tail], axis=-1)


def _prep_even(w_in, gate_b, w_qb, w_kvb, qk_q_g, qk_k_g, w_out):
    q_m, k_m, v_m, o_m, gates, q_a, kv_a, k_r = jnp.split(
        w_in, [512, 1024, 1536, 2048, 2064, 2320, 2448], axis=1)
    zeros = lambda c: jnp.zeros((D, c), F32)
    k_r96 = jnp.concatenate([zeros(MLA_NOPE), k_r], axis=1)
    misc = jnp.concatenate([gates, zeros(LANES - 16)], axis=1)
    w = jnp.concatenate([q_m, k_m, v_m, o_m, q_a, kv_a, _head_tile(k_r96, False), _head_tile(k_r96, True), misc],
                        axis=1).astype(BF16)
    qb = w_qb.reshape(MLA_Q_LORA, MLA_HEADS, MLA_QK)
    wqb = _head_tile(qb, False).reshape(MLA_Q_LORA, MLA_W).astype(BF16)
    wqs = _head_tile(qb, True).reshape(MLA_Q_LORA, MLA_W).astype(BF16)
    kvb = w_kvb.reshape(MLA_KV_LORA, MLA_HEADS, MLA_NOPE + MLA_V)
    wkn = jnp.pad(kvb[:, :, :MLA_NOPE], ((0, 0), (0, 0), (0, MLA_PAD - MLA_NOPE)))
    wkn = wkn.reshape(MLA_KV_LORA, MLA_W).astype(BF16)
    wv = kvb[:, :, MLA_NOPE:].reshape(MLA_KV_LORA, MLA_HEADS * MLA_V).astype(BF16)
    qg96 = qk_q_g * (MLA_QK ** -0.5 * LOG2_E)
    qg = _head_tile(qg96, False).reshape(1, MLA_PAD)
    qgs = _head_tile(qg96, True).reshape(1, MLA_PAD)
    kg = _head_tile(qk_k_g, False).reshape(1, MLA_PAD)
    kgs = _head_tile(qk_k_g, True).reshape(1, MLA_PAD)
    gb = gate_b.reshape(2, 8)
    w_top = w_out[:512].astype(BF16)
    w_bot = w_out[512:].astype(BF16)
    return w, wqb, wqs, wkn, wv, qg, qgs, kg, kgs, gb.reshape(2, 1, 8), gb.reshape(2, 8, 1), w_top, w_bot


def kernel(x, c, positions, ada_w, ada_b, norm_mix_g, norm_ffn_g, hy_w_in, ml_conv_w, ml_conv_b, ml_gate_b, ml_norm_g, mla_q_norm_g, mla_kv_norm_g, mla_w_qb, mla_w_kvb, mla_qk_q_g, mla_qk_k_g, hy_w_out, ret_w_in, ret_decay_f, ret_decay_b, ret_gn_g, ret_w_out, moe_router_w, moe_router_b, moe_w_gu, moe_b_gu, moe_w_down, moe_b_down):
    batch, seq, _ = x.shape
    n = batch * seq

    def b_of(i, t):
        return (i * t) // seq

    mod_all = _ada(c, ada_w, ada_b).reshape(DEPTH, batch * 6, 1, D)
    cr, sr, cm, sm = _rope_tables(positions)
    head_of_lane = jnp.arange(MLA_W, dtype=jnp.int32) // MLA_PAD
    ind = (head_of_lane[:, None] == jnp.arange(LANES, dtype=jnp.int32)[None, :]).astype(BF16)
    spread = ind.T
    xf = x.reshape(n, D)
    for layer in range(DEPTH):
        mod = mod_all[layer]
        j = layer // 2
        mix_g = norm_mix_g[layer].reshape(1, D)
        ffn_g = norm_ffn_g[layer].reshape(1, D)
        rw = _split_bf16(jnp.pad(moe_router_w[layer], ((0, 0), (0, LANES - N_EXPERTS))))
        rb = jnp.concatenate([moe_router_b[layer], jnp.full((LANES - N_EXPERTS,), NEG_BIG, F32)]).reshape(1, LANES)
        if layer % 2 == 0:
            w, wqb, wqs, wkn, wv, qg, qgs, kg, kgs, gbc, gbr, w_top, w_bot = _prep_even(
                hy_w_in[j], ml_gate_b[j], mla_w_qb[j], mla_w_kvb[j], mla_qk_q_g[j], mla_qk_k_g[j], hy_w_out[j])
            qk_pre, v_m, o_m, misc, q_a, k_a, v_a = _in_even(
                xf, mod, b_of, mix_g, w, mla_q_norm_g[j].reshape(1, -1), mla_kv_norm_g[j].reshape(1, -1),
                wqb, wqs, wkn, wv, qg, qgs, kg, kgs, ind, spread, cm, sm)
            qk = _conv(qk_pre, ml_conv_w[j], ml_conv_b[j], seq)
            gcol = misc[:, :16].reshape(n, 2, 8).transpose(1, 0, 2)
            grow = gcol.transpose(0, 2, 1)
            h_f, h_b = _mlstm(qk, v_m, gcol, grow, gbc, gbr, batch, seq)
            o_mla = _attn(q_a, k_a, v_a, batch, seq)
            x1, hf, logits = _out_even(h_f, h_b, o_m, o_mla, ml_norm_g[j].reshape(1, -1), w_top, w_bot,
                                       xf, mod, b_of, ffn_g, rw, rb)
        else:
            dec = jnp.broadcast_to(jnp.stack([ret_decay_f[j], ret_decay_b[j]])[:, :, None], (2, RET_HEADS, LANES))
            q_r, k_r, v_r, g_r = _in_odd(xf, mod, b_of, mix_g, ret_w_in[j].astype(BF16), cr, sr)
            y_f, y_b = _retention(q_r, k_r, v_r, dec.astype(F32), batch, seq)
            x1, hf, logits = _out_odd(y_f, y_b, g_r, ret_gn_g[j].reshape(1, -1), ret_w_out[j].astype(BF16),
                                      xf, mod, b_of, ffn_g, rw, rb)
        xf = _moe(x1, hf, logits, mod, b_of, layer, moe_w_gu, moe_b_gu, moe_w_down, moe_b_down)
    return xf.reshape(batch, seq, D)
```

```python
import functools

import jax
import jax.numpy as jnp
from jax import lax
from jax.experimental import pallas as pl
from jax.experimental.pallas import tpu as pltpu

F32 = jnp.float32
BF16 = jnp.bfloat16
HIGHEST = lax.Precision.HIGHEST

D = 1024
DEPTH = 4
EPS = 1e-6
ROPE_BASE = 10000.0

ML_HEADS = 4
ML_DK = 128
ML_DV = 128
ML_CONV = 5
ML_CHUNK = 256

MLA_HEADS = 8
MLA_Q_LORA = 256
MLA_KV_LORA = 128
MLA_NOPE = 64
MLA_ROPE = 32
MLA_V = 64
MLA_QK = MLA_NOPE + MLA_ROPE
MLA_PAD = 128
MLA_W = MLA_HEADS * MLA_PAD

RET_HEADS = 4
RET_DK = 256
RET_DV = 512
RET_CHUNK = 256

N_EXPERTS = 32
TOP_K = 4
EXPERT_FF = 1024
SWIGLU_LIMIT = 7.0
SWIGLU_ALPHA = 1.702
MOE_BLOCK = 512

LANES = 128
NEG_BIG = -1e30
LOG2_E = 1.4426950408889634

ROW_TILE = 512
ATTN_Q_TILE = 256
ROUTE_TILE = 256
DISPATCH_TILE = 1024
MOVE_TILE = 512
MOVE_UNROLL = 8
VMEM_LIMIT = 56 * 1024 * 1024


def _params(semantics, **kw):
    return pltpu.CompilerParams(dimension_semantics=semantics, vmem_limit_bytes=VMEM_LIMIT, **kw)


def _dot(a, b):
    return jnp.dot(a, b, preferred_element_type=F32)


def _dot_nt(a, b):
    return lax.dot_general(a, b, (((1,), (1,)), ((), ())), preferred_element_type=F32)


def _dot_tn(a, b):
    return lax.dot_general(a, b, (((0,), (0,)), ((), ())), preferred_element_type=F32)


def _log_sigmoid(x):
    return jnp.minimum(x, 0.0) - jnp.log1p(jnp.exp(-jnp.abs(x)))


def _rms(x, denom=None):
    n = x.shape[-1] if denom is None else denom
    return x * lax.rsqrt(jnp.sum(x * x, axis=-1, keepdims=True) / n + EPS)


def _norm_mod(x, g, sc, sh):
    return (_rms(x) * g) * (1.0 + sc) + sh


def _ada_kernel(c_ref, w_ref, b_ref, o_ref):
    c = c_ref[...]
    cs = c * jax.nn.sigmoid(c)
    o_ref[0] = jnp.dot(cs, w_ref[0], precision=HIGHEST, preferred_element_type=F32) + b_ref[0]


def _ada(c, ada_w, ada_b):
    b = c.shape[0]
    return pl.pallas_call(
        _ada_kernel,
        out_shape=jax.ShapeDtypeStruct((DEPTH, b, 6 * D), F32),
        grid=(DEPTH, 6),
        in_specs=[
            pl.BlockSpec((b, D), lambda l, j: (0, 0)),
            pl.BlockSpec((1, D, D), lambda l, j: (l, 0, j)),
            pl.BlockSpec((1, 1, D), lambda l, j: (l, 0, j)),
        ],
        out_specs=pl.BlockSpec((1, b, D), lambda l, j: (l, 0, j)),
        compiler_params=_params(("parallel", "parallel")),
        name="ada",
    )(c, ada_w, ada_b.reshape(DEPTH, 1, 6 * D))


def _rope_kernel(pos_ref, invr_ref, invm_ref, cr_ref, sr_ref, cm_ref, sm_ref):
    pos = pos_ref[...].astype(F32)
    ang_r = pos * invr_ref[...]
    cr_ref[...] = jnp.cos(ang_r)
    sr_ref[...] = jnp.sin(ang_r)
    ang_m = pos * invm_ref[...]
    lane = lax.broadcasted_iota(jnp.int32, ang_m.shape, 1)
    s = jnp.sin(ang_m)
    cm_ref[...] = jnp.cos(ang_m)
    sm_ref[...] = jnp.where(lane < MLA_NOPE + MLA_ROPE // 2, -s, s)


def _rope_tables(positions):
    n = positions.size
    inv_r = ROPE_BASE ** (-jnp.arange(0, RET_DK, 2, dtype=F32) / RET_DK)
    inv_m = ROPE_BASE ** (-jnp.arange(0, MLA_ROPE, 2, dtype=F32) / MLA_ROPE)
    half = MLA_ROPE // 2
    invm = jnp.zeros((LANES,), F32).at[MLA_NOPE:MLA_NOPE + half].set(inv_m).at[MLA_NOPE + half:MLA_QK].set(inv_m)
    t = ROW_TILE
    tab = jax.ShapeDtypeStruct((n, LANES), F32)
    row = pl.BlockSpec((t, LANES), lambda i: (i, 0))
    vec = pl.BlockSpec((1, LANES), lambda i: (0, 0))
    return pl.pallas_call(
        _rope_kernel,
        out_shape=(tab,) * 4,
        grid=(n // t,),
        in_specs=[pl.BlockSpec((t, 1), lambda i: (i, 0)), vec, vec],
        out_specs=(row,) * 4,
        compiler_params=_params(("parallel",)),
        name="rope",
    )(positions.reshape(n, 1), inv_r.reshape(1, LANES), invm.reshape(1, LANES))


EVEN_COLS = 4 * 512 + MLA_Q_LORA + 4 * LANES


def _split_bf16(x):
    hi = x.astype(BF16)
    return hi, (x - hi.astype(F32)).astype(BF16)


def _head_rsqrt(x, ind_ref, spread_ref):
    hi, lo = _split_bf16(x * x)
    s = _dot(hi, ind_ref[...]) + _dot(lo, ind_ref[...])
    hi, lo = _split_bf16(lax.rsqrt(s * (1.0 / MLA_QK) + EPS))
    return _dot(hi, spread_ref[...]) + _dot(lo, spread_ref[...])


def _in_even_kernel(x_ref, g_ref, sc_ref, sh_ref, w_ref, qng_ref, kvng_ref, wqb_ref, wqs_ref, wkn_ref, wv_ref,
                    qg_ref, qgs_ref, kg_ref, kgs_ref, ind_ref, spread_ref, cm_ref, sm_ref,
                    qk_ref, v_ref, o_ref, misc_ref, q_out, k_out, v_out):
    h = _norm_mod(x_ref[...], g_ref[...], sc_ref[0], sh_ref[0]).astype(BF16)
    qk_ref[...] = _dot(h, w_ref[:, 0:1024])
    v_ref[...] = _dot(h, w_ref[:, 1024:1536]).astype(BF16)
    o_ref[...] = _dot(h, w_ref[:, 1536:2048])
    rest = _dot(h, w_ref[:, 2048:EVEN_COLS])
    q_a = rest[:, 0:256]
    kv_a = rest[:, 256:384]
    krp = rest[:, 384:512]
    krs = rest[:, 512:640]
    misc_ref[...] = rest[:, 640:768]
    qn = (_rms(q_a) * qng_ref[...]).astype(BF16)
    kvn = (_rms(kv_a) * kvng_ref[...]).astype(BF16)
    q = _dot(qn, wqb_ref[...])
    qs = _dot(qn, wqs_ref[...])
    k = _dot(kvn, wkn_ref[...]) + jnp.concatenate([krp] * MLA_HEADS, axis=1)
    v_out[...] = _dot(kvn, wv_ref[...]).astype(BF16)
    q_rs = _head_rsqrt(q, ind_ref, spread_ref)
    k_rs = _head_rsqrt(k, ind_ref, spread_ref)
    cm, sm = cm_ref[...], sm_ref[...]
    q_c, q_s = qg_ref[...] * cm, qgs_ref[...] * sm
    k_c, k_s = kg_ref[...] * cm, kgs_ref[...] * sm
    ks = krs * k_s
    for hd in range(MLA_HEADS):
        sl = slice(hd * MLA_PAD, (hd + 1) * MLA_PAD)
        q_out[:, sl] = ((q[:, sl] * q_c + qs[:, sl] * q_s) * q_rs[:, sl]).astype(BF16)
        k_out[:, sl] = ((k[:, sl] * k_c + ks) * k_rs[:, sl]).astype(BF16)


def _in_even(x, mod, b_of, norm_g, w, qng, kvng, wqb, wqs, wkn, wv, qg, qgs, kg, kgs, ind, spread, cm, sm):
    n = x.shape[0]
    t = ROW_TILE
    full = lambda shape: pl.BlockSpec(shape, lambda i: (0,) * len(shape))
    row = lambda c: pl.BlockSpec((t, c), lambda i: (i, 0))
    out_shape = (
        jax.ShapeDtypeStruct((n, 1024), F32),
        jax.ShapeDtypeStruct((n, 512), BF16),
        jax.ShapeDtypeStruct((n, 512), F32),
        jax.ShapeDtypeStruct((n, LANES), F32),
        jax.ShapeDtypeStruct((n, MLA_W), BF16),
        jax.ShapeDtypeStruct((n, MLA_W), BF16),
        jax.ShapeDtypeStruct((n, 512), BF16),
    )
    vec = full((1, LANES))
    return pl.pallas_call(
        _in_even_kernel,
        out_shape=out_shape,
        grid=(n // t,),
        in_specs=[
            row(D), full((1, D)),
            pl.BlockSpec((1, 1, D), lambda i: (b_of(i, t) * 6 + 1, 0, 0)),
            pl.BlockSpec((1, 1, D), lambda i: (b_of(i, t) * 6 + 0, 0, 0)),
            full((D, EVEN_COLS)), full((1, 256)), full((1, 128)), full((256, MLA_W)), full((256, MLA_W)),
            full((128, MLA_W)), full((128, 512)), vec, vec, vec, vec, full((MLA_W, LANES)), full((LANES, MLA_W)),
            row(LANES), row(LANES),
        ],
        out_specs=(row(1024), row(512), row(512), row(LANES), row(MLA_W), row(MLA_W), row(512)),
        compiler_params=_params(("parallel",)),
        name="in_even",
    )(x, norm_g, mod, mod, w, qng, kvng, wqb, wqs, wkn, wv, qg, qgs, kg, kgs, ind, spread, cm, sm)


def _conv_kernel(seq_tiles, prev_ref, x_ref, next_ref, w_ref, b_ref, s_ref, o_ref, buf):
    i = pl.program_id(0)
    t = x_ref.shape[0]
    first = (i % seq_tiles) == 0
    last = (i % seq_tiles) == seq_tiles - 1
    buf[0:8, :] = jnp.where(first, 0.0, prev_ref[...])
    buf[8:8 + t, :] = x_ref[...]
    buf[8 + t:16 + t, :] = jnp.where(last, 0.0, next_ref[...])
    pad = ML_CONV // 2
    acc = b_ref[...] + w_ref[0:1, :] * buf[8 - pad:8 - pad + t, :]
    for k in range(1, ML_CONV):
        acc = acc + w_ref[k:k + 1, :] * buf[8 - pad + k:8 - pad + k + t, :]
    o_ref[...] = (acc * jax.nn.sigmoid(acc) * s_ref[...]).astype(BF16)


def _conv(qk_pre, conv_w, conv_b, seq):
    n, c = qk_pre.shape
    t = ROW_TILE
    nb8 = n // 8
    scale = jnp.concatenate([jnp.ones((512,), F32), jnp.full((512,), ML_DK ** -0.5, F32)]).reshape(1, c)
    return pl.pallas_call(
        functools.partial(_conv_kernel, seq // t),
        out_shape=jax.ShapeDtypeStruct((n, c), BF16),
        grid=(n // t,),
        in_specs=[
            pl.BlockSpec((8, c), lambda i: (jnp.maximum(i * (t // 8) - 1, 0), 0)),
            pl.BlockSpec((t, c), lambda i: (i, 0)),
            pl.BlockSpec((8, c), lambda i: (jnp.minimum((i + 1) * (t // 8), nb8 - 1), 0)),
            pl.BlockSpec((ML_CONV, c), lambda i: (0, 0)),
            pl.BlockSpec((1, c), lambda i: (0, 0)),
            pl.BlockSpec((1, c), lambda i: (0, 0)),
        ],
        out_specs=pl.BlockSpec((t, c), lambda i: (i, 0)),
        scratch_shapes=[pltpu.VMEM((t + 16, c), F32)],
        compiler_params=_params(("parallel",)),
        name="conv",
    )(qk_pre, qk_pre, qk_pre, conv_w, conv_b.reshape(1, c), scale)


def _mlstm_kernel(qf_ref, kf_ref, vf_ref, qb_ref, kb_ref, vb_ref, gcf_ref, gcb_ref, grf_ref, grb_ref,
                  bc_ref, br_ref, of_ref, ob_ref, c_ref, m_ref):
    c = pl.program_id(1)
    L = ML_CHUNK

    @pl.when(c == 0)
    def _():
        c_ref[...] = jnp.zeros_like(c_ref)
        m_ref[...] = jnp.zeros_like(m_ref)

    row = lax.broadcasted_iota(jnp.int32, (L, L), 0)
    col = lax.broadcasted_iota(jnp.int32, (L, L), 1)
    lane = lax.broadcasted_iota(jnp.int32, (L, ML_DV), 1)
    ones_col = jnp.where(lane == 0, 1.0, 0.0).astype(BF16)
    dirs = ((qf_ref, kf_ref, vf_ref, gcf_ref, grf_ref, of_ref, row >= col, row <= col),
            (qb_ref, kb_ref, vb_ref, gcb_ref, grb_ref, ob_ref, row <= col, row >= col))
    for d, (q_ref, k_ref, v_ref, gc_ref, gr_ref, o_ref, causal, causal_t) in enumerate(dirs):
        gc = gc_ref[0] + bc_ref[d]
        gr = gr_ref[0] + br_ref[d]
        i_col = gc[:, 0:ML_HEADS]
        f_col = _log_sigmoid(gc[:, ML_HEADS:2 * ML_HEADS])
        i_row = gr[0:ML_HEADS, :]
        f_row = _log_sigmoid(gr[ML_HEADS:2 * ML_HEADS, :])
        bcum_col = jnp.dot(jnp.where(causal, 1.0, 0.0), f_col, precision=HIGHEST, preferred_element_type=F32)
        bcum_row = jnp.dot(f_row, jnp.where(causal_t, 1.0, 0.0), precision=HIGHEST, preferred_element_type=F32)
        for h in range(ML_HEADS):
            sl = slice(h * ML_DK, (h + 1) * ML_DK)
            st = d * ML_HEADS + h
            q = q_ref[:, sl]
            k = k_ref[:, sl]
            vext = jnp.concatenate([v_ref[:, sl], ones_col], axis=1)
            bc = bcum_col[:, h:h + 1]
            br = bcum_row[h:h + 1, :]
            ir = i_row[h:h + 1, :]
            ic = i_col[:, h:h + 1]
            m_prev = m_ref[st:st + 1, 0:1]
            log_d = jnp.where(causal, bc - br + ir, -jnp.inf)
            m_inter = bc + m_prev
            m_t = jnp.maximum(m_inter, jnp.max(log_d, axis=-1, keepdims=True))
            w = (_dot_nt(q, k) * jnp.exp(log_d - m_t)).astype(BF16)
            a_inter = jnp.exp(m_inter - m_t)
            tot = _dot(w, vext) + a_inter * _dot(q, c_ref[st].astype(BF16))
            den = tot[:, ML_DV:ML_DV + 1]
            o_ref[:, sl] = tot[:, 0:ML_DV] / jnp.maximum(jnp.abs(den), jnp.exp(-m_t))
            total = jnp.sum(f_row[h:h + 1, :], axis=-1, keepdims=True)
            g_row = total - br + ir
            m_new = jnp.maximum(total + m_prev, jnp.max(g_row, axis=-1, keepdims=True))
            a_state = jnp.exp(total + m_prev - m_new)
            kw = (k.astype(F32) * jnp.exp(total - bc + ic - m_new)).astype(BF16)
            c_ref[st] = a_state * c_ref[st] + _dot_tn(kw, vext)
            m_ref[st:st + 1, :] = jnp.broadcast_to(m_new, (1, LANES))


def _mlstm(qk, v, gcol, grow, bcol, brow, batch, seq):
    n = qk.shape[0]
    L = ML_CHUNK
    nc = seq // L
    fw = lambda b, c: b * nc + c
    bw = lambda b, c: b * nc + nc - 1 - c
    out = jax.ShapeDtypeStruct((n, ML_HEADS * ML_DV), F32)
    return pl.pallas_call(
        _mlstm_kernel,
        out_shape=(out, out),
        grid=(batch, nc),
        in_specs=[
            pl.BlockSpec((L, 512), lambda b, c: (fw(b, c), 0)),
            pl.BlockSpec((L, 512), lambda b, c: (fw(b, c), 1)),
            pl.BlockSpec((L, 512), lambda b, c: (fw(b, c), 0)),
            pl.BlockSpec((L, 512), lambda b, c: (bw(b, c), 0)),
            pl.BlockSpec((L, 512), lambda b, c: (bw(b, c), 1)),
            pl.BlockSpec((L, 512), lambda b, c: (bw(b, c), 0)),
            pl.BlockSpec((1, L, 8), lambda b, c: (0, fw(b, c), 0)),
            pl.BlockSpec((1, L, 8), lambda b, c: (1, bw(b, c), 0)),
            pl.BlockSpec((1, 8, L), lambda b, c: (0, 0, fw(b, c))),
            pl.BlockSpec((1, 8, L), lambda b, c: (1, 0, bw(b, c))),
            pl.BlockSpec((2, 1, 8), lambda b, c: (0, 0, 0)),
            pl.BlockSpec((2, 8, 1), lambda b, c: (0, 0, 0)),
        ],
        out_specs=(pl.BlockSpec((L, 512), lambda b, c: (fw(b, c), 0)),
                   pl.BlockSpec((L, 512), lambda b, c: (bw(b, c), 0))),
        scratch_shapes=[pltpu.VMEM((2 * ML_HEADS, ML_DK, 2 * ML_DV), F32), pltpu.VMEM((8, LANES), F32)],
        compiler_params=_params(("parallel", "arbitrary")),
        name="mlstm",
    )(qk, qk, v, qk, qk, v, gcol, gcol, grow, grow, bcol, brow)


def _attn_kernel(q_ref, k_ref, v_ref, o_ref, vext):
    @pl.when(pl.program_id(2) == 0)
    def _():
        lane = lax.broadcasted_iota(jnp.int32, (v_ref.shape[0], MLA_PAD - MLA_V), 1)
        ones_col = jnp.where(lane == 0, 1.0, 0.0).astype(BF16)
        for j in range(2):
            vext[:, j * MLA_PAD:(j + 1) * MLA_PAD] = jnp.concatenate(
                [v_ref[:, j * MLA_V:(j + 1) * MLA_V], ones_col], axis=1)

    outs = []
    for j in range(2):
        q = q_ref[:, j * MLA_PAD:(j + 1) * MLA_PAD]
        k = k_ref[:, j * MLA_PAD:(j + 1) * MLA_PAD]
        s = _dot_nt(q, k)
        p = jnp.exp2(s - jnp.max(s, axis=-1, keepdims=True))
        oe = _dot(p.astype(BF16), vext[:, j * MLA_PAD:(j + 1) * MLA_PAD])
        outs.append(oe[:, 0:MLA_V] / oe[:, MLA_V:MLA_V + 1])
    o_ref[...] = jnp.concatenate(outs, axis=1).astype(BF16)


def _attn(q, k, v, batch, seq):
    n = q.shape[0]
    t = min(ATTN_Q_TILE, seq)
    nq = seq // t
    return pl.pallas_call(
        _attn_kernel,
        out_shape=jax.ShapeDtypeStruct((n, MLA_HEADS * MLA_V), BF16),
        grid=(batch, MLA_HEADS // 2, nq),
        in_specs=[
            pl.BlockSpec((t, 2 * MLA_PAD), lambda b, h, i: (b * nq + i, h)),
            pl.BlockSpec((seq, 2 * MLA_PAD), lambda b, h, i: (b, h)),
            pl.BlockSpec((seq, 2 * MLA_V), lambda b, h, i: (b, h)),
        ],
        out_specs=pl.BlockSpec((t, 2 * MLA_V), lambda b, h, i: (b * nq + i, h)),
        scratch_shapes=[pltpu.VMEM((seq, 2 * MLA_PAD), BF16)],
        compiler_params=_params(("parallel", "parallel", "arbitrary")),
        name="attn",
    )(q, k, v)


def _mixer_tail(y, x_ref, g1_ref, fg_ref, sc2_ref, sh2_ref, rwh_ref, rwl_ref, rb_ref, x_out, hf_out, lg_out):
    x1 = x_ref[...] + g1_ref[0] * y
    x_out[...] = x1
    hf = _norm_mod(x1, fg_ref[...], sc2_ref[0], sh2_ref[0])
    hf_out[...] = hf
    hi, lo = _split_bf16(hf)
    lg_out[...] = _dot(hi, rwh_ref[...]) + (_dot(lo, rwh_ref[...]) + _dot(hi, rwl_ref[...])) + rb_ref[...]


def _tail_specs(n, t, b_of):
    full = lambda shape: pl.BlockSpec(shape, lambda i: (0,) * len(shape))
    mod = lambda j: pl.BlockSpec((1, 1, D), lambda i: (b_of(i, t) * 6 + j, 0, 0))
    row = lambda c: pl.BlockSpec((t, c), lambda i: (i, 0))
    in_specs = [row(D), mod(2), full((1, D)), mod(4), mod(3), full((D, LANES)), full((D, LANES)), full((1, LANES))]
    out_shape = (jax.ShapeDtypeStruct((n, D), F32), jax.ShapeDtypeStruct((n, D), F32),
                 jax.ShapeDtypeStruct((n, LANES), F32))
    out_specs = (row(D), row(D), row(LANES))
    return in_specs, out_shape, out_specs


def _out_even_kernel(hf_ref, hb_ref, om_ref, oa_ref, mg_ref, wt_ref, wb_ref, *tail):
    hs = hf_ref[...] + hb_ref[...]
    gate = jax.nn.sigmoid(om_ref[...])
    parts = []
    for h in range(ML_HEADS):
        sl = slice(h * ML_DV, (h + 1) * ML_DV)
        parts.append(_rms(hs[:, sl]) * mg_ref[:, sl] * gate[:, sl])
    hml = jnp.concatenate(parts, axis=1).astype(BF16)
    y = _dot(hml, wt_ref[...]) + _dot(oa_ref[...], wb_ref[...])
    _mixer_tail(y, *tail)


def _out_even(h_f, h_b, o_m, o_mla, ml_g, w_top, w_bot, x, mod, b_of, ffn_g, rw, rb):
    n = x.shape[0]
    t = ROW_TILE
    full = lambda shape: pl.BlockSpec(shape, lambda i: (0,) * len(shape))
    row = lambda c: pl.BlockSpec((t, c), lambda i: (i, 0))
    tin, out_shape, out_specs = _tail_specs(n, t, b_of)
    return pl.pallas_call(
        _out_even_kernel,
        out_shape=out_shape,
        grid=(n // t,),
        in_specs=[row(512), row(512), row(512), row(512), full((1, 512)), full((512, D)), full((512, D))] + tin,
        out_specs=out_specs,
        compiler_params=_params(("parallel",)),
        name="out_even",
    )(h_f, h_b, o_m, o_mla, ml_g, w_top, w_bot, x, mod, ffn_g, mod, mod, *rw, rb)


def _in_odd_kernel(x_ref, g_ref, sc_ref, sh_ref, w_ref, cr_ref, sr_ref, q_out, k_out, v_out, g_out):
    h = _norm_mod(x_ref[...], g_ref[...], sc_ref[0], sh_ref[0]).astype(BF16)
    cos, sin = cr_ref[...], sr_ref[...]
    hw = RET_DK // 2
    for idx, (dst, scale) in enumerate(((q_out, RET_DK ** -0.5), (k_out, 1.0))):
        z = _dot(h, w_ref[:, idx * 1024:(idx + 1) * 1024])
        for hd in range(RET_HEADS):
            x1 = z[:, hd * RET_DK:hd * RET_DK + hw]
            x2 = z[:, hd * RET_DK + hw:(hd + 1) * RET_DK]
            dst[:, hd * RET_DK:hd * RET_DK + hw] = ((x1 * cos - x2 * sin) * scale).astype(BF16)
            dst[:, hd * RET_DK + hw:(hd + 1) * RET_DK] = ((x2 * cos + x1 * sin) * scale).astype(BF16)
    v_out[...] = _dot(h, w_ref[:, 2048:4096]).astype(BF16)
    g_out[...] = _dot(h, w_ref[:, 4096:6144])


def _in_odd(x, mod, b_of, norm_g, w, cr, sr):
    n = x.shape[0]
    t = ROW_TILE
    full = lambda shape: pl.BlockSpec(shape, lambda i: (0,) * len(shape))
    row = lambda c: pl.BlockSpec((t, c), lambda i: (i, 0))
    out_shape = (
        jax.ShapeDtypeStruct((n, 1024), BF16), jax.ShapeDtypeStruct((n, 1024), BF16),
        jax.ShapeDtypeStruct((n, 2048), BF16), jax.ShapeDtypeStruct((n, 2048), F32),
    )
    return pl.pallas_call(
        _in_odd_kernel,
        out_shape=out_shape,
        grid=(n // t,),
        in_specs=[
            row(D), full((1, D)),
            pl.BlockSpec((1, 1, D), lambda i: (b_of(i, t) * 6 + 1, 0, 0)),
            pl.BlockSpec((1, 1, D), lambda i: (b_of(i, t) * 6 + 0, 0, 0)),
            full((D, 6144)), row(LANES), row(LANES),
        ],
        out_specs=(row(1024), row(1024), row(2048), row(2048)),
        compiler_params=_params(("parallel",)),
        name="in_odd",
    )(x, norm_g, mod, mod, w, cr, sr)


def _ret_kernel(qf_ref, kf_ref, vf_ref, qb_ref, kb_ref, vb_ref, dec_ref, of_ref, ob_ref, s_ref):
    c = pl.program_id(1)
    L = RET_CHUNK

    @pl.when(c == 0)
    def _():
        s_ref[...] = jnp.zeros_like(s_ref)

    row = lax.broadcasted_iota(jnp.int32, (L, L), 0)
    col = lax.broadcasted_iota(jnp.int32, (L, L), 1)
    pos = lax.broadcasted_iota(jnp.int32, (L, 1), 0)
    dirs = ((qf_ref, kf_ref, vf_ref, of_ref, row - col, pos + 1, L - 1 - pos),
            (qb_ref, kb_ref, vb_ref, ob_ref, col - row, L - pos, pos))
    for d, (q_ref, k_ref, v_ref, o_ref, rel_i, exp_q_i, exp_k_i) in enumerate(dirs):
        rel = rel_i.astype(F32)
        exp_q = exp_q_i.astype(F32)
        exp_k = exp_k_i.astype(F32)
        for h in range(RET_HEADS):
            st = d * RET_HEADS + h
            lg = _log_sigmoid(dec_ref[d, h:h + 1, :])
            lg1 = lg[:, 0:1]
            decay_in = jnp.where(rel >= 0, jnp.exp(jnp.maximum(rel, 0.0) * lg1), 0.0)
            q = q_ref[:, h * RET_DK:(h + 1) * RET_DK]
            k = k_ref[:, h * RET_DK:(h + 1) * RET_DK]
            v = v_ref[:, h * RET_DV:(h + 1) * RET_DV]
            sc = (_dot_nt(q, k) * decay_in).astype(BF16)
            o_ref[:, h * RET_DV:(h + 1) * RET_DV] = (
                _dot(sc, v) + jnp.exp(exp_q * lg1) * _dot(q, s_ref[st].astype(BF16)))
            kd = (k.astype(F32) * jnp.exp(exp_k * lg1)).astype(BF16)
            s_ref[st] = jnp.exp(L * lg1) * s_ref[st] + _dot_tn(kd, v)


def _retention(q, k, v, dec, batch, seq):
    n = q.shape[0]
    L = RET_CHUNK
    nc = seq // L
    fw = lambda b, c: (b * nc + c, 0)
    bw = lambda b, c: (b * nc + nc - 1 - c, 0)
    out = jax.ShapeDtypeStruct((n, RET_HEADS * RET_DV), F32)
    return pl.pallas_call(
        _ret_kernel,
        out_shape=(out, out),
        grid=(batch, nc),
        in_specs=[
            pl.BlockSpec((L, 1024), fw), pl.BlockSpec((L, 1024), fw), pl.BlockSpec((L, 2048), fw),
            pl.BlockSpec((L, 1024), bw), pl.BlockSpec((L, 1024), bw), pl.BlockSpec((L, 2048), bw),
            pl.BlockSpec((2, RET_HEADS, LANES), lambda b, c: (0, 0, 0)),
        ],
        out_specs=(pl.BlockSpec((L, 2048), fw), pl.BlockSpec((L, 2048), bw)),
        scratch_shapes=[pltpu.VMEM((2 * RET_HEADS, RET_DK, RET_DV), F32)],
        compiler_params=_params(("parallel", "arbitrary")),
        name="retention",
    )(q, k, v, q, k, v, dec)


def _out_odd_kernel(yf_ref, yb_ref, g_ref, gn_ref, w_ref, *tail):
    ys = yf_ref[...] + yb_ref[...]
    g = g_ref[...]
    gate = g * jax.nn.sigmoid(g)
    parts = []
    for h in range(RET_HEADS):
        sl = slice(h * RET_DV, (h + 1) * RET_DV)
        yh = ys[:, sl]
        yc = yh - jnp.mean(yh, axis=-1, keepdims=True)
        var = jnp.mean(yc * yc, axis=-1, keepdims=True)
        parts.append(yc * lax.rsqrt(var + EPS) * gn_ref[:, sl] * gate[:, sl])
    yn = jnp.concatenate(parts, axis=1).astype(BF16)
    _mixer_tail(_dot(yn, w_ref[...]), *tail)


def _out_odd(y_f, y_b, g, gn_g, w_out, x, mod, b_of, ffn_g, rw, rb):
    n = x.shape[0]
    t = ROW_TILE
    full = lambda shape: pl.BlockSpec(shape, lambda i: (0,) * len(shape))
    row = lambda c: pl.BlockSpec((t, c), lambda i: (i, 0))
    tin, out_shape, out_specs = _tail_specs(n, t, b_of)
    return pl.pallas_call(
        _out_odd_kernel,
        out_shape=out_shape,
        grid=(n // t,),
        in_specs=[row(2048), row(2048), row(2048), full((1, 2048)), full((2048, D))] + tin,
        out_specs=out_specs,
        compiler_params=_params(("parallel",)),
        name="out_odd",
    )(y_f, y_b, g, gn_g, w_out, x, mod, ffn_g, mod, mod, *rw, rb)


def _route_kernel(lg_ref, e_out, r_out, w_out, cnt_out, base_ref):
    i = pl.program_id(0)
    t = lg_ref.shape[0]

    @pl.when(i == 0)
    def _():
        base_ref[...] = jnp.zeros_like(base_ref)

    l = lg_ref[...]
    lane = lax.broadcasted_iota(jnp.int32, l.shape, 1)
    sel = jnp.zeros(l.shape, F32)
    vals, idxs, hots = [], [], []
    for _k in range(TOP_K):
        m = jnp.max(l, axis=-1, keepdims=True)
        idx = jnp.min(jnp.where(l == m, lane, LANES), axis=-1, keepdims=True)
        hot = lane == idx
        vals.append(m)
        idxs.append(idx)
        hots.append(hot)
        sel = sel + jnp.where(hot, 1.0, 0.0)
        l = jnp.where(hot, -jnp.inf, l)
    es = [jnp.exp(v - vals[0]) for v in vals]
    den = es[0] + es[1] + es[2] + es[3]
    row = lax.broadcasted_iota(jnp.int32, (t, t), 0)
    col = lax.broadcasted_iota(jnp.int32, (t, t), 1)
    before = jnp.where(row > col, 1.0, 0.0).astype(BF16)
    rank_mat = _dot(before, sel.astype(BF16)) + base_ref[...]
    e_acc = jnp.zeros(l.shape, jnp.int32)
    r_acc = jnp.zeros(l.shape, jnp.int32)
    w_acc = jnp.zeros(l.shape, F32)
    for k in range(TOP_K):
        rk = jnp.sum(jnp.where(hots[k], rank_mat, 0.0), axis=-1, keepdims=True).astype(jnp.int32)
        e_acc = jnp.where(lane == k, idxs[k], e_acc)
        r_acc = jnp.where(lane == k, rk, r_acc)
        w_acc = jnp.where(lane == k, es[k] / den, w_acc)
    e_out[...] = e_acc
    r_out[...] = r_acc
    w_out[...] = w_acc
    base_ref[...] = base_ref[...] + jnp.sum(sel, axis=0, keepdims=True)
    cnt_out[...] = base_ref[...]


def _route(logits):
    n = logits.shape[0]
    t = ROUTE_TILE
    row = pl.BlockSpec((t, LANES), lambda i: (i, 0))
    return pl.pallas_call(
        _route_kernel,
        out_shape=(jax.ShapeDtypeStruct((n, LANES), jnp.int32), jax.ShapeDtypeStruct((n, LANES), jnp.int32),
                   jax.ShapeDtypeStruct((n, LANES), F32), jax.ShapeDtypeStruct((1, LANES), F32)),
        grid=(n // t,),
        in_specs=[row],
        out_specs=(row, row, row, pl.BlockSpec((1, LANES), lambda i: (0, 0))),
        scratch_shapes=[pltpu.VMEM((1, LANES), F32)],
        compiler_params=_params(("arbitrary",)),
        name="route",
    )(logits)


def _dispatch_kernel(dest_ref, pad_ref, hf_ref, xs_ref, sem):
    t = hf_ref.shape[0]

    def row_copy(src_row, dst_row):
        return pltpu.make_async_copy(hf_ref.at[pl.ds(src_row, 1)], xs_ref.at[pl.ds(dst_row, 1)], sem)

    def wait_rows():
        pltpu.make_async_copy(hf_ref, xs_ref.at[pl.ds(0, t)], sem).wait()

    def issue(g, carry):
        for u in range(MOVE_UNROLL):
            tok = g * MOVE_UNROLL + u
            for k in range(TOP_K):
                row_copy(tok, dest_ref[0, 0, tok * TOP_K + k]).start(priority=k % 2)
        return carry

    lax.fori_loop(0, t // MOVE_UNROLL, issue, 0)
    for _ in range(TOP_K):
        wait_rows()

    @pl.when(pl.program_id(0) == 0)
    def _():
        per_iter = MOVE_UNROLL * TOP_K
        n_free = pad_ref.shape[-1]

        def pad_issue(g, carry):
            for u in range(per_iter):
                row_copy(0, pad_ref[0, 0, g * per_iter + u]).start(priority=u % 2)
            return carry

        lax.fori_loop(0, n_free // per_iter, pad_issue, 0)
        for _ in range(n_free // t):
            wait_rows()


def _dispatch(hf, dest, pad_dest, rows):
    n = hf.shape[0]
    t = DISPATCH_TILE
    n_free = pad_dest.shape[0]
    assert n_free % t == 0 and t % MOVE_UNROLL == 0
    return pl.pallas_call(
        _dispatch_kernel,
        out_shape=jax.ShapeDtypeStruct((rows, D), F32),
        grid=(n // t,),
        in_specs=[
            pl.BlockSpec((1, 1, t * TOP_K), lambda i: (i, 0, 0), memory_space=pltpu.SMEM),
            pl.BlockSpec((1, 1, n_free), lambda i: (0, 0, 0), memory_space=pltpu.SMEM),
            pl.BlockSpec((t, D), lambda i: (i, 0)),
        ],
        out_specs=pl.BlockSpec(memory_space=pl.ANY),
        scratch_shapes=[pltpu.SemaphoreType.DMA(())],
        compiler_params=_params(("arbitrary",), has_side_effects=True),
        name="dispatch",
    )(dest.reshape(n // t, 1, t * TOP_K), pad_dest.reshape(1, 1, n_free), hf)


def _expert_kernel(be_ref, nu_ref, xs_ref, wgu_ref, bgu_ref, wd_ref, bd_ref, ys_ref, wgu_bf, wd_bf):
    i = pl.program_id(0)
    fresh = jnp.logical_or(i == 0, be_ref[i] != be_ref[jnp.maximum(i - 1, 0)])

    @pl.when(fresh)
    def _():
        wgu_bf[...] = wgu_ref[0, 0].astype(BF16)
        wd_bf[...] = wd_ref[0, 0].astype(BF16)

    @pl.when(i < nu_ref[0])
    def _():
        gu = _dot(xs_ref[...].astype(BF16), wgu_bf[...]) + bgu_ref[0, 0]
        gate = jnp.minimum(gu[:, :EXPERT_FF], SWIGLU_LIMIT)
        up = jnp.clip(gu[:, EXPERT_FF:], -SWIGLU_LIMIT, SWIGLU_LIMIT)
        act = (up + 1.0) * gate * jax.nn.sigmoid(SWIGLU_ALPHA * gate)
        ys_ref[...] = _dot(act.astype(BF16), wd_bf[...]) + bd_ref[0, 0]

    @pl.when(i >= nu_ref[0])
    def _():
        ys_ref[...] = jnp.zeros_like(ys_ref)


def _experts(xs, nb, layer, blk_expert, n_used, w_gu, b_gu, w_down, b_down):
    grid_spec = pltpu.PrefetchScalarGridSpec(
        num_scalar_prefetch=2,
        grid=(nb,),
        in_specs=[
            pl.BlockSpec((MOE_BLOCK, D), lambda i, be, nu: (jnp.minimum(i, nu[0] - 1), 0)),
            pl.BlockSpec((1, 1, D, 2 * EXPERT_FF), lambda i, be, nu: (layer, be[i], 0, 0)),
            pl.BlockSpec((1, 1, 1, 2 * EXPERT_FF), lambda i, be, nu: (layer, be[i], 0, 0)),
            pl.BlockSpec((1, 1, EXPERT_FF, D), lambda i, be, nu: (layer, be[i], 0, 0)),
            pl.BlockSpec((1, 1, 1, D), lambda i, be, nu: (layer, be[i], 0, 0)),
        ],
        out_specs=pl.BlockSpec((MOE_BLOCK, D), lambda i, be, nu: (i, 0)),
        scratch_shapes=[pltpu.VMEM((D, 2 * EXPERT_FF), BF16), pltpu.VMEM((EXPERT_FF, D), BF16)],
    )
    return pl.pallas_call(
        _expert_kernel,
        out_shape=jax.ShapeDtypeStruct((nb * MOE_BLOCK, D), F32),
        grid_spec=grid_spec,
        compiler_params=_params(("arbitrary",)),
        name="experts",
    )(blk_expert, n_used, xs, w_gu, b_gu.reshape(DEPTH, N_EXPERTS, 1, 2 * EXPERT_FF), w_down,
      b_down.reshape(DEPTH, N_EXPERTS, 1, D))


def _combine_kernel(dest0_ref, dest_ref, ys_ref, wt_ref, x_ref, g2_ref, o_ref, buf_a, buf_b, sem_a, sem_b):
    i = pl.program_id(0)
    t = x_ref.shape[0]

    def start_rows(idx_ref, tok, buf, sem):
        for k in range(TOP_K):
            pltpu.make_async_copy(ys_ref.at[pl.ds(idx_ref[0, 0, tok * TOP_K + k], 1)],
                                  buf.at[k, pl.ds(tok, 1)], sem).start(priority=k % 2)

    def wait_tile(buf, sem):
        for k in range(TOP_K):
            pltpu.make_async_copy(ys_ref.at[pl.ds(0, t)], buf.at[k], sem).wait()

    @pl.when(i == 0)
    def _():
        def issue(g, carry):
            for u in range(MOVE_UNROLL):
                start_rows(dest0_ref, g * MOVE_UNROLL + u, buf_a, sem_a)
            return carry
        lax.fori_loop(0, t // MOVE_UNROLL, issue, 0)
        wait_tile(buf_a, sem_a)

    g2 = g2_ref[0]

    def step(cur, nxt, sem_nxt):
        def body(g, carry):
            for u in range(MOVE_UNROLL):
                start_rows(dest_ref, g * MOVE_UNROLL + u, nxt, sem_nxt)
            rows = pl.ds(pl.multiple_of(g * MOVE_UNROLL, MOVE_UNROLL), MOVE_UNROLL)
            wt = wt_ref[rows, :]
            acc = cur[0, rows, :] * wt[:, 0:1]
            for k in range(1, TOP_K):
                acc = acc + cur[k, rows, :] * wt[:, k:k + 1]
            o_ref[rows, :] = x_ref[rows, :] + g2 * acc
            return carry
        lax.fori_loop(0, t // MOVE_UNROLL, body, 0)
        wait_tile(nxt, sem_nxt)

    @pl.when(i % 2 == 0)
    def _():
        step(buf_a, buf_b, sem_b)

    @pl.when(i % 2 == 1)
    def _():
        step(buf_b, buf_a, sem_a)


def _combine(ys, dest, wts, x, mod, b_of):
    n = x.shape[0]
    t = MOVE_TILE
    nt = n // t
    dest3 = dest.reshape(nt, 1, t * TOP_K)
    idx = lambda f: pl.BlockSpec((1, 1, t * TOP_K), lambda i: (f(i), 0, 0), memory_space=pltpu.SMEM)
    return pl.pallas_call(
        _combine_kernel,
        out_shape=jax.ShapeDtypeStruct((n, D), F32),
        grid=(nt,),
        in_specs=[
            idx(lambda i: 0),
            idx(lambda i: jnp.minimum(i + 1, nt - 1)),
            pl.BlockSpec(memory_space=pl.ANY),
            pl.BlockSpec((t, LANES), lambda i: (i, 0)),
            pl.BlockSpec((t, D), lambda i: (i, 0)),
            pl.BlockSpec((1, 1, D), lambda i: (b_of(i, t) * 6 + 5, 0, 0)),
        ],
        out_specs=pl.BlockSpec((t, D), lambda i: (i, 0)),
        scratch_shapes=[pltpu.VMEM((TOP_K, t, D), F32), pltpu.VMEM((TOP_K, t, D), F32),
                        pltpu.SemaphoreType.DMA(()), pltpu.SemaphoreType.DMA(())],
        compiler_params=_params(("arbitrary",)),
        name="combine",
    )(dest3, dest3, ys, wts, x, mod)


def _moe(x1, hf, logits, mod, b_of, layer, w_gu, b_gu, w_down, b_down):
    n = x1.shape[0]
    eidx, rank, wts, counts = _route(logits)
    experts = jnp.arange(N_EXPERTS, dtype=jnp.int32)
    cnt = counts[0, :N_EXPERTS].astype(jnp.int32)
    padded = (cnt + MOE_BLOCK - 1) // MOE_BLOCK * MOE_BLOCK
    pend = jnp.cumsum(padded)
    pstart = pend - padded
    e4 = eidx[:, :TOP_K]
    dest = rank[:, :TOP_K] + jnp.sum(jnp.where(e4[:, :, None] == experts, pstart, 0), axis=-1)
    nb = -(-(n * TOP_K) // MOE_BLOCK) + N_EXPERTS
    blk_start = jnp.arange(nb, dtype=jnp.int32) * MOE_BLOCK
    blk_expert = jnp.minimum(jnp.sum((pend[None, :] <= blk_start[:, None]).astype(jnp.int32), axis=1),
                             N_EXPERTS - 1)
    n_used = (pend[-1:] // MOE_BLOCK).astype(jnp.int32)
    n_free = nb * MOE_BLOCK - n * TOP_K
    free_end = jnp.cumsum(padded - cnt)
    seg_first = jnp.concatenate([pstart + cnt, pend[-1:]])
    seg_skip = jnp.concatenate([free_end - (padded - cnt), free_end[-1:]])
    jj = jnp.arange(n_free, dtype=jnp.int32)
    seg = jnp.sum((free_end[None, :] <= jj[:, None]).astype(jnp.int32), axis=1)
    hot = seg[:, None] == jnp.arange(N_EXPERTS + 1, dtype=jnp.int32)
    pad_dest = jj + jnp.sum(jnp.where(hot, seg_first - seg_skip, 0), axis=1)
    xs = _dispatch(hf, dest, pad_dest, nb * MOE_BLOCK)
    ys = _experts(xs, nb, layer, blk_expert, n_used, w_gu, b_gu, w_down, b_down)
    return _combine(ys, dest, wts, x1, mod, b_of)


def _head_tile(t, partner):
    half = MLA_ROPE // 2
    lead = t.shape[:-1]
    tail = jnp.zeros(lead + (MLA_PAD - MLA_QK,), t.dtype)
    if not partner:
        return jnp.concatenate([t, tail], axis=-1)
    return jnp.concatenate([jnp.zeros(lead + (MLA_NOPE,), t.dtype), t[..., MLA_NOPE + half:],
                            t[..., MLA_NOPE:MLA_NOPE + half], tail], axis=-1)


def _prep_even(w_in, gate_b, w_qb, w_kvb, qk_q_g, qk_k_g, w_out):
    q_m, k_m, v_m, o_m, gates, q_a, kv_a, k_r = jnp.split(
        w_in, [512, 1024, 1536, 2048, 2064, 2320, 2448], axis=1)
    zeros = lambda c: jnp.zeros((D, c), F32)
    k_r96 = jnp.concatenate([zeros(MLA_NOPE), k_r], axis=1)
    misc = jnp.concatenate([gates, zeros(LANES - 16)], axis=1)
    w = jnp.concatenate([q_m, k_m, v_m, o_m, q_a, kv_a, _head_tile(k_r96, False), _head_tile(k_r96, True), misc],
                        axis=1).astype(BF16)
    qb = w_qb.reshape(MLA_Q_LORA, MLA_HEADS, MLA_QK)
    wqb = _head_tile(qb, False).reshape(MLA_Q_LORA, MLA_W).astype(BF16)
    wqs = _head_tile(qb, True).reshape(MLA_Q_LORA, MLA_W).astype(BF16)
    kvb = w_kvb.reshape(MLA_KV_LORA, MLA_HEADS, MLA_NOPE + MLA_V)
    wkn = jnp.pad(kvb[:, :, :MLA_NOPE], ((0, 0), (0, 0), (0, MLA_PAD - MLA_NOPE)))
    wkn = wkn.reshape(MLA_KV_LORA, MLA_W).astype(BF16)
    wv = kvb[:, :, MLA_NOPE:].reshape(MLA_KV_LORA, MLA_HEADS * MLA_V).astype(BF16)
    qg96 = qk_q_g * (MLA_QK ** -0.5 * LOG2_E)
    qg = _head_tile(qg96, False).reshape(1, MLA_PAD)
    qgs = _head_tile(qg96, True).reshape(1, MLA_PAD)
    kg = _head_tile(qk_k_g, False).reshape(1, MLA_PAD)
    kgs = _head_tile(qk_k_g, True).reshape(1, MLA_PAD)
    gb = gate_b.reshape(2, 8)
    w_top = w_out[:512].astype(BF16)
    w_bot = w_out[512:].astype(BF16)
    return w, wqb, wqs, wkn, wv, qg, qgs, kg, kgs, gb.reshape(2, 1, 8), gb.reshape(2, 8, 1), w_top, w_bot


def kernel(x, c, positions, ada_w, ada_b, norm_mix_g, norm_ffn_g, hy_w_in, ml_conv_w, ml_conv_b, ml_gate_b, ml_norm_g, mla_q_norm_g, mla_kv_norm_g, mla_w_qb, mla_w_kvb, mla_qk_q_g, mla_qk_k_g, hy_w_out, ret_w_in, ret_decay_f, ret_decay_b, ret_gn_g, ret_w_out, moe_router_w, moe_router_b, moe_w_gu, moe_b_gu, moe_w_down, moe_b_down):
    batch, seq, _ = x.shape
    n = batch * seq

    def b_of(i, t):
        return (i * t) // seq

    mod_all = _ada(c, ada_w, ada_b).reshape(DEPTH, batch * 6, 1, D)
    cr, sr, cm, sm = _rope_tables(positions)
    head_of_lane = jnp.arange(MLA_W, dtype=jnp.int32) // MLA_PAD
    ind = (head_of_lane[:, None] == jnp.arange(LANES, dtype=jnp.int32)[None, :]).astype(BF16)
    spread = ind.T
    xf = x.reshape(n, D)
    for layer in range(DEPTH):
        mod = mod_all[layer]
        j = layer // 2
        mix_g = norm_mix_g[layer].reshape(1, D)
        ffn_g = norm_ffn_g[layer].reshape(1, D)
        rw = _split_bf16(jnp.pad(moe_router_w[layer], ((0, 0), (0, LANES - N_EXPERTS))))
        rb = jnp.concatenate([moe_router_b[layer], jnp.full((LANES - N_EXPERTS,), NEG_BIG, F32)]).reshape(1, LANES)
        if layer % 2 == 0:
            w, wqb, wqs, wkn, wv, qg, qgs, kg, kgs, gbc, gbr, w_top, w_bot = _prep_even(
                hy_w_in[j], ml_gate_b[j], mla_w_qb[j], mla_w_kvb[j], mla_qk_q_g[j], mla_qk_k_g[j], hy_w_out[j])
            qk_pre, v_m, o_m, misc, q_a, k_a, v_a = _in_even(
                xf, mod, b_of, mix_g, w, mla_q_norm_g[j].reshape(1, -1), mla_kv_norm_g[j].reshape(1, -1),
                wqb, wqs, wkn, wv, qg, qgs, kg, kgs, ind, spread, cm, sm)
            qk = _conv(qk_pre, ml_conv_w[j], ml_conv_b[j], seq)
            gcol = misc[:, :16].reshape(n, 2, 8).transpose(1, 0, 2)
            grow = gcol.transpose(0, 2, 1)
            h_f, h_b = _mlstm(qk, v_m, gcol, grow, gbc, gbr, batch, seq)
            o_mla = _attn(q_a, k_a, v_a, batch, seq)
            x1, hf, logits = _out_even(h_f, h_b, o_m, o_mla, ml_norm_g[j].reshape(1, -1), w_top, w_bot,
                                       xf, mod, b_of, ffn_g, rw, rb)
        else:
            dec = jnp.broadcast_to(jnp.stack([ret_decay_f[j], ret_decay_b[j]])[:, :, None], (2, RET_HEADS, LANES))
            q_r, k_r, v_r, g_r = _in_odd(xf, mod, b_of, mix_g, ret_w_in[j].astype(BF16), cr, sr)
            y_f, y_b = _retention(q_r, k_r, v_r, dec.astype(F32), batch, seq)
            x1, hf, logits = _out_odd(y_f, y_b, g_r, ret_gn_g[j].reshape(1, -1), ret_w_out[j].astype(BF16),
                                      xf, mod, b_of, ffn_g, rw, rb)
        xf = _moe(x1, hf, logits, mod, b_of, layer, moe_w_gu, moe_b_gu, moe_w_down, moe_b_down)
    return xf.reshape(batch, seq, D)
```

```python
import functools

import jax
import jax.numpy as jnp
from jax import lax
from jax.experimental import pallas as pl
from jax.experimental.pallas import tpu as pltpu

F32 = jnp.float32
BF16 = jnp.bfloat16
HIGHEST = lax.Precision.HIGHEST

D = 1024
DEPTH = 4
EPS = 1e-6
ROPE_BASE = 10000.0

ML_HEADS = 4
ML_DK = 128
ML_DV = 128
ML_CONV = 5
ML_CHUNK = 256

MLA_HEADS = 8
MLA_Q_LORA = 256
MLA_KV_LORA = 128
MLA_NOPE = 64
MLA_ROPE = 32
MLA_V = 64
MLA_QK = MLA_NOPE + MLA_ROPE
MLA_PAD = 128
MLA_W = MLA_HEADS * MLA_PAD

RET_HEADS = 4
RET_DK = 256
RET_DV = 512
RET_CHUNK = 256

N_EXPERTS = 32
TOP_K = 4
EXPERT_FF = 1024
SWIGLU_LIMIT = 7.0
SWIGLU_ALPHA = 1.702
MOE_BLOCK = 512

LANES = 128
NEG_BIG = -1e30
LOG2_E = 1.4426950408889634

ROW_TILE = 512
ATTN_Q_TILE = 256
ROUTE_TILE = 512
DISPATCH_TILE = 1024
MOVE_TILE = 512
MOVE_UNROLL = 8
VMEM_LIMIT = 56 * 1024 * 1024


def _params(semantics, **kw):
    return pltpu.CompilerParams(dimension_semantics=semantics, vmem_limit_bytes=VMEM_LIMIT, **kw)


def _dot(a, b):
    return jnp.dot(a, b, preferred_element_type=F32)


def _dot_nt(a, b):
    return lax.dot_general(a, b, (((1,), (1,)), ((), ())), preferred_element_type=F32)


def _dot_tn(a, b):
    return lax.dot_general(a, b, (((0,), (0,)), ((), ())), preferred_element_type=F32)


def _log_sigmoid(x):
    return jnp.minimum(x, 0.0) - jnp.log1p(jnp.exp(-jnp.abs(x)))


def _rms(x, denom=None):
    n = x.shape[-1] if denom is None else denom
    return x * lax.rsqrt(jnp.sum(x * x, axis=-1, keepdims=True) / n + EPS)


def _norm_mod(x, g, sc, sh):
    return (_rms(x) * g) * (1.0 + sc) + sh


def _ada_kernel(c_ref, w_ref, b_ref, o_ref):
    c = c_ref[...]
    cs = c * jax.nn.sigmoid(c)
    o_ref[0] = jnp.dot(cs, w_ref[0], precision=HIGHEST, preferred_element_type=F32) + b_ref[0]


def _ada(c, ada_w, ada_b):
    b = c.shape[0]
    return pl.pallas_call(
        _ada_kernel,
        out_shape=jax.ShapeDtypeStruct((DEPTH, b, 6 * D), F32),
        grid=(DEPTH, 6),
        in_specs=[
            pl.BlockSpec((b, D), lambda l, j: (0, 0)),
            pl.BlockSpec((1, D, D), lambda l, j: (l, 0, j)),
            pl.BlockSpec((1, 1, D), lambda l, j: (l, 0, j)),
        ],
        out_specs=pl.BlockSpec((1, b, D), lambda l, j: (l, 0, j)),
        compiler_params=_params(("parallel", "parallel")),
        name="ada",
    )(c, ada_w, ada_b.reshape(DEPTH, 1, 6 * D))


def _rope_kernel(pos_ref, invr_ref, invm_ref, cr_ref, sr_ref, cm_ref, sm_ref):
    pos = pos_ref[...].astype(F32)
    ang_r = pos * invr_ref[...]
    cr_ref[...] = jnp.cos(ang_r)
    sr_ref[...] = jnp.sin(ang_r)
    ang_m = pos * invm_ref[...]
    lane = lax.broadcasted_iota(jnp.int32, ang_m.shape, 1)
    s = jnp.sin(ang_m)
    cm_ref[...] = jnp.cos(ang_m)
    sm_ref[...] = jnp.where(lane < MLA_NOPE + MLA_ROPE // 2, -s, s)


def _rope_tables(positions):
    n = positions.size
    inv_r = ROPE_BASE ** (-jnp.arange(0, RET_DK, 2, dtype=F32) / RET_DK)
    inv_m = ROPE_BASE ** (-jnp.arange(0, MLA_ROPE, 2, dtype=F32) / MLA_ROPE)
    half = MLA_ROPE // 2
    invm = jnp.zeros((LANES,), F32).at[MLA_NOPE:MLA_NOPE + half].set(inv_m).at[MLA_NOPE + half:MLA_QK].set(inv_m)
    t = ROW_TILE
    tab = jax.ShapeDtypeStruct((n, LANES), F32)
    row = pl.BlockSpec((t, LANES), lambda i: (i, 0))
    vec = pl.BlockSpec((1, LANES), lambda i: (0, 0))
    return pl.pallas_call(
        _rope_kernel,
        out_shape=(tab,) * 4,
        grid=(n // t,),
        in_specs=[pl.BlockSpec((t, 1), lambda i: (i, 0)), vec, vec],
        out_specs=(row,) * 4,
        compiler_params=_params(("parallel",)),
        name="rope",
    )(positions.reshape(n, 1), inv_r.reshape(1, LANES), invm.reshape(1, LANES))


EVEN_COLS = 4 * 512 + MLA_Q_LORA + 4 * LANES


def _split_bf16(x):
    hi = x.astype(BF16)
    return hi, (x - hi.astype(F32)).astype(BF16)


def _head_rsqrt(x, ind_ref, spread_ref):
    hi, lo = _split_bf16(x * x)
    s = _dot(hi, ind_ref[...]) + _dot(lo, ind_ref[...])
    hi, lo = _split_bf16(lax.rsqrt(s * (1.0 / MLA_QK) + EPS))
    return _dot(hi, spread_ref[...]) + _dot(lo, spread_ref[...])


def _in_even_kernel(x_ref, g_ref, sc_ref, sh_ref, w_ref, qng_ref, kvng_ref, wqb_ref, wqs_ref, wkn_ref, wv_ref,
                    qg_ref, qgs_ref, kg_ref, kgs_ref, ind_ref, spread_ref, cm_ref, sm_ref,
                    qk_ref, v_ref, o_ref, misc_ref, q_out, k_out, v_out):
    h = _norm_mod(x_ref[...], g_ref[...], sc_ref[0], sh_ref[0]).astype(BF16)
    qk_ref[...] = _dot(h, w_ref[:, 0:1024])
    v_ref[...] = _dot(h, w_ref[:, 1024:1536]).astype(BF16)
    o_ref[...] = _dot(h, w_ref[:, 1536:2048])
    rest = _dot(h, w_ref[:, 2048:EVEN_COLS])
    q_a = rest[:, 0:256]
    kv_a = rest[:, 256:384]
    krp = rest[:, 384:512]
    krs = rest[:, 512:640]
    misc_ref[...] = rest[:, 640:768]
    qn = (_rms(q_a) * qng_ref[...]).astype(BF16)
    kvn = (_rms(kv_a) * kvng_ref[...]).astype(BF16)
    q = _dot(qn, wqb_ref[...])
    qs = _dot(qn, wqs_ref[...])
    k = _dot(kvn, wkn_ref[...]) + jnp.concatenate([krp] * MLA_HEADS, axis=1)
    v_out[...] = _dot(kvn, wv_ref[...]).astype(BF16)
    q_rs = _head_rsqrt(q, ind_ref, spread_ref)
    k_rs = _head_rsqrt(k, ind_ref, spread_ref)
    cm, sm = cm_ref[...], sm_ref[...]
    q_c, q_s = qg_ref[...] * cm, qgs_ref[...] * sm
    k_c, k_s = kg_ref[...] * cm, kgs_ref[...] * sm
    ks = krs * k_s
    for hd in range(MLA_HEADS):
        sl = slice(hd * MLA_PAD, (hd + 1) * MLA_PAD)
        q_out[:, sl] = ((q[:, sl] * q_c + qs[:, sl] * q_s) * q_rs[:, sl]).astype(BF16)
        k_out[:, sl] = ((k[:, sl] * k_c + ks) * k_rs[:, sl]).astype(BF16)


def _in_even(x, mod, b_of, norm_g, w, qng, kvng, wqb, wqs, wkn, wv, qg, qgs, kg, kgs, ind, spread, cm, sm):
    n = x.shape[0]
    t = ROW_TILE
    full = lambda shape: pl.BlockSpec(shape, lambda i: (0,) * len(shape))
    row = lambda c: pl.BlockSpec((t, c), lambda i: (i, 0))
    out_shape = (
        jax.ShapeDtypeStruct((n, 1024), F32),
        jax.ShapeDtypeStruct((n, 512), BF16),
        jax.ShapeDtypeStruct((n, 512), F32),
        jax.ShapeDtypeStruct((n, LANES), F32),
        jax.ShapeDtypeStruct((n, MLA_W), BF16),
        jax.ShapeDtypeStruct((n, MLA_W), BF16),
        jax.ShapeDtypeStruct((n, 512), BF16),
    )
    vec = full((1, LANES))
    return pl.pallas_call(
        _in_even_kernel,
        out_shape=out_shape,
        grid=(n // t,),
        in_specs=[
            row(D), full((1, D)),
            pl.BlockSpec((1, 1, D), lambda i: (b_of(i, t) * 6 + 1, 0, 0)),
            pl.BlockSpec((1, 1, D), lambda i: (b_of(i, t) * 6 + 0, 0, 0)),
            full((D, EVEN_COLS)), full((1, 256)), full((1, 128)), full((256, MLA_W)), full((256, MLA_W)),
            full((128, MLA_W)), full((128, 512)), vec, vec, vec, vec, full((MLA_W, LANES)), full((LANES, MLA_W)),
            row(LANES), row(LANES),
        ],
        out_specs=(row(1024), row(512), row(512), row(LANES), row(MLA_W), row(MLA_W), row(512)),
        compiler_params=_params(("parallel",)),
        name="in_even",
    )(x, norm_g, mod, mod, w, qng, kvng, wqb, wqs, wkn, wv, qg, qgs, kg, kgs, ind, spread, cm, sm)


def _conv_kernel(seq_tiles, prev_ref, x_ref, next_ref, w_ref, b_ref, s_ref, o_ref, buf):
    i = pl.program_id(0)
    t = x_ref.shape[0]
    first = (i % seq_tiles) == 0
    last = (i % seq_tiles) == seq_tiles - 1
    buf[0:8, :] = jnp.where(first, 0.0, prev_ref[...])
    buf[8:8 + t, :] = x_ref[...]
    buf[8 + t:16 + t, :] = jnp.where(last, 0.0, next_ref[...])
    pad = ML_CONV // 2
    acc = b_ref[...] + w_ref[0:1, :] * buf[8 - pad:8 - pad + t, :]
    for k in range(1, ML_CONV):
        acc = acc + w_ref[k:k + 1, :] * buf[8 - pad + k:8 - pad + k + t, :]
    o_ref[...] = (acc * jax.nn.sigmoid(acc) * s_ref[...]).astype(BF16)


def _conv(qk_pre, conv_w, conv_b, seq):
    n, c = qk_pre.shape
    t = ROW_TILE
    nb8 = n // 8
    scale = jnp.concatenate([jnp.ones((512,), F32), jnp.full((512,), ML_DK ** -0.5, F32)]).reshape(1, c)
    return pl.pallas_call(
        functools.partial(_conv_kernel, seq // t),
        out_shape=jax.ShapeDtypeStruct((n, c), BF16),
        grid=(n // t,),
        in_specs=[
            pl.BlockSpec((8, c), lambda i: (jnp.maximum(i * (t // 8) - 1, 0), 0)),
            pl.BlockSpec((t, c), lambda i: (i, 0)),
            pl.BlockSpec((8, c), lambda i: (jnp.minimum((i + 1) * (t // 8), nb8 - 1), 0)),
            pl.BlockSpec((ML_CONV, c), lambda i: (0, 0)),
            pl.BlockSpec((1, c), lambda i: (0, 0)),
            pl.BlockSpec((1, c), lambda i: (0, 0)),
        ],
        out_specs=pl.BlockSpec((t, c), lambda i: (i, 0)),
        scratch_shapes=[pltpu.VMEM((t + 16, c), F32)],
        compiler_params=_params(("parallel",)),
        name="conv",
    )(qk_pre, qk_pre, qk_pre, conv_w, conv_b.reshape(1, c), scale)


def _mlstm_kernel(qf_ref, kf_ref, vf_ref, qb_ref, kb_ref, vb_ref, gcf_ref, gcb_ref, grf_ref, grb_ref,
                  bc_ref, br_ref, of_ref, ob_ref, c_ref, m_ref):
    c = pl.program_id(1)
    L = ML_CHUNK

    @pl.when(c == 0)
    def _():
        c_ref[...] = jnp.zeros_like(c_ref)
        m_ref[...] = jnp.zeros_like(m_ref)

    row = lax.broadcasted_iota(jnp.int32, (L, L), 0)
    col = lax.broadcasted_iota(jnp.int32, (L, L), 1)
    lane = lax.broadcasted_iota(jnp.int32, (L, ML_DV), 1)
    ones_col = jnp.where(lane == 0, 1.0, 0.0).astype(BF16)
    dirs = ((qf_ref, kf_ref, vf_ref, gcf_ref, grf_ref, of_ref, row >= col, row <= col),
            (qb_ref, kb_ref, vb_ref, gcb_ref, grb_ref, ob_ref, row <= col, row >= col))
    for d, (q_ref, k_ref, v_ref, gc_ref, gr_ref, o_ref, causal, causal_t) in enumerate(dirs):
        gc = gc_ref[0] + bc_ref[d]
        gr = gr_ref[0] + br_ref[d]
        i_col = gc[:, 0:ML_HEADS]
        f_col = _log_sigmoid(gc[:, ML_HEADS:2 * ML_HEADS])
        i_row = gr[0:ML_HEADS, :]
        f_row = _log_sigmoid(gr[ML_HEADS:2 * ML_HEADS, :])
        bcum_col = jnp.dot(jnp.where(causal, 1.0, 0.0), f_col, precision=HIGHEST, preferred_element_type=F32)
        bcum_row = jnp.dot(f_row, jnp.where(causal_t, 1.0, 0.0), precision=HIGHEST, preferred_element_type=F32)
        for h in range(ML_HEADS):
            sl = slice(h * ML_DK, (h + 1) * ML_DK)
            st = d * ML_HEADS + h
            q = q_ref[:, sl]
            k = k_ref[:, sl]
            vext = jnp.concatenate([v_ref[:, sl], ones_col], axis=1)
            bc = bcum_col[:, h:h + 1]
            br = bcum_row[h:h + 1, :]
            ir = i_row[h:h + 1, :]
            ic = i_col[:, h:h + 1]
            m_prev = m_ref[st:st + 1, 0:1]
            log_d = jnp.where(causal, bc - br + ir, -jnp.inf)
            m_inter = bc + m_prev
            m_t = jnp.maximum(m_inter, jnp.max(log_d, axis=-1, keepdims=True))
            w = (_dot_nt(q, k) * jnp.exp(log_d - m_t)).astype(BF16)
            a_inter = jnp.exp(m_inter - m_t)
            tot = _dot(w, vext) + a_inter * _dot(q, c_ref[st].astype(BF16))
            den = tot[:, ML_DV:ML_DV + 1]
            o_ref[:, sl] = tot[:, 0:ML_DV] / jnp.maximum(jnp.abs(den), jnp.exp(-m_t))
            total = jnp.sum(f_row[h:h + 1, :], axis=-1, keepdims=True)
            g_row = total - br + ir
            m_new = jnp.maximum(total + m_prev, jnp.max(g_row, axis=-1, keepdims=True))
            a_state = jnp.exp(total + m_prev - m_new)
            kw = (k.astype(F32) * jnp.exp(total - bc + ic - m_new)).astype(BF16)
            c_ref[st] = a_state * c_ref[st] + _dot_tn(kw, vext)
            m_ref[st:st + 1, :] = jnp.broadcast_to(m_new, (1, LANES))


def _mlstm(qk, v, gcol, grow, bcol, brow, batch, seq):
    n = qk.shape[0]
    L = ML_CHUNK
    nc = seq // L
    fw = lambda b, c: b * nc + c
    bw = lambda b, c: b * nc + nc - 1 - c
    out = jax.ShapeDtypeStruct((n, ML_HEADS * ML_DV), F32)
    return pl.pallas_call(
        _mlstm_kernel,
        out_shape=(out, out),
        grid=(batch, nc),
        in_specs=[
            pl.BlockSpec((L, 512), lambda b, c: (fw(b, c), 0)),
            pl.BlockSpec((L, 512), lambda b, c: (fw(b, c), 1)),
            pl.BlockSpec((L, 512), lambda b, c: (fw(b, c), 0)),
            pl.BlockSpec((L, 512), lambda b, c: (bw(b, c), 0)),
            pl.BlockSpec((L, 512), lambda b, c: (bw(b, c), 1)),
            pl.BlockSpec((L, 512), lambda b, c: (bw(b, c), 0)),
            pl.BlockSpec((1, L, 8), lambda b, c: (0, fw(b, c), 0)),
            pl.BlockSpec((1, L, 8), lambda b, c: (1, bw(b, c), 0)),
            pl.BlockSpec((1, 8, L), lambda b, c: (0, 0, fw(b, c))),
            pl.BlockSpec((1, 8, L), lambda b, c: (1, 0, bw(b, c))),
            pl.BlockSpec((2, 1, 8), lambda b, c: (0, 0, 0)),
            pl.BlockSpec((2, 8, 1), lambda b, c: (0, 0, 0)),
        ],
        out_specs=(pl.BlockSpec((L, 512), lambda b, c: (fw(b, c), 0)),
                   pl.BlockSpec((L, 512), lambda b, c: (bw(b, c), 0))),
        scratch_shapes=[pltpu.VMEM((2 * ML_HEADS, ML_DK, 2 * ML_DV), F32), pltpu.VMEM((8, LANES), F32)],
        compiler_params=_params(("parallel", "arbitrary")),
        name="mlstm",
    )(qk, qk, v, qk, qk, v, gcol, gcol, grow, grow, bcol, brow)


def _attn_kernel(q_ref, k_ref, v_ref, o_ref, vext):
    @pl.when(pl.program_id(2) == 0)
    def _():
        lane = lax.broadcasted_iota(jnp.int32, (v_ref.shape[0], MLA_PAD - MLA_V), 1)
        ones_col = jnp.where(lane == 0, 1.0, 0.0).astype(BF16)
        for j in range(2):
            vext[:, j * MLA_PAD:(j + 1) * MLA_PAD] = jnp.concatenate(
                [v_ref[:, j * MLA_V:(j + 1) * MLA_V], ones_col], axis=1)

    outs = []
    for j in range(2):
        q = q_ref[:, j * MLA_PAD:(j + 1) * MLA_PAD]
        k = k_ref[:, j * MLA_PAD:(j + 1) * MLA_PAD]
        s = _dot_nt(q, k)
        p = jnp.exp2(s - jnp.max(s, axis=-1, keepdims=True))
        oe = _dot(p.astype(BF16), vext[:, j * MLA_PAD:(j + 1) * MLA_PAD])
        outs.append(oe[:, 0:MLA_V] / oe[:, MLA_V:MLA_V + 1])
    o_ref[...] = jnp.concatenate(outs, axis=1).astype(BF16)


def _attn(q, k, v, batch, seq):
    n = q.shape[0]
    t = min(ATTN_Q_TILE, seq)
    nq = seq // t
    return pl.pallas_call(
        _attn_kernel,
        out_shape=jax.ShapeDtypeStruct((n, MLA_HEADS * MLA_V), BF16),
        grid=(batch, MLA_HEADS // 2, nq),
        in_specs=[
            pl.BlockSpec((t, 2 * MLA_PAD), lambda b, h, i: (b * nq + i, h)),
            pl.BlockSpec((seq, 2 * MLA_PAD), lambda b, h, i: (b, h)),
            pl.BlockSpec((seq, 2 * MLA_V), lambda b, h, i: (b, h)),
        ],
        out_specs=pl.BlockSpec((t, 2 * MLA_V), lambda b, h, i: (b * nq + i, h)),
        scratch_shapes=[pltpu.VMEM((seq, 2 * MLA_PAD), BF16)],
        compiler_params=_params(("parallel", "parallel", "arbitrary")),
        name="attn",
    )(q, k, v)


def _mixer_tail(y, x_ref, g1_ref, fg_ref, sc2_ref, sh2_ref, rwh_ref, rwl_ref, rb_ref, x_out, hf_out, lg_out):
    x1 = x_ref[...] + g1_ref[0] * y
    x_out[...] = x1
    hf = _norm_mod(x1, fg_ref[...], sc2_ref[0], sh2_ref[0])
    hf_out[...] = hf
    hi, lo = _split_bf16(hf)
    lg_out[...] = _dot(hi, rwh_ref[...]) + (_dot(lo, rwh_ref[...]) + _dot(hi, rwl_ref[...])) + rb_ref[...]


def _tail_specs(n, t, b_of):
    full = lambda shape: pl.BlockSpec(shape, lambda i: (0,) * len(shape))
    mod = lambda j: pl.BlockSpec((1, 1, D), lambda i: (b_of(i, t) * 6 + j, 0, 0))
    row = lambda c: pl.BlockSpec((t, c), lambda i: (i, 0))
    in_specs = [row(D), mod(2), full((1, D)), mod(4), mod(3), full((D, LANES)), full((D, LANES)), full((1, LANES))]
    out_shape = (jax.ShapeDtypeStruct((n, D), F32), jax.ShapeDtypeStruct((n, D), F32),
                 jax.ShapeDtypeStruct((n, LANES), F32))
    out_specs = (row(D), row(D), row(LANES))
    return in_specs, out_shape, out_specs


def _out_even_kernel(hf_ref, hb_ref, om_ref, oa_ref, mg_ref, wt_ref, wb_ref, *tail):
    hs = hf_ref[...] + hb_ref[...]
    gate = jax.nn.sigmoid(om_ref[...])
    parts = []
    for h in range(ML_HEADS):
        sl = slice(h * ML_DV, (h + 1) * ML_DV)
        parts.append(_rms(hs[:, sl]) * mg_ref[:, sl] * gate[:, sl])
    hml = jnp.concatenate(parts, axis=1).astype(BF16)
    y = _dot(hml, wt_ref[...]) + _dot(oa_ref[...], wb_ref[...])
    _mixer_tail(y, *tail)


def _out_even(h_f, h_b, o_m, o_mla, ml_g, w_top, w_bot, x, mod, b_of, ffn_g, rw, rb):
    n = x.shape[0]
    t = ROW_TILE
    full = lambda shape: pl.BlockSpec(shape, lambda i: (0,) * len(shape))
    row = lambda c: pl.BlockSpec((t, c), lambda i: (i, 0))
    tin, out_shape, out_specs = _tail_specs(n, t, b_of)
    return pl.pallas_call(
        _out_even_kernel,
        out_shape=out_shape,
        grid=(n // t,),
        in_specs=[row(512), row(512), row(512), row(512), full((1, 512)), full((512, D)), full((512, D))] + tin,
        out_specs=out_specs,
        compiler_params=_params(("parallel",)),
        name="out_even",
    )(h_f, h_b, o_m, o_mla, ml_g, w_top, w_bot, x, mod, ffn_g, mod, mod, *rw, rb)


def _in_odd_kernel(x_ref, g_ref, sc_ref, sh_ref, w_ref, cr_ref, sr_ref, q_out, k_out, v_out, g_out):
    h = _norm_mod(x_ref[...], g_ref[...], sc_ref[0], sh_ref[0]).astype(BF16)
    cos, sin = cr_ref[...], sr_ref[...]
    hw = RET_DK // 2
    for idx, (dst, scale) in enumerate(((q_out, RET_DK ** -0.5), (k_out, 1.0))):
        z = _dot(h, w_ref[:, idx * 1024:(idx + 1) * 1024])
        for hd in range(RET_HEADS):
            x1 = z[:, hd * RET_DK:hd * RET_DK + hw]
            x2 = z[:, hd * RET_DK + hw:(hd + 1) * RET_DK]
            dst[:, hd * RET_DK:hd * RET_DK + hw] = ((x1 * cos - x2 * sin) * scale).astype(BF16)
            dst[:, hd * RET_DK + hw:(hd + 1) * RET_DK] = ((x2 * cos + x1 * sin) * scale).astype(BF16)
    v_out[...] = _dot(h, w_ref[:, 2048:4096]).astype(BF16)
    g_out[...] = _dot(h, w_ref[:, 4096:6144])


def _in_odd(x, mod, b_of, norm_g, w, cr, sr):
    n = x.shape[0]
    t = ROW_TILE
    full = lambda shape: pl.BlockSpec(shape, lambda i: (0,) * len(shape))
    row = lambda c: pl.BlockSpec((t, c), lambda i: (i, 0))
    out_shape = (
        jax.ShapeDtypeStruct((n, 1024), BF16), jax.ShapeDtypeStruct((n, 1024), BF16),
        jax.ShapeDtypeStruct((n, 2048), BF16), jax.ShapeDtypeStruct((n, 2048), F32),
    )
    return pl.pallas_call(
        _in_odd_kernel,
        out_shape=out_shape,
        grid=(n // t,),
        in_specs=[
            row(D), full((1, D)),
            pl.BlockSpec((1, 1, D), lambda i: (b_of(i, t) * 6 + 1, 0, 0)),
            pl.BlockSpec((1, 1, D), lambda i: (b_of(i, t) * 6 + 0, 0, 0)),
            full((D, 6144)), row(LANES), row(LANES),
        ],
        out_specs=(row(1024), row(1024), row(2048), row(2048)),
        compiler_params=_params(("parallel",)),
        name="in_odd",
    )(x, norm_g, mod, mod, w, cr, sr)


def _ret_kernel(qf_ref, kf_ref, vf_ref, qb_ref, kb_ref, vb_ref, dec_ref, of_ref, ob_ref, s_ref):
    c = pl.program_id(1)
    L = RET_CHUNK

    @pl.when(c == 0)
    def _():
        s_ref[...] = jnp.zeros_like(s_ref)

    row = lax.broadcasted_iota(jnp.int32, (L, L), 0)
    col = lax.broadcasted_iota(jnp.int32, (L, L), 1)
    pos = lax.broadcasted_iota(jnp.int32, (L, 1), 0)
    dirs = ((qf_ref, kf_ref, vf_ref, of_ref, row - col, pos + 1, L - 1 - pos),
            (qb_ref, kb_ref, vb_ref, ob_ref, col - row, L - pos, pos))
    for d, (q_ref, k_ref, v_ref, o_ref, rel_i, exp_q_i, exp_k_i) in enumerate(dirs):
        rel = rel_i.astype(F32)
        exp_q = exp_q_i.astype(F32)
        exp_k = exp_k_i.astype(F32)
        for h in range(RET_HEADS):
            st = d * RET_HEADS + h
            lg = _log_sigmoid(dec_ref[d, h:h + 1, :])
            lg1 = lg[:, 0:1]
            decay_in = jnp.where(rel >= 0, jnp.exp(jnp.maximum(rel, 0.0) * lg1), 0.0)
            q = q_ref[:, h * RET_DK:(h + 1) * RET_DK]
            k = k_ref[:, h * RET_DK:(h + 1) * RET_DK]
            v = v_ref[:, h * RET_DV:(h + 1) * RET_DV]
            sc = (_dot_nt(q, k) * decay_in).astype(BF16)
            o_ref[:, h * RET_DV:(h + 1) * RET_DV] = (
                _dot(sc, v) + jnp.exp(exp_q * lg1) * _dot(q, s_ref[st].astype(BF16)))
            kd = (k.astype(F32) * jnp.exp(exp_k * lg1)).astype(BF16)
            s_ref[st] = jnp.exp(L * lg1) * s_ref[st] + _dot_tn(kd, v)


def _retention(q, k, v, dec, batch, seq):
    n = q.shape[0]
    L = RET_CHUNK
    nc = seq // L
    fw = lambda b, c: (b * nc + c, 0)
    bw = lambda b, c: (b * nc + nc - 1 - c, 0)
    out = jax.ShapeDtypeStruct((n, RET_HEADS * RET_DV), F32)
    return pl.pallas_call(
        _ret_kernel,
        out_shape=(out, out),
        grid=(batch, nc),
        in_specs=[
            pl.BlockSpec((L, 1024), fw), pl.BlockSpec((L, 1024), fw), pl.BlockSpec((L, 2048), fw),
            pl.BlockSpec((L, 1024), bw), pl.BlockSpec((L, 1024), bw), pl.BlockSpec((L, 2048), bw),
            pl.BlockSpec((2, RET_HEADS, LANES), lambda b, c: (0, 0, 0)),
        ],
        out_specs=(pl.BlockSpec((L, 2048), fw), pl.BlockSpec((L, 2048), bw)),
        scratch_shapes=[pltpu.VMEM((2 * RET_HEADS, RET_DK, RET_DV), F32)],
        compiler_params=_params(("parallel", "arbitrary")),
        name="retention",
    )(q, k, v, q, k, v, dec)


def _out_odd_kernel(yf_ref, yb_ref, g_ref, gn_ref, w_ref, *tail):
    ys = yf_ref[...] + yb_ref[...]
    g = g_ref[...]
    gate = g * jax.nn.sigmoid(g)
    parts = []
    for h in range(RET_HEADS):
        sl = slice(h * RET_DV, (h + 1) * RET_DV)
        yh = ys[:, sl]
        yc = yh - jnp.mean(yh, axis=-1, keepdims=True)
        var = jnp.mean(yc * yc, axis=-1, keepdims=True)
        parts.append(yc * lax.rsqrt(var + EPS) * gn_ref[:, sl] * gate[:, sl])
    yn = jnp.concatenate(parts, axis=1).astype(BF16)
    _mixer_tail(_dot(yn, w_ref[...]), *tail)


def _out_odd(y_f, y_b, g, gn_g, w_out, x, mod, b_of, ffn_g, rw, rb):
    n = x.shape[0]
    t = ROW_TILE
    full = lambda shape: pl.BlockSpec(shape, lambda i: (0,) * len(shape))
    row = lambda c: pl.BlockSpec((t, c), lambda i: (i, 0))
    tin, out_shape, out_specs = _tail_specs(n, t, b_of)
    return pl.pallas_call(
        _out_odd_kernel,
        out_shape=out_shape,
        grid=(n // t,),
        in_specs=[row(2048), row(2048), row(2048), full((1, 2048)), full((2048, D))] + tin,
        out_specs=out_specs,
        compiler_params=_params(("parallel",)),
        name="out_odd",
    )(y_f, y_b, g, gn_g, w_out, x, mod, ffn_g, mod, mod, *rw, rb)


def _route_kernel(lg_ref, e_out, r_out, w_out, cnt_out, base_ref):
    i = pl.program_id(0)
    t = lg_ref.shape[0]

    @pl.when(i == 0)
    def _():
        base_ref[...] = jnp.zeros_like(base_ref)

    l = lg_ref[...]
    lane = lax.broadcasted_iota(jnp.int32, l.shape, 1)
    sel = jnp.zeros(l.shape, F32)
    vals, idxs, hots = [], [], []
    for _k in range(TOP_K):
        m = jnp.max(l, axis=-1, keepdims=True)
        idx = jnp.min(jnp.where(l == m, lane, LANES), axis=-1, keepdims=True)
        hot = lane == idx
        vals.append(m)
        idxs.append(idx)
        hots.append(hot)
        sel = sel + jnp.where(hot, 1.0, 0.0)
        l = jnp.where(hot, -jnp.inf, l)
    es = [jnp.exp(v - vals[0]) for v in vals]
    den = es[0] + es[1] + es[2] + es[3]
    row = lax.broadcasted_iota(jnp.int32, (t, t), 0)
    col = lax.broadcasted_iota(jnp.int32, (t, t), 1)
    before = jnp.where(row > col, 1.0, 0.0).astype(BF16)
    rank_mat = _dot(before, sel.astype(BF16)) + base_ref[...]
    e_acc = jnp.zeros(l.shape, jnp.int32)
    r_acc = jnp.zeros(l.shape, jnp.int32)
    w_acc = jnp.zeros(l.shape, F32)
    for k in range(TOP_K):
        rk = jnp.sum(jnp.where(hots[k], rank_mat, 0.0), axis=-1, keepdims=True).astype(jnp.int32)
        e_acc = jnp.where(lane == k, idxs[k], e_acc)
        r_acc = jnp.where(lane == k, rk, r_acc)
        w_acc = jnp.where(lane == k, es[k] / den, w_acc)
    e_out[...] = e_acc
    r_out[...] = r_acc
    w_out[...] = w_acc
    base_ref[...] = base_ref[...] + jnp.sum(sel, axis=0, keepdims=True)
    cnt_out[...] = base_ref[...]


def _route(logits):
    n = logits.shape[0]
    t = ROUTE_TILE
    row = pl.BlockSpec((t, LANES), lambda i: (i, 0))
    return pl.pallas_call(
        _route_kernel,
        out_shape=(jax.ShapeDtypeStruct((n, LANES), jnp.int32), jax.ShapeDtypeStruct((n, LANES), jnp.int32),
                   jax.ShapeDtypeStruct((n, LANES), F32), jax.ShapeDtypeStruct((1, LANES), F32)),
        grid=(n // t,),
        in_specs=[row],
        out_specs=(row, row, row, pl.BlockSpec((1, LANES), lambda i: (0, 0))),
        scratch_shapes=[pltpu.VMEM((1, LANES), F32)],
        compiler_params=_params(("arbitrary",)),
        name="route",
    )(logits)


def _dispatch_kernel(dest_ref, pad_ref, hf_ref, xs_ref, sem):
    t = hf_ref.shape[0]

    def row_copy(src_row, dst_row):
        return pltpu.make_async_copy(hf_ref.at[pl.ds(src_row, 1)], xs_ref.at[pl.ds(dst_row, 1)], sem)

    def wait_rows():
        pltpu.make_async_copy(hf_ref, xs_ref.at[pl.ds(0, t)], sem).wait()

    def issue(g, carry):
        for u in range(MOVE_UNROLL):
            tok = g * MOVE_UNROLL + u
            for k in range(TOP_K):
                row_copy(tok, dest_ref[0, 0, tok * TOP_K + k]).start(priority=k % 2)
        return carry

    lax.fori_loop(0, t // MOVE_UNROLL, issue, 0)
    for _ in range(TOP_K):
        wait_rows()

    @pl.when(pl.program_id(0) == 0)
    def _():
        per_iter = MOVE_UNROLL * TOP_K
        n_free = pad_ref.shape[-1]

        def pad_issue(g, carry):
            for u in range(per_iter):
                row_copy(0, pad_ref[0, 0, g * per_iter + u]).start(priority=u % 2)
            return carry

        lax.fori_loop(0, n_free // per_iter, pad_issue, 0)
        for _ in range(n_free // t):
            wait_rows()


def _dispatch(hf, dest, pad_dest, rows):
    n = hf.shape[0]
    t = DISPATCH_TILE
    n_free = pad_dest.shape[0]
    assert n_free % t == 0 and t % MOVE_UNROLL == 0
    return pl.pallas_call(
        _dispatch_kernel,
        out_shape=jax.ShapeDtypeStruct((rows, D), F32),
        grid=(n // t,),
        in_specs=[
            pl.BlockSpec((1, 1, t * TOP_K), lambda i: (i, 0, 0), memory_space=pltpu.SMEM),
            pl.BlockSpec((1, 1, n_free), lambda i: (0, 0, 0), memory_space=pltpu.SMEM),
            pl.BlockSpec((t, D), lambda i: (i, 0)),
        ],
        out_specs=pl.BlockSpec(memory_space=pl.ANY),
        scratch_shapes=[pltpu.SemaphoreType.DMA(())],
        compiler_params=_params(("arbitrary",), has_side_effects=True),
        name="dispatch",
    )(dest.reshape(n // t, 1, t * TOP_K), pad_dest.reshape(1, 1, n_free), hf)


def _expert_kernel(be_ref, nu_ref, xs_ref, wgu_ref, bgu_ref, wd_ref, bd_ref, ys_ref, wgu_bf, wd_bf):
    i = pl.program_id(0)
    fresh = jnp.logical_or(i == 0, be_ref[i] != be_ref[jnp.maximum(i - 1, 0)])

    @pl.when(fresh)
    def _():
        wgu_bf[...] = wgu_ref[0, 0].astype(BF16)
        wd_bf[...] = wd_ref[0, 0].astype(BF16)

    @pl.when(i < nu_ref[0])
    def _():
        gu = _dot(xs_ref[...].astype(BF16), wgu_bf[...]) + bgu_ref[0, 0]
        gate = jnp.minimum(gu[:, :EXPERT_FF], SWIGLU_LIMIT)
        up = jnp.clip(gu[:, EXPERT_FF:], -SWIGLU_LIMIT, SWIGLU_LIMIT)
        act = (up + 1.0) * gate * jax.nn.sigmoid(SWIGLU_ALPHA * gate)
        ys_ref[...] = _dot(act.astype(BF16), wd_bf[...]) + bd_ref[0, 0]

    @pl.when(i >= nu_ref[0])
    def _():
        ys_ref[...] = jnp.zeros_like(ys_ref)


def _experts(xs, nb, layer, blk_expert, n_used, w_gu, b_gu, w_down, b_down):
    grid_spec = pltpu.PrefetchScalarGridSpec(
        num_scalar_prefetch=2,
        grid=(nb,),
        in_specs=[
            pl.BlockSpec((MOE_BLOCK, D), lambda i, be, nu: (jnp.minimum(i, nu[0] - 1), 0)),
            pl.BlockSpec((1, 1, D, 2 * EXPERT_FF), lambda i, be, nu: (layer, be[i], 0, 0)),
            pl.BlockSpec((1, 1, 1, 2 * EXPERT_FF), lambda i, be, nu: (layer, be[i], 0, 0)),
            pl.BlockSpec((1, 1, EXPERT_FF, D), lambda i, be, nu: (layer, be[i], 0, 0)),
            pl.BlockSpec((1, 1, 1, D), lambda i, be, nu: (layer, be[i], 0, 0)),
        ],
        out_specs=pl.BlockSpec((MOE_BLOCK, D), lambda i, be, nu: (i, 0)),
        scratch_shapes=[pltpu.VMEM((D, 2 * EXPERT_FF), BF16), pltpu.VMEM((EXPERT_FF, D), BF16)],
    )
    return pl.pallas_call(
        _expert_kernel,
        out_shape=jax.ShapeDtypeStruct((nb * MOE_BLOCK, D), F32),
        grid_spec=grid_spec,
        compiler_params=_params(("arbitrary",)),
        name="experts",
    )(blk_expert, n_used, xs, w_gu, b_gu.reshape(DEPTH, N_EXPERTS, 1, 2 * EXPERT_FF), w_down,
      b_down.reshape(DEPTH, N_EXPERTS, 1, D))


def _combine_kernel(dest0_ref, dest_ref, ys_ref, wt_ref, x_ref, g2_ref, o_ref, buf_a, buf_b, sem_a, sem_b):
    i = pl.program_id(0)
    t = x_ref.shape[0]

    def start_rows(idx_ref, tok, buf, sem):
        for k in range(TOP_K):
            pltpu.make_async_copy(ys_ref.at[pl.ds(idx_ref[0, 0, tok * TOP_K + k], 1)],
                                  buf.at[tok, pl.ds(k, 1)], sem).start(priority=k % 2)

    def wait_tile(buf, sem):
        for k in range(TOP_K):
            pltpu.make_async_copy(ys_ref.at[pl.ds(0, t)], buf.at[pl.ds(0, t), k], sem).wait()

    @pl.when(i == 0)
    def _():
        def issue(g, carry):
            for u in range(MOVE_UNROLL):
                start_rows(dest0_ref, g * MOVE_UNROLL + u, buf_a, sem_a)
            return carry
        lax.fori_loop(0, t // MOVE_UNROLL, issue, 0)
        wait_tile(buf_a, sem_a)

    g2 = g2_ref[0]

    def step(cur, nxt, sem_nxt):
        def body(g, carry):
            for u in range(MOVE_UNROLL):
                start_rows(dest_ref, g * MOVE_UNROLL + u, nxt, sem_nxt)
            rows = pl.ds(pl.multiple_of(g * MOVE_UNROLL, MOVE_UNROLL), MOVE_UNROLL)
            wt = wt_ref[rows, :]
            acc = cur[rows, 0, :] * wt[:, 0:1]
            for k in range(1, TOP_K):
                acc = acc + cur[rows, k, :] * wt[:, k:k + 1]
            o_ref[rows, :] = x_ref[rows, :] + g2 * acc
            return carry
        lax.fori_loop(0, t // MOVE_UNROLL, body, 0)
        wait_tile(nxt, sem_nxt)

    @pl.when(i % 2 == 0)
    def _():
        step(buf_a, buf_b, sem_b)

    @pl.when(i % 2 == 1)
    def _():
        step(buf_b, buf_a, sem_a)


def _combine(ys, dest, wts, x, mod, b_of):
    n = x.shape[0]
    t = MOVE_TILE
    nt = n // t
    dest3 = dest.reshape(nt, 1, t * TOP_K)
    idx = lambda f: pl.BlockSpec((1, 1, t * TOP_K), lambda i: (f(i), 0, 0), memory_space=pltpu.SMEM)
    return pl.pallas_call(
        _combine_kernel,
        out_shape=jax.ShapeDtypeStruct((n, D), F32),
        grid=(nt,),
        in_specs=[
            idx(lambda i: 0),
            idx(lambda i: jnp.minimum(i + 1, nt - 1)),
            pl.BlockSpec(memory_space=pl.ANY),
            pl.BlockSpec((t, LANES), lambda i: (i, 0)),
            pl.BlockSpec((t, D), lambda i: (i, 0)),
            pl.BlockSpec((1, 1, D), lambda i: (b_of(i, t) * 6 + 5, 0, 0)),
        ],
        out_specs=pl.BlockSpec((t, D), lambda i: (i, 0)),
        scratch_shapes=[pltpu.VMEM((t, TOP_K, D), F32), pltpu.VMEM((t, TOP_K, D), F32),
                        pltpu.SemaphoreType.DMA(()), pltpu.SemaphoreType.DMA(())],
        compiler_params=_params(("arbitrary",)),
        name="combine",
    )(dest3, dest3, ys, wts, x, mod)


def _moe(x1, hf, logits, mod, b_of, layer, w_gu, b_gu, w_down, b_down):
    n = x1.shape[0]
    eidx, rank, wts, counts = _route(logits)
    experts = jnp.arange(N_EXPERTS, dtype=jnp.int32)
    cnt = counts[0, :N_EXPERTS].astype(jnp.int32)
    padded = (cnt + MOE_BLOCK - 1) // MOE_BLOCK * MOE_BLOCK
    pend = jnp.cumsum(padded)
    pstart = pend - padded
    e4 = eidx[:, :TOP_K]
    dest = rank[:, :TOP_K] + jnp.sum(jnp.where(e4[:, :, None] == experts, pstart, 0), axis=-1)
    nb = -(-(n * TOP_K) // MOE_BLOCK) + N_EXPERTS
    blk_start = jnp.arange(nb, dtype=jnp.int32) * MOE_BLOCK
    blk_expert = jnp.minimum(jnp.sum((pend[None, :] <= blk_start[:, None]).astype(jnp.int32), axis=1),
                             N_EXPERTS - 1)
    n_used = (pend[-1:] // MOE_BLOCK).astype(jnp.int32)
    n_free = nb * MOE_BLOCK - n * TOP_K
    free_end = jnp.cumsum(padded - cnt)
    seg_first = jnp.concatenate([pstart + cnt, pend[-1:]])
    seg_skip = jnp.concatenate([free_end - (padded - cnt), free_end[-1:]])
    jj = jnp.arange(n_free, dtype=jnp.int32)
    seg = jnp.sum((free_end[None, :] <= jj[:, None]).astype(jnp.int32), axis=1)
    hot = seg[:, None] == jnp.arange(N_EXPERTS + 1, dtype=jnp.int32)
    pad_dest = jj + jnp.sum(jnp.where(hot, seg_first - seg_skip, 0), axis=1)
    xs = _dispatch(hf, dest, pad_dest, nb * MOE_BLOCK)
    ys = _experts(xs, nb, layer, blk_expert, n_used, w_gu, b_gu, w_down, b_down)
    return _combine(ys, dest, wts, x1, mod, b_of)


def _head_tile(t, partner):
    half = MLA_ROPE // 2
    lead = t.shape[:-1]
    tail = jnp.zeros(lead + (MLA_PAD - MLA_QK,), t.dtype)
    if not partner:
        return jnp.concatenate([t, tail], axis=-1)
    return jnp.concatenate([jnp.zeros(lead + (MLA_NOPE,), t.dtype), t[..., MLA_NOPE + half:],
                            t[..., MLA_NOPE:MLA_NOPE + half], tail], axis=-1)


def _prep_even(w_in, gate_b, w_qb, w_kvb, qk_q_g, qk_k_g, w_out):
    q_m, k_m, v_m, o_m, gates, q_a, kv_a, k_r = jnp.split(
        w_in, [512, 1024, 1536, 2048, 2064, 2320, 2448], axis=1)
    zeros = lambda c: jnp.zeros((D, c), F32)
    k_r96 = jnp.concatenate([zeros(MLA_NOPE), k_r], axis=1)
    misc = jnp.concatenate([gates, zeros(LANES - 16)], axis=1)
    w = jnp.concatenate([q_m, k_m, v_m, o_m, q_a, kv_a, _head_tile(k_r96, False), _head_tile(k_r96, True), misc],
                        axis=1).astype(BF16)
    qb = w_qb.reshape(MLA_Q_LORA, MLA_HEADS, MLA_QK)
    wqb = _head_tile(qb, False).reshape(MLA_Q_LORA, MLA_W).astype(BF16)
    wqs = _head_tile(qb, True).reshape(MLA_Q_LORA, MLA_W).astype(BF16)
    kvb = w_kvb.reshape(MLA_KV_LORA, MLA_HEADS, MLA_NOPE + MLA_V)
    wkn = jnp.pad(kvb[:, :, :MLA_NOPE], ((0, 0), (0, 0), (0, MLA_PAD - MLA_NOPE)))
    wkn = wkn.reshape(MLA_KV_LORA, MLA_W).astype(BF16)
    wv = kvb[:, :, MLA_NOPE:].reshape(MLA_KV_LORA, MLA_HEADS * MLA_V).astype(BF16)
    qg96 = qk_q_g * (MLA_QK ** -0.5 * LOG2_E)
    qg = _head_tile(qg96, False).reshape(1, MLA_PAD)
    qgs = _head_tile(qg96, True).reshape(1, MLA_PAD)
    kg = _head_tile(qk_k_g, False).reshape(1, MLA_PAD)
    kgs = _head_tile(qk_k_g, True).reshape(1, MLA_PAD)
    gb = gate_b.reshape(2, 8)
    w_top = w_out[:512].astype(BF16)
    w_bot = w_out[512:].astype(BF16)
    return w, wqb, wqs, wkn, wv, qg, qgs, kg, kgs, gb.reshape(2, 1, 8), gb.reshape(2, 8, 1), w_top, w_bot


def kernel(x, c, positions, ada_w, ada_b, norm_mix_g, norm_ffn_g, hy_w_in, ml_conv_w, ml_conv_b, ml_gate_b, ml_norm_g, mla_q_norm_g, mla_kv_norm_g, mla_w_qb, mla_w_kvb, mla_qk_q_g, mla_qk_k_g, hy_w_out, ret_w_in, ret_decay_f, ret_decay_b, ret_gn_g, ret_w_out, moe_router_w, moe_router_b, moe_w_gu, moe_b_gu, moe_w_down, moe_b_down):
    batch, seq, _ = x.shape
    n = batch * seq

    def b_of(i, t):
        return (i * t) // seq

    mod_all = _ada(c, ada_w, ada_b).reshape(DEPTH, batch * 6, 1, D)
    cr, sr, cm, sm = _rope_tables(positions)
    head_of_lane = jnp.arange(MLA_W, dtype=jnp.int32) // MLA_PAD
    ind = (head_of_lane[:, None] == jnp.arange(LANES, dtype=jnp.int32)[None, :]).astype(BF16)
    spread = ind.T
    xf = x.reshape(n, D)
    for layer in range(DEPTH):
        mod = mod_all[layer]
        j = layer // 2
        mix_g = norm_mix_g[layer].reshape(1, D)
        ffn_g = norm_ffn_g[layer].reshape(1, D)
        rw = _split_bf16(jnp.pad(moe_router_w[layer], ((0, 0), (0, LANES - N_EXPERTS))))
        rb = jnp.concatenate([moe_router_b[layer], jnp.full((LANES - N_EXPERTS,), NEG_BIG, F32)]).reshape(1, LANES)
        if layer % 2 == 0:
            w, wqb, wqs, wkn, wv, qg, qgs, kg, kgs, gbc, gbr, w_top, w_bot = _prep_even(
                hy_w_in[j], ml_gate_b[j], mla_w_qb[j], mla_w_kvb[j], mla_qk_q_g[j], mla_qk_k_g[j], hy_w_out[j])
            qk_pre, v_m, o_m, misc, q_a, k_a, v_a = _in_even(
                xf, mod, b_of, mix_g, w, mla_q_norm_g[j].reshape(1, -1), mla_kv_norm_g[j].reshape(1, -1),
                wqb, wqs, wkn, wv, qg, qgs, kg, kgs, ind, spread, cm, sm)
            qk = _conv(qk_pre, ml_conv_w[j], ml_conv_b[j], seq)
            gcol = misc[:, :16].reshape(n, 2, 8).transpose(1, 0, 2)
            grow = gcol.transpose(0, 2, 1)
            h_f, h_b = _mlstm(qk, v_m, gcol, grow, gbc, gbr, batch, seq)
            o_mla = _attn(q_a, k_a, v_a, batch, seq)
            x1, hf, logits = _out_even(h_f, h_b, o_m, o_mla, ml_norm_g[j].reshape(1, -1), w_top, w_bot,
                                       xf, mod, b_of, ffn_g, rw, rb)
        else:
            dec = jnp.broadcast_to(jnp.stack([ret_decay_f[j], ret_decay_b[j]])[:, :, None], (2, RET_HEADS, LANES))
            q_r, k_r, v_r, g_r = _in_odd(xf, mod, b_of, mix_g, ret_w_in[j].astype(BF16), cr, sr)
            y_f, y_b = _retention(q_r, k_r, v_r, dec.astype(F32), batch, seq)
            x1, hf, logits = _out_odd(y_f, y_b, g_r, ret_gn_g[j].reshape(1, -1), ret_w_out[j].astype(BF16),
                                      xf, mod, b_of, ffn_g, rw, rb)
        xf = _moe(x1, hf, logits, mod, b_of, layer, moe_w_gu, moe_b_gu, moe_w_down, moe_b_down)
    return xf.reshape(batch, seq, D)
```

```python
import functools

import jax
import jax.numpy as jnp
from jax import lax
from jax.experimental import pallas as pl
from jax.experimental.pallas import tpu as pltpu

F32 = jnp.float32
BF16 = jnp.bfloat16
HIGHEST = lax.Precision.HIGHEST

D = 1024
DEPTH = 4
EPS = 1e-6
ROPE_BASE = 10000.0

ML_HEADS = 4
ML_DK = 128
ML_DV = 128
ML_CONV = 5
ML_CHUNK = 256

MLA_HEADS = 8
MLA_Q_LORA = 256
MLA_KV_LORA = 128
MLA_NOPE = 64
MLA_ROPE = 32
MLA_V = 64
MLA_QK = MLA_NOPE + MLA_ROPE
MLA_PAD = 128
MLA_W = MLA_HEADS * MLA_PAD

RET_HEADS = 4
RET_DK = 256
RET_DV = 512
RET_CHUNK = 256

N_EXPERTS = 32
TOP_K = 4
EXPERT_FF = 1024
SWIGLU_LIMIT = 7.0
SWIGLU_ALPHA = 1.702
MOE_BLOCK = 512

LANES = 128
NEG_BIG = -1e30
LOG2_E = 1.4426950408889634

ROW_TILE = 512
ATTN_Q_TILE = 256
ROUTE_TILE = 512
DISPATCH_TILE = 2048
MOVE_TILE = 512
MOVE_UNROLL = 8
VMEM_LIMIT = 56 * 1024 * 1024


def _params(semantics, **kw):
    return pltpu.CompilerParams(dimension_semantics=semantics, vmem_limit_bytes=VMEM_LIMIT, **kw)


def _dot(a, b):
    return jnp.dot(a, b, preferred_element_type=F32)


def _dot_nt(a, b):
    return lax.dot_general(a, b, (((1,), (1,)), ((), ())), preferred_element_type=F32)


def _dot_tn(a, b):
    return lax.dot_general(a, b, (((0,), (0,)), ((), ())), preferred_element_type=F32)


def _log_sigmoid(x):
    return jnp.minimum(x, 0.0) - jnp.log1p(jnp.exp(-jnp.abs(x)))


def _rms(x, denom=None):
    n = x.shape[-1] if denom is None else denom
    return x * lax.rsqrt(jnp.sum(x * x, axis=-1, keepdims=True) / n + EPS)


def _norm_mod(x, g, sc, sh):
    return (_rms(x) * g) * (1.0 + sc) + sh


def _ada_kernel(c_ref, w_ref, b_ref, o_ref):
    c = c_ref[...]
    cs = c * jax.nn.sigmoid(c)
    o_ref[0] = jnp.dot(cs, w_ref[0], precision=HIGHEST, preferred_element_type=F32) + b_ref[0]


def _ada(c, ada_w, ada_b):
    b = c.shape[0]
    return pl.pallas_call(
        _ada_kernel,
        out_shape=jax.ShapeDtypeStruct((DEPTH, b, 6 * D), F32),
        grid=(DEPTH, 6),
        in_specs=[
            pl.BlockSpec((b, D), lambda l, j: (0, 0)),
            pl.BlockSpec((1, D, D), lambda l, j: (l, 0, j)),
            pl.BlockSpec((1, 1, D), lambda l, j: (l, 0, j)),
        ],
        out_specs=pl.BlockSpec((1, b, D), lambda l, j: (l, 0, j)),
        compiler_params=_params(("parallel", "parallel")),
        name="ada",
    )(c, ada_w, ada_b.reshape(DEPTH, 1, 6 * D))


def _rope_kernel(pos_ref, invr_ref, invm_ref, cr_ref, sr_ref, cm_ref, sm_ref):
    pos = pos_ref[...].astype(F32)
    ang_r = pos * invr_ref[...]
    cr_ref[...] = jnp.cos(ang_r)
    sr_ref[...] = jnp.sin(ang_r)
    ang_m = pos * invm_ref[...]
    lane = lax.broadcasted_iota(jnp.int32, ang_m.shape, 1)
    s = jnp.sin(ang_m)
    cm_ref[...] = jnp.cos(ang_m)
    sm_ref[...] = jnp.where(lane < MLA_NOPE + MLA_ROPE // 2, -s, s)


def _rope_tables(positions):
    n = positions.size
    inv_r = ROPE_BASE ** (-jnp.arange(0, RET_DK, 2, dtype=F32) / RET_DK)
    inv_m = ROPE_BASE ** (-jnp.arange(0, MLA_ROPE, 2, dtype=F32) / MLA_ROPE)
    half = MLA_ROPE // 2
    invm = jnp.zeros((LANES,), F32).at[MLA_NOPE:MLA_NOPE + half].set(inv_m).at[MLA_NOPE + half:MLA_QK].set(inv_m)
    t = ROW_TILE
    tab = jax.ShapeDtypeStruct((n, LANES), F32)
    row = pl.BlockSpec((t, LANES), lambda i: (i, 0))
    vec = pl.BlockSpec((1, LANES), lambda i: (0, 0))
    return pl.pallas_call(
        _rope_kernel,
        out_shape=(tab,) * 4,
        grid=(n // t,),
        in_specs=[pl.BlockSpec((t, 1), lambda i: (i, 0)), vec, vec],
        out_specs=(row,) * 4,
        compiler_params=_params(("parallel",)),
        name="rope",
    )(positions.reshape(n, 1), inv_r.reshape(1, LANES), invm.reshape(1, LANES))


EVEN_COLS = 4 * 512 + MLA_Q_LORA + 4 * LANES


def _split_bf16(x):
    hi = x.astype(BF16)
    return hi, (x - hi.astype(F32)).astype(BF16)


def _head_rsqrt(x, ind_ref, spread_ref):
    hi, lo = _split_bf16(x * x)
    s = _dot(hi, ind_ref[...]) + _dot(lo, ind_ref[...])
    hi, lo = _split_bf16(lax.rsqrt(s * (1.0 / MLA_QK) + EPS))
    return _dot(hi, spread_ref[...]) + _dot(lo, spread_ref[...])


def _in_even_kernel(x_ref, g_ref, sc_ref, sh_ref, w_ref, qng_ref, kvng_ref, wqb_ref, wqs_ref, wkn_ref, wv_ref,
                    qg_ref, qgs_ref, kg_ref, kgs_ref, ind_ref, spread_ref, cm_ref, sm_ref,
                    qk_ref, v_ref, o_ref, misc_ref, q_out, k_out, v_out):
    h = _norm_mod(x_ref[...], g_ref[...], sc_ref[0], sh_ref[0]).astype(BF16)
    qk_ref[...] = _dot(h, w_ref[:, 0:1024])
    v_ref[...] = _dot(h, w_ref[:, 1024:1536]).astype(BF16)
    o_ref[...] = _dot(h, w_ref[:, 1536:2048])
    rest = _dot(h, w_ref[:, 2048:EVEN_COLS])
    q_a = rest[:, 0:256]
    kv_a = rest[:, 256:384]
    krp = rest[:, 384:512]
    krs = rest[:, 512:640]
    misc_ref[...] = rest[:, 640:768]
    qn = (_rms(q_a) * qng_ref[...]).astype(BF16)
    kvn = (_rms(kv_a) * kvng_ref[...]).astype(BF16)
    q = _dot(qn, wqb_ref[...])
    qs = _dot(qn, wqs_ref[...])
    k = _dot(kvn, wkn_ref[...]) + jnp.concatenate([krp] * MLA_HEADS, axis=1)
    v_out[...] = _dot(kvn, wv_ref[...]).astype(BF16)
    q_rs = _head_rsqrt(q, ind_ref, spread_ref)
    k_rs = _head_rsqrt(k, ind_ref, spread_ref)
    cm, sm = cm_ref[...], sm_ref[...]
    q_c, q_s = qg_ref[...] * cm, qgs_ref[...] * sm
    k_c, k_s = kg_ref[...] * cm, kgs_ref[...] * sm
    ks = krs * k_s
    for hd in range(MLA_HEADS):
        sl = slice(hd * MLA_PAD, (hd + 1) * MLA_PAD)
        q_out[:, sl] = ((q[:, sl] * q_c + qs[:, sl] * q_s) * q_rs[:, sl]).astype(BF16)
        k_out[:, sl] = ((k[:, sl] * k_c + ks) * k_rs[:, sl]).astype(BF16)


def _in_even(x, mod, b_of, norm_g, w, qng, kvng, wqb, wqs, wkn, wv, qg, qgs, kg, kgs, ind, spread, cm, sm):
    n = x.shape[0]
    t = ROW_TILE
    full = lambda shape: pl.BlockSpec(shape, lambda i: (0,) * len(shape))
    row = lambda c: pl.BlockSpec((t, c), lambda i: (i, 0))
    out_shape = (
        jax.ShapeDtypeStruct((n, 1024), F32),
        jax.ShapeDtypeStruct((n, 512), BF16),
        jax.ShapeDtypeStruct((n, 512), F32),
        jax.ShapeDtypeStruct((n, LANES), F32),
        jax.ShapeDtypeStruct((n, MLA_W), BF16),
        jax.ShapeDtypeStruct((n, MLA_W), BF16),
        jax.ShapeDtypeStruct((n, 512), BF16),
    )
    vec = full((1, LANES))
    return pl.pallas_call(
        _in_even_kernel,
        out_shape=out_shape,
        grid=(n // t,),
        in_specs=[
            row(D), full((1, D)),
            pl.BlockSpec((1, 1, D), lambda i: (b_of(i, t) * 6 + 1, 0, 0)),
            pl.BlockSpec((1, 1, D), lambda i: (b_of(i, t) * 6 + 0, 0, 0)),
            full((D, EVEN_COLS)), full((1, 256)), full((1, 128)), full((256, MLA_W)), full((256, MLA_W)),
            full((128, MLA_W)), full((128, 512)), vec, vec, vec, vec, full((MLA_W, LANES)), full((LANES, MLA_W)),
            row(LANES), row(LANES),
        ],
        out_specs=(row(1024), row(512), row(512), row(LANES), row(MLA_W), row(MLA_W), row(512)),
        compiler_params=_params(("parallel",)),
        name="in_even",
    )(x, norm_g, mod, mod, w, qng, kvng, wqb, wqs, wkn, wv, qg, qgs, kg, kgs, ind, spread, cm, sm)


def _conv_kernel(seq_tiles, prev_ref, x_ref, next_ref, w_ref, b_ref, s_ref, o_ref, buf):
    i = pl.program_id(0)
    t = x_ref.shape[0]
    first = (i % seq_tiles) == 0
    last = (i % seq_tiles) == seq_tiles - 1
    buf[0:8, :] = jnp.where(first, 0.0, prev_ref[...])
    buf[8:8 + t, :] = x_ref[...]
    buf[8 + t:16 + t, :] = jnp.where(last, 0.0, next_ref[...])
    pad = ML_CONV // 2
    acc = b_ref[...] + w_ref[0:1, :] * buf[8 - pad:8 - pad + t, :]
    for k in range(1, ML_CONV):
        acc = acc + w_ref[k:k + 1, :] * buf[8 - pad + k:8 - pad + k + t, :]
    o_ref[...] = (acc * jax.nn.sigmoid(acc) * s_ref[...]).astype(BF16)


def _conv(qk_pre, conv_w, conv_b, seq):
    n, c = qk_pre.shape
    t = ROW_TILE
    nb8 = n // 8
    scale = jnp.concatenate([jnp.ones((512,), F32), jnp.full((512,), ML_DK ** -0.5, F32)]).reshape(1, c)
    return pl.pallas_call(
        functools.partial(_conv_kernel, seq // t),
        out_shape=jax.ShapeDtypeStruct((n, c), BF16),
        grid=(n // t,),
        in_specs=[
            pl.BlockSpec((8, c), lambda i: (jnp.maximum(i * (t // 8) - 1, 0), 0)),
            pl.BlockSpec((t, c), lambda i: (i, 0)),
            pl.BlockSpec((8, c), lambda i: (jnp.minimum((i + 1) * (t // 8), nb8 - 1), 0)),
            pl.BlockSpec((ML_CONV, c), lambda i: (0, 0)),
            pl.BlockSpec((1, c), lambda i: (0, 0)),
            pl.BlockSpec((1, c), lambda i: (0, 0)),
        ],
        out_specs=pl.BlockSpec((t, c), lambda i: (i, 0)),
        scratch_shapes=[pltpu.VMEM((t + 16, c), F32)],
        compiler_params=_params(("parallel",)),
        name="conv",
    )(qk_pre, qk_pre, qk_pre, conv_w, conv_b.reshape(1, c), scale)


def _mlstm_kernel(qf_ref, kf_ref, vf_ref, qb_ref, kb_ref, vb_ref, gcf_ref, gcb_ref, grf_ref, grb_ref,
                  bc_ref, br_ref, of_ref, ob_ref, c_ref, m_ref):
    c = pl.program_id(1)
    L = ML_CHUNK

    @pl.when(c == 0)
    def _():
        c_ref[...] = jnp.zeros_like(c_ref)
        m_ref[...] = jnp.zeros_like(m_ref)

    row = lax.broadcasted_iota(jnp.int32, (L, L), 0)
    col = lax.broadcasted_iota(jnp.int32, (L, L), 1)
    lane = lax.broadcasted_iota(jnp.int32, (L, ML_DV), 1)
    ones_col = jnp.where(lane == 0, 1.0, 0.0).astype(BF16)
    dirs = ((qf_ref, kf_ref, vf_ref, gcf_ref, grf_ref, of_ref, row >= col, row <= col),
            (qb_ref, kb_ref, vb_ref, gcb_ref, grb_ref, ob_ref, row <= col, row >= col))
    for d, (q_ref, k_ref, v_ref, gc_ref, gr_ref, o_ref, causal, causal_t) in enumerate(dirs):
        gc = gc_ref[0] + bc_ref[d]
        gr = gr_ref[0] + br_ref[d]
        i_col = gc[:, 0:ML_HEADS]
        f_col = _log_sigmoid(gc[:, ML_HEADS:2 * ML_HEADS])
        i_row = gr[0:ML_HEADS, :]
        f_row = _log_sigmoid(gr[ML_HEADS:2 * ML_HEADS, :])
        bcum_col = jnp.dot(jnp.where(causal, 1.0, 0.0), f_col, precision=HIGHEST, preferred_element_type=F32)
        bcum_row = jnp.dot(f_row, jnp.where(causal_t, 1.0, 0.0), precision=HIGHEST, preferred_element_type=F32)
        for h in range(ML_HEADS):
            sl = slice(h * ML_DK, (h + 1) * ML_DK)
            st = d * ML_HEADS + h
            q = q_ref[:, sl]
            k = k_ref[:, sl]
            vext = jnp.concatenate([v_ref[:, sl], ones_col], axis=1)
            bc = bcum_col[:, h:h + 1]
            br = bcum_row[h:h + 1, :]
            ir = i_row[h:h + 1, :]
            ic = i_col[:, h:h + 1]
            m_prev = m_ref[st:st + 1, 0:1]
            log_d = jnp.where(causal, bc - br + ir, -jnp.inf)
            m_inter = bc + m_prev
            m_t = jnp.maximum(m_inter, jnp.max(log_d, axis=-1, keepdims=True))
            w = (_dot_nt(q, k) * jnp.exp(log_d - m_t)).astype(BF16)
            a_inter = jnp.exp(m_inter - m_t)
            tot = _dot(w, vext) + a_inter * _dot(q, c_ref[st].astype(BF16))
            den = tot[:, ML_DV:ML_DV + 1]
            o_ref[:, sl] = tot[:, 0:ML_DV] / jnp.maximum(jnp.abs(den), jnp.exp(-m_t))
            total = jnp.sum(f_row[h:h + 1, :], axis=-1, keepdims=True)
            g_row = total - br + ir
            m_new = jnp.maximum(total + m_prev, jnp.max(g_row, axis=-1, keepdims=True))
            a_state = jnp.exp(total + m_prev - m_new)
            kw = (k.astype(F32) * jnp.exp(total - bc + ic - m_new)).astype(BF16)
            c_ref[st] = a_state * c_ref[st] + _dot_tn(kw, vext)
            m_ref[st:st + 1, :] = jnp.broadcast_to(m_new, (1, LANES))


def _mlstm(qk, v, gcol, grow, bcol, brow, batch, seq):
    n = qk.shape[0]
    L = ML_CHUNK
    nc = seq // L
    fw = lambda b, c: b * nc + c
    bw = lambda b, c: b * nc + nc - 1 - c
    out = jax.ShapeDtypeStruct((n, ML_HEADS * ML_DV), F32)
    return pl.pallas_call(
        _mlstm_kernel,
        out_shape=(out, out),
        grid=(batch, nc),
        in_specs=[
            pl.BlockSpec((L, 512), lambda b, c: (fw(b, c), 0)),
            pl.BlockSpec((L, 512), lambda b, c: (fw(b, c), 1)),
            pl.BlockSpec((L, 512), lambda b, c: (fw(b, c), 0)),
            pl.BlockSpec((L, 512), lambda b, c: (bw(b, c), 0)),
            pl.BlockSpec((L, 512), lambda b, c: (bw(b, c), 1)),
            pl.BlockSpec((L, 512), lambda b, c: (bw(b, c), 0)),
            pl.BlockSpec((1, L, 8), lambda b, c: (0, fw(b, c), 0)),
            pl.BlockSpec((1, L, 8), lambda b, c: (1, bw(b, c), 0)),
            pl.BlockSpec((1, 8, L), lambda b, c: (0, 0, fw(b, c))),
            pl.BlockSpec((1, 8, L), lambda b, c: (1, 0, bw(b, c))),
            pl.BlockSpec((2, 1, 8), lambda b, c: (0, 0, 0)),
            pl.BlockSpec((2, 8, 1), lambda b, c: (0, 0, 0)),
        ],
        out_specs=(pl.BlockSpec((L, 512), lambda b, c: (fw(b, c), 0)),
                   pl.BlockSpec((L, 512), lambda b, c: (bw(b, c), 0))),
        scratch_shapes=[pltpu.VMEM((2 * ML_HEADS, ML_DK, 2 * ML_DV), F32), pltpu.VMEM((8, LANES), F32)],
        compiler_params=_params(("parallel", "arbitrary")),
        name="mlstm",
    )(qk, qk, v, qk, qk, v, gcol, gcol, grow, grow, bcol, brow)


def _attn_kernel(q_ref, k_ref, v_ref, o_ref, vext):
    @pl.when(pl.program_id(2) == 0)
    def _():
        lane = lax.broadcasted_iota(jnp.int32, (v_ref.shape[0], MLA_PAD - MLA_V), 1)
        ones_col = jnp.where(lane == 0, 1.0, 0.0).astype(BF16)
        for j in range(2):
            vext[:, j * MLA_PAD:(j + 1) * MLA_PAD] = jnp.concatenate(
                [v_ref[:, j * MLA_V:(j + 1) * MLA_V], ones_col], axis=1)

    outs = []
    for j in range(2):
        q = q_ref[:, j * MLA_PAD:(j + 1) * MLA_PAD]
        k = k_ref[:, j * MLA_PAD:(j + 1) * MLA_PAD]
        s = _dot_nt(q, k)
        p = jnp.exp2(s - jnp.max(s, axis=-1, keepdims=True))
        oe = _dot(p.astype(BF16), vext[:, j * MLA_PAD:(j + 1) * MLA_PAD])
        outs.append(oe[:, 0:MLA_V] / oe[:, MLA_V:MLA_V + 1])
    o_ref[...] = jnp.concatenate(outs, axis=1).astype(BF16)


def _attn(q, k, v, batch, seq):
    n = q.shape[0]
    t = min(ATTN_Q_TILE, seq)
    nq = seq // t
    return pl.pallas_call(
        _attn_kernel,
        out_shape=jax.ShapeDtypeStruct((n, MLA_HEADS * MLA_V), BF16),
        grid=(batch, MLA_HEADS // 2, nq),
        in_specs=[
            pl.BlockSpec((t, 2 * MLA_PAD), lambda b, h, i: (b * nq + i, h)),
            pl.BlockSpec((seq, 2 * MLA_PAD), lambda b, h, i: (b, h)),
            pl.BlockSpec((seq, 2 * MLA_V), lambda b, h, i: (b, h)),
        ],
        out_specs=pl.BlockSpec((t, 2 * MLA_V), lambda b, h, i: (b * nq + i, h)),
        scratch_shapes=[pltpu.VMEM((seq, 2 * MLA_PAD), BF16)],
        compiler_params=_params(("parallel", "parallel", "arbitrary")),
        name="attn",
    )(q, k, v)


def _mixer_tail(y, x_ref, g1_ref, fg_ref, sc2_ref, sh2_ref, rwh_ref, rwl_ref, rb_ref, x_out, hf_out, lg_out):
    x1 = x_ref[...] + g1_ref[0] * y
    x_out[...] = x1
    hf = _norm_mod(x1, fg_ref[...], sc2_ref[0], sh2_ref[0])
    hf_out[...] = hf
    hi, lo = _split_bf16(hf)
    lg_out[...] = _dot(hi, rwh_ref[...]) + (_dot(lo, rwh_ref[...]) + _dot(hi, rwl_ref[...])) + rb_ref[...]


def _tail_specs(n, t, b_of):
    full = lambda shape: pl.BlockSpec(shape, lambda i: (0,) * len(shape))
    mod = lambda j: pl.BlockSpec((1, 1, D), lambda i: (b_of(i, t) * 6 + j, 0, 0))
    row = lambda c: pl.BlockSpec((t, c), lambda i: (i, 0))
    in_specs = [row(D), mod(2), full((1, D)), mod(4), mod(3), full((D, LANES)), full((D, LANES)), full((1, LANES))]
    out_shape = (jax.ShapeDtypeStruct((n, D), F32), jax.ShapeDtypeStruct((n, D), F32),
                 jax.ShapeDtypeStruct((n, LANES), F32))
    out_specs = (row(D), row(D), row(LANES))
    return in_specs, out_shape, out_specs


def _out_even_kernel(hf_ref, hb_ref, om_ref, oa_ref, mg_ref, wt_ref, wb_ref, *tail):
    hs = hf_ref[...] + hb_ref[...]
    gate = jax.nn.sigmoid(om_ref[...])
    parts = []
    for h in range(ML_HEADS):
        sl = slice(h * ML_DV, (h + 1) * ML_DV)
        parts.append(_rms(hs[:, sl]) * mg_ref[:, sl] * gate[:, sl])
    hml = jnp.concatenate(parts, axis=1).astype(BF16)
    y = _dot(hml, wt_ref[...]) + _dot(oa_ref[...], wb_ref[...])
    _mixer_tail(y, *tail)


def _out_even(h_f, h_b, o_m, o_mla, ml_g, w_top, w_bot, x, mod, b_of, ffn_g, rw, rb):
    n = x.shape[0]
    t = ROW_TILE
    full = lambda shape: pl.BlockSpec(shape, lambda i: (0,) * len(shape))
    row = lambda c: pl.BlockSpec((t, c), lambda i: (i, 0))
    tin, out_shape, out_specs = _tail_specs(n, t, b_of)
    return pl.pallas_call(
        _out_even_kernel,
        out_shape=out_shape,
        grid=(n // t,),
        in_specs=[row(512), row(512), row(512), row(512), full((1, 512)), full((512, D)), full((512, D))] + tin,
        out_specs=out_specs,
        compiler_params=_params(("parallel",)),
        name="out_even",
    )(h_f, h_b, o_m, o_mla, ml_g, w_top, w_bot, x, mod, ffn_g, mod, mod, *rw, rb)


def _in_odd_kernel(x_ref, g_ref, sc_ref, sh_ref, w_ref, cr_ref, sr_ref, q_out, k_out, v_out, g_out):
    h = _norm_mod(x_ref[...], g_ref[...], sc_ref[0], sh_ref[0]).astype(BF16)
    cos, sin = cr_ref[...], sr_ref[...]
    hw = RET_DK // 2
    for idx, (dst, scale) in enumerate(((q_out, RET_DK ** -0.5), (k_out, 1.0))):
        z = _dot(h, w_ref[:, idx * 1024:(idx + 1) * 1024])
        for hd in range(RET_HEADS):
            x1 = z[:, hd * RET_DK:hd * RET_DK + hw]
            x2 = z[:, hd * RET_DK + hw:(hd + 1) * RET_DK]
            dst[:, hd * RET_DK:hd * RET_DK + hw] = ((x1 * cos - x2 * sin) * scale).astype(BF16)
            dst[:, hd * RET_DK + hw:(hd + 1) * RET_DK] = ((x2 * cos + x1 * sin) * scale).astype(BF16)
    v_out[...] = _dot(h, w_ref[:, 2048:4096]).astype(BF16)
    g_out[...] = _dot(h, w_ref[:, 4096:6144])


def _in_odd(x, mod, b_of, norm_g, w, cr, sr):
    n = x.shape[0]
    t = ROW_TILE
    full = lambda shape: pl.BlockSpec(shape, lambda i: (0,) * len(shape))
    row = lambda c: pl.BlockSpec((t, c), lambda i: (i, 0))
    out_shape = (
        jax.ShapeDtypeStruct((n, 1024), BF16), jax.ShapeDtypeStruct((n, 1024), BF16),
        jax.ShapeDtypeStruct((n, 2048), BF16), jax.ShapeDtypeStruct((n, 2048), F32),
    )
    return pl.pallas_call(
        _in_odd_kernel,
        out_shape=out_shape,
        grid=(n // t,),
        in_specs=[
            row(D), full((1, D)),
            pl.BlockSpec((1, 1, D), lambda i: (b_of(i, t) * 6 + 1, 0, 0)),
            pl.BlockSpec((1, 1, D), lambda i: (b_of(i, t) * 6 + 0, 0, 0)),
            full((D, 6144)), row(LANES), row(LANES),
        ],
        out_specs=(row(1024), row(1024), row(2048), row(2048)),
        compiler_params=_params(("parallel",)),
        name="in_odd",
    )(x, norm_g, mod, mod, w, cr, sr)


def _ret_kernel(qf_ref, kf_ref, vf_ref, qb_ref, kb_ref, vb_ref, dec_ref, of_ref, ob_ref, s_ref):
    c = pl.program_id(1)
    L = RET_CHUNK

    @pl.when(c == 0)
    def _():
        s_ref[...] = jnp.zeros_like(s_ref)

    row = lax.broadcasted_iota(jnp.int32, (L, L), 0)
    col = lax.broadcasted_iota(jnp.int32, (L, L), 1)
    pos = lax.broadcasted_iota(jnp.int32, (L, 1), 0)
    dirs = ((qf_ref, kf_ref, vf_ref, of_ref, row - col, pos + 1, L - 1 - pos),
            (qb_ref, kb_ref, vb_ref, ob_ref, col - row, L - pos, pos))
    for d, (q_ref, k_ref, v_ref, o_ref, rel_i, exp_q_i, exp_k_i) in enumerate(dirs):
        rel = rel_i.astype(F32)
        exp_q = exp_q_i.astype(F32)
        exp_k = exp_k_i.astype(F32)
        for h in range(RET_HEADS):
            st = d * RET_HEADS + h
            lg = _log_sigmoid(dec_ref[d, h:h + 1, :])
            lg1 = lg[:, 0:1]
            decay_in = jnp.where(rel >= 0, jnp.exp(jnp.maximum(rel, 0.0) * lg1), 0.0)
            q = q_ref[:, h * RET_DK:(h + 1) * RET_DK]
            k = k_ref[:, h * RET_DK:(h + 1) * RET_DK]
            v = v_ref[:, h * RET_DV:(h + 1) * RET_DV]
            sc = (_dot_nt(q, k) * decay_in).astype(BF16)
            o_ref[:, h * RET_DV:(h + 1) * RET_DV] = (
                _dot(sc, v) + jnp.exp(exp_q * lg1) * _dot(q, s_ref[st].astype(BF16)))
            kd = (k.astype(F32) * jnp.exp(exp_k * lg1)).astype(BF16)
            s_ref[st] = jnp.exp(L * lg1) * s_ref[st] + _dot_tn(kd, v)


def _retention(q, k, v, dec, batch, seq):
    n = q.shape[0]
    L = RET_CHUNK
    nc = seq // L
    fw = lambda b, c: (b * nc + c, 0)
    bw = lambda b, c: (b * nc + nc - 1 - c, 0)
    out = jax.ShapeDtypeStruct((n, RET_HEADS * RET_DV), F32)
    return pl.pallas_call(
        _ret_kernel,
        out_shape=(out, out),
        grid=(batch, nc),
        in_specs=[
            pl.BlockSpec((L, 1024), fw), pl.BlockSpec((L, 1024), fw), pl.BlockSpec((L, 2048), fw),
            pl.BlockSpec((L, 1024), bw), pl.BlockSpec((L, 1024), bw), pl.BlockSpec((L, 2048), bw),
            pl.BlockSpec((2, RET_HEADS, LANES), lambda b, c: (0, 0, 0)),
        ],
        out_specs=(pl.BlockSpec((L, 2048), fw), pl.BlockSpec((L, 2048), bw)),
        scratch_shapes=[pltpu.VMEM((2 * RET_HEADS, RET_DK, RET_DV), F32)],
        compiler_params=_params(("parallel", "arbitrary")),
        name="retention",
    )(q, k, v, q, k, v, dec)


def _out_odd_kernel(yf_ref, yb_ref, g_ref, gn_ref, w_ref, *tail):
    ys = yf_ref[...] + yb_ref[...]
    g = g_ref[...]
    gate = g * jax.nn.sigmoid(g)
    parts = []
    for h in range(RET_HEADS):
        sl = slice(h * RET_DV, (h + 1) * RET_DV)
        yh = ys[:, sl]
        yc = yh - jnp.mean(yh, axis=-1, keepdims=True)
        var = jnp.mean(yc * yc, axis=-1, keepdims=True)
        parts.append(yc * lax.rsqrt(var + EPS) * gn_ref[:, sl] * gate[:, sl])
    yn = jnp.concatenate(parts, axis=1).astype(BF16)
    _mixer_tail(_dot(yn, w_ref[...]), *tail)


def _out_odd(y_f, y_b, g, gn_g, w_out, x, mod, b_of, ffn_g, rw, rb):
    n = x.shape[0]
    t = ROW_TILE
    full = lambda shape: pl.BlockSpec(shape, lambda i: (0,) * len(shape))
    row = lambda c: pl.BlockSpec((t, c), lambda i: (i, 0))
    tin, out_shape, out_specs = _tail_specs(n, t, b_of)
    return pl.pallas_call(
        _out_odd_kernel,
        out_shape=out_shape,
        grid=(n // t,),
        in_specs=[row(2048), row(2048), row(2048), full((1, 2048)), full((2048, D))] + tin,
        out_specs=out_specs,
        compiler_params=_params(("parallel",)),
        name="out_odd",
    )(y_f, y_b, g, gn_g, w_out, x, mod, ffn_g, mod, mod, *rw, rb)


def _route_kernel(lg_ref, e_out, r_out, w_out, cnt_out, base_ref):
    i = pl.program_id(0)
    t = lg_ref.shape[0]

    @pl.when(i == 0)
    def _():
        base_ref[...] = jnp.zeros_like(base_ref)

    l = lg_ref[...]
    lane = lax.broadcasted_iota(jnp.int32, l.shape, 1)
    sel = jnp.zeros(l.shape, F32)
    vals, idxs, hots = [], [], []
    for _k in range(TOP_K):
        m = jnp.max(l, axis=-1, keepdims=True)
        idx = jnp.min(jnp.where(l == m, lane, LANES), axis=-1, keepdims=True)
        hot = lane == idx
        vals.append(m)
        idxs.append(idx)
        hots.append(hot)
        sel = sel + jnp.where(hot, 1.0, 0.0)
        l = jnp.where(hot, -jnp.inf, l)
    es = [jnp.exp(v - vals[0]) for v in vals]
    den = es[0] + es[1] + es[2] + es[3]
    row = lax.broadcasted_iota(jnp.int32, (t, t), 0)
    col = lax.broadcasted_iota(jnp.int32, (t, t), 1)
    before = jnp.where(row > col, 1.0, 0.0).astype(BF16)
    rank_mat = _dot(before, sel.astype(BF16)) + base_ref[...]
    e_acc = jnp.zeros(l.shape, jnp.int32)
    r_acc = jnp.zeros(l.shape, jnp.int32)
    w_acc = jnp.zeros(l.shape, F32)
    for k in range(TOP_K):
        rk = jnp.sum(jnp.where(hots[k], rank_mat, 0.0), axis=-1, keepdims=True).astype(jnp.int32)
        e_acc = jnp.where(lane == k, idxs[k], e_acc)
        r_acc = jnp.where(lane == k, rk, r_acc)
        w_acc = jnp.where(lane == k, es[k] / den, w_acc)
    e_out[...] = e_acc
    r_out[...] = r_acc
    w_out[...] = w_acc
    base_ref[...] = base_ref[...] + jnp.sum(sel, axis=0, keepdims=True)
    cnt_out[...] = base_ref[...]


def _route(logits):
    n = logits.shape[0]
    t = ROUTE_TILE
    row = pl.BlockSpec((t, LANES), lambda i: (i, 0))
    return pl.pallas_call(
        _route_kernel,
        out_shape=(jax.ShapeDtypeStruct((n, LANES), jnp.int32), jax.ShapeDtypeStruct((n, LANES), jnp.int32),
                   jax.ShapeDtypeStruct((n, LANES), F32), jax.ShapeDtypeStruct((1, LANES), F32)),
        grid=(n // t,),
        in_specs=[row],
        out_specs=(row, row, row, pl.BlockSpec((1, LANES), lambda i: (0, 0))),
        scratch_shapes=[pltpu.VMEM((1, LANES), F32)],
        compiler_params=_params(("arbitrary",)),
        name="route",
    )(logits)


def _dispatch_kernel(dest_ref, pad_ref, hf_ref, xs_ref, sem):
    t = hf_ref.shape[0]

    def row_copy(src_row, dst_row):
        return pltpu.make_async_copy(hf_ref.at[pl.ds(src_row, 1)], xs_ref.at[pl.ds(dst_row, 1)], sem)

    def wait_rows():
        pltpu.make_async_copy(hf_ref, xs_ref.at[pl.ds(0, t)], sem).wait()

    def issue(g, carry):
        for u in range(MOVE_UNROLL):
            tok = g * MOVE_UNROLL + u
            for k in range(TOP_K):
                row_copy(tok, dest_ref[0, 0, tok * TOP_K + k]).start(priority=k % 2)
        return carry

    lax.fori_loop(0, t // MOVE_UNROLL, issue, 0)
    for _ in range(TOP_K):
        wait_rows()

    @pl.when(pl.program_id(0) == 0)
    def _():
        per_iter = MOVE_UNROLL * TOP_K
        n_free = pad_ref.shape[-1]

        def pad_issue(g, carry):
            for u in range(per_iter):
                row_copy(0, pad_ref[0, 0, g * per_iter + u]).start(priority=u % 2)
            return carry

        lax.fori_loop(0, n_free // per_iter, pad_issue, 0)
        for _ in range(n_free // t):
            wait_rows()


def _dispatch(hf, dest, pad_dest, rows):
    n = hf.shape[0]
    t = DISPATCH_TILE
    n_free = pad_dest.shape[0]
    assert n_free % t == 0 and t % MOVE_UNROLL == 0
    return pl.pallas_call(
        _dispatch_kernel,
        out_shape=jax.ShapeDtypeStruct((rows, D), F32),
        grid=(n // t,),
        in_specs=[
            pl.BlockSpec((1, 1, t * TOP_K), lambda i: (i, 0, 0), memory_space=pltpu.SMEM),
            pl.BlockSpec((1, 1, n_free), lambda i: (0, 0, 0), memory_space=pltpu.SMEM),
            pl.BlockSpec((t, D), lambda i: (i, 0)),
        ],
        out_specs=pl.BlockSpec(memory_space=pl.ANY),
        scratch_shapes=[pltpu.SemaphoreType.DMA(())],
        compiler_params=_params(("arbitrary",), has_side_effects=True),
        name="dispatch",
    )(dest.reshape(n // t, 1, t * TOP_K), pad_dest.reshape(1, 1, n_free), hf)


def _expert_kernel(be_ref, nu_ref, xs_ref, wgu_ref, bgu_ref, wd_ref, bd_ref, ys_ref, wgu_bf, wd_bf):
    i = pl.program_id(0)
    fresh = jnp.logical_or(i == 0, be_ref[i] != be_ref[jnp.maximum(i - 1, 0)])

    @pl.when(fresh)
    def _():
        wgu_bf[...] = wgu_ref[0, 0].astype(BF16)
        wd_bf[...] = wd_ref[0, 0].astype(BF16)

    @pl.when(i < nu_ref[0])
    def _():
        gu = _dot(xs_ref[...].astype(BF16), wgu_bf[...]) + bgu_ref[0, 0]
        gate = jnp.minimum(gu[:, :EXPERT_FF], SWIGLU_LIMIT)
        up = jnp.clip(gu[:, EXPERT_FF:], -SWIGLU_LIMIT, SWIGLU_LIMIT)
        act = (up + 1.0) * gate * jax.nn.sigmoid(SWIGLU_ALPHA * gate)
        ys_ref[...] = _dot(act.astype(BF16), wd_bf[...]) + bd_ref[0, 0]

    @pl.when(i >= nu_ref[0])
    def _():
        ys_ref[...] = jnp.zeros_like(ys_ref)


def _experts(xs, nb, layer, blk_expert, n_used, w_gu, b_gu, w_down, b_down):
    grid_spec = pltpu.PrefetchScalarGridSpec(
        num_scalar_prefetch=2,
        grid=(nb,),
        in_specs=[
            pl.BlockSpec((MOE_BLOCK, D), lambda i, be, nu: (jnp.minimum(i, nu[0] - 1), 0)),
            pl.BlockSpec((1, 1, D, 2 * EXPERT_FF), lambda i, be, nu: (layer, be[i], 0, 0)),
            pl.BlockSpec((1, 1, 1, 2 * EXPERT_FF), lambda i, be, nu: (layer, be[i], 0, 0)),
            pl.BlockSpec((1, 1, EXPERT_FF, D), lambda i, be, nu: (layer, be[i], 0, 0)),
            pl.BlockSpec((1, 1, 1, D), lambda i, be, nu: (layer, be[i], 0, 0)),
        ],
        out_specs=pl.BlockSpec((MOE_BLOCK, D), lambda i, be, nu: (i, 0)),
        scratch_shapes=[pltpu.VMEM((D, 2 * EXPERT_FF), BF16), pltpu.VMEM((EXPERT_FF, D), BF16)],
    )
    return pl.pallas_call(
        _expert_kernel,
        out_shape=jax.ShapeDtypeStruct((nb * MOE_BLOCK, D), F32),
        grid_spec=grid_spec,
        compiler_params=_params(("arbitrary",)),
        name="experts",
    )(blk_expert, n_used, xs, w_gu, b_gu.reshape(DEPTH, N_EXPERTS, 1, 2 * EXPERT_FF), w_down,
      b_down.reshape(DEPTH, N_EXPERTS, 1, D))


def _combine_kernel(dest0_ref, dest_ref, ys_ref, wt_ref, x_ref, g2_ref, o_ref, buf_a, buf_b, sem_a, sem_b):
    i = pl.program_id(0)
    t = x_ref.shape[0]

    def start_rows(idx_ref, tok, buf, sem):
        for k in range(TOP_K):
            pltpu.make_async_copy(ys_ref.at[pl.ds(idx_ref[0, 0, tok * TOP_K + k], 1)],
                                  buf.at[tok, pl.ds(k, 1)], sem).start(priority=k % 2)

    def wait_tile(buf, sem):
        for k in range(TOP_K):
            pltpu.make_async_copy(ys_ref.at[pl.ds(0, t)], buf.at[pl.ds(0, t), k], sem).wait()

    @pl.when(i == 0)
    def _():
        def issue(g, carry):
            for u in range(MOVE_UNROLL):
                start_rows(dest0_ref, g * MOVE_UNROLL + u, buf_a, sem_a)
            return carry
        lax.fori_loop(0, t // MOVE_UNROLL, issue, 0)
        wait_tile(buf_a, sem_a)

    g2 = g2_ref[0]

    def step(cur, nxt, sem_nxt):
        def body(g, carry):
            for u in range(MOVE_UNROLL):
                start_rows(dest_ref, g * MOVE_UNROLL + u, nxt, sem_nxt)
            rows = pl.ds(pl.multiple_of(g * MOVE_UNROLL, MOVE_UNROLL), MOVE_UNROLL)
            wt = wt_ref[rows, :]
            acc = cur[rows, 0, :] * wt[:, 0:1]
            for k in range(1, TOP_K):
                acc = acc + cur[rows, k, :] * wt[:, k:k + 1]
            o_ref[rows, :] = x_ref[rows, :] + g2 * acc
            return carry
        lax.fori_loop(0, t // MOVE_UNROLL, body, 0)
        wait_tile(nxt, sem_nxt)

    @pl.when(i % 2 == 0)
    def _():
        step(buf_a, buf_b, sem_b)

    @pl.when(i % 2 == 1)
    def _():
        step(buf_b, buf_a, sem_a)


def _combine(ys, dest, wts, x, mod, b_of):
    n = x.shape[0]
    t = MOVE_TILE
    nt = n // t
    dest3 = dest.reshape(nt, 1, t * TOP_K)
    idx = lambda f: pl.BlockSpec((1, 1, t * TOP_K), lambda i: (f(i), 0, 0), memory_space=pltpu.SMEM)
    return pl.pallas_call(
        _combine_kernel,
        out_shape=jax.ShapeDtypeStruct((n, D), F32),
        grid=(nt,),
        in_specs=[
            idx(lambda i: 0),
            idx(lambda i: jnp.minimum(i + 1, nt - 1)),
            pl.BlockSpec(memory_space=pl.ANY),
            pl.BlockSpec((t, LANES), lambda i: (i, 0)),
            pl.BlockSpec((t, D), lambda i: (i, 0)),
            pl.BlockSpec((1, 1, D), lambda i: (b_of(i, t) * 6 + 5, 0, 0)),
        ],
        out_specs=pl.BlockSpec((t, D), lambda i: (i, 0)),
        scratch_shapes=[pltpu.VMEM((t, TOP_K, D), F32), pltpu.VMEM((t, TOP_K, D), F32),
                        pltpu.SemaphoreType.DMA(()), pltpu.SemaphoreType.DMA(())],
        compiler_params=_params(("arbitrary",)),
        name="combine",
    )(dest3, dest3, ys, wts, x, mod)


def _moe(x1, hf, logits, mod, b_of, layer, w_gu, b_gu, w_down, b_down):
    n = x1.shape[0]
    eidx, rank, wts, counts = _route(logits)
    experts = jnp.arange(N_EXPERTS, dtype=jnp.int32)
    cnt = counts[0, :N_EXPERTS].astype(jnp.int32)
    padded = (cnt + MOE_BLOCK - 1) // MOE_BLOCK * MOE_BLOCK
    pend = jnp.cumsum(padded)
    pstart = pend - padded
    e4 = eidx[:, :TOP_K]
    dest = rank[:, :TOP_K] + jnp.sum(jnp.where(e4[:, :, None] == experts, pstart, 0), axis=-1)
    nb = -(-(n * TOP_K) // MOE_BLOCK) + N_EXPERTS
    blk_start = jnp.arange(nb, dtype=jnp.int32) * MOE_BLOCK
    blk_expert = jnp.minimum(jnp.sum((pend[None, :] <= blk_start[:, None]).astype(jnp.int32), axis=1),
                             N_EXPERTS - 1)
    n_used = (pend[-1:] // MOE_BLOCK).astype(jnp.int32)
    n_free = nb * MOE_BLOCK - n * TOP_K
    free_end = jnp.cumsum(padded - cnt)
    seg_first = jnp.concatenate([pstart + cnt, pend[-1:]])
    seg_skip = jnp.concatenate([free_end - (padded - cnt), free_end[-1:]])
    jj = jnp.arange(n_free, dtype=jnp.int32)
    seg = jnp.sum((free_end[None, :] <= jj[:, None]).astype(jnp.int32), axis=1)
    hot = seg[:, None] == jnp.arange(N_EXPERTS + 1, dtype=jnp.int32)
    pad_dest = jj + jnp.sum(jnp.where(hot, seg_first - seg_skip, 0), axis=1)
    xs = _dispatch(hf, dest, pad_dest, nb * MOE_BLOCK)
    ys = _experts(xs, nb, layer, blk_expert, n_used, w_gu, b_gu, w_down, b_down)
    return _combine(ys, dest, wts, x1, mod, b_of)


def _head_tile(t, partner):
    half = MLA_ROPE // 2
    lead = t.shape[:-1]
    tail = jnp.zeros(lead + (MLA_PAD - MLA_QK,), t.dtype)
    if not partner:
        return jnp.concatenate([t, tail], axis=-1)
    return jnp.concatenate([jnp.zeros(lead + (MLA_NOPE,), t.dtype), t[..., MLA_NOPE + half:],
                            t[..., MLA_NOPE:MLA_NOPE + half], tail], axis=-1)


def _prep_even(w_in, gate_b, w_qb, w_kvb, qk_q_g, qk_k_g, w_out):
    q_m, k_m, v_m, o_m, gates, q_a, kv_a, k_r = jnp.split(
        w_in, [512, 1024, 1536, 2048, 2064, 2320, 2448], axis=1)
    zeros = lambda c: jnp.zeros((D, c), F32)
    k_r96 = jnp.concatenate([zeros(MLA_NOPE), k_r], axis=1)
    misc = jnp.concatenate([gates, zeros(LANES - 16)], axis=1)
    w = jnp.concatenate([q_m, k_m, v_m, o_m, q_a, kv_a, _head_tile(k_r96, False), _head_tile(k_r96, True), misc],
                        axis=1).astype(BF16)
    qb = w_qb.reshape(MLA_Q_LORA, MLA_HEADS, MLA_QK)
    wqb = _head_tile(qb, False).reshape(MLA_Q_LORA, MLA_W).astype(BF16)
    wqs = _head_tile(qb, True).reshape(MLA_Q_LORA, MLA_W).astype(BF16)
    kvb = w_kvb.reshape(MLA_KV_LORA, MLA_HEADS, MLA_NOPE + MLA_V)
    wkn = jnp.pad(kvb[:, :, :MLA_NOPE], ((0, 0), (0, 0), (0, MLA_PAD - MLA_NOPE)))
    wkn = wkn.reshape(MLA_KV_LORA, MLA_W).astype(BF16)
    wv = kvb[:, :, MLA_NOPE:].reshape(MLA_KV_LORA, MLA_HEADS * MLA_V).astype(BF16)
    qg96 = qk_q_g * (MLA_QK ** -0.5 * LOG2_E)
    qg = _head_tile(qg96, False).reshape(1, MLA_PAD)
    qgs = _head_tile(qg96, True).reshape(1, MLA_PAD)
    kg = _head_tile(qk_k_g, False).reshape(1, MLA_PAD)
    kgs = _head_tile(qk_k_g, True).reshape(1, MLA_PAD)
    gb = gate_b.reshape(2, 8)
    w_top = w_out[:512].astype(BF16)
    w_bot = w_out[512:].astype(BF16)
    return w, wqb, wqs, wkn, wv, qg, qgs, kg, kgs, gb.reshape(2, 1, 8), gb.reshape(2, 8, 1), w_top, w_bot


def kernel(x, c, positions, ada_w, ada_b, norm_mix_g, norm_ffn_g, hy_w_in, ml_conv_w, ml_conv_b, ml_gate_b, ml_norm_g, mla_q_norm_g, mla_kv_norm_g, mla_w_qb, mla_w_kvb, mla_qk_q_g, mla_qk_k_g, hy_w_out, ret_w_in, ret_decay_f, ret_decay_b, ret_gn_g, ret_w_out, moe_router_w, moe_router_b, moe_w_gu, moe_b_gu, moe_w_down, moe_b_down):
    batch, seq, _ = x.shape
    n = batch * seq

    def b_of(i, t):
        return (i * t) // seq

    mod_all = _ada(c, ada_w, ada_b).reshape(DEPTH, batch * 6, 1, D)
    cr, sr, cm, sm = _rope_tables(positions)
    head_of_lane = jnp.arange(MLA_W, dtype=jnp.int32) // MLA_PAD
    ind = (head_of_lane[:, None] == jnp.arange(LANES, dtype=jnp.int32)[None, :]).astype(BF16)
    spread = ind.T
    xf = x.reshape(n, D)
    for layer in range(DEPTH):
        mod = mod_all[layer]
        j = layer // 2
        mix_g = norm_mix_g[layer].reshape(1, D)
        ffn_g = norm_ffn_g[layer].reshape(1, D)
        rw = _split_bf16(jnp.pad(moe_router_w[layer], ((0, 0), (0, LANES - N_EXPERTS))))
        rb = jnp.concatenate([moe_router_b[layer], jnp.full((LANES - N_EXPERTS,), NEG_BIG, F32)]).reshape(1, LANES)
        if layer % 2 == 0:
            w, wqb, wqs, wkn, wv, qg, qgs, kg, kgs, gbc, gbr, w_top, w_bot = _prep_even(
                hy_w_in[j], ml_gate_b[j], mla_w_qb[j], mla_w_kvb[j], mla_qk_q_g[j], mla_qk_k_g[j], hy_w_out[j])
            qk_pre, v_m, o_m, misc, q_a, k_a, v_a = _in_even(
                xf, mod, b_of, mix_g, w, mla_q_norm_g[j].reshape(1, -1), mla_kv_norm_g[j].reshape(1, -1),
                wqb, wqs, wkn, wv, qg, qgs, kg, kgs, ind, spread, cm, sm)
            qk = _conv(qk_pre, ml_conv_w[j], ml_conv_b[j], seq)
            gcol = misc[:, :16].reshape(n, 2, 8).transpose(1, 0, 2)
            grow = gcol.transpose(0, 2, 1)
            h_f, h_b = _mlstm(qk, v_m, gcol, grow, gbc, gbr, batch, seq)
            o_mla = _attn(q_a, k_a, v_a, batch, seq)
            x1, hf, logits = _out_even(h_f, h_b, o_m, o_mla, ml_norm_g[j].reshape(1, -1), w_top, w_bot,
                                       xf, mod, b_of, ffn_g, rw, rb)
        else:
            dec = jnp.broadcast_to(jnp.stack([ret_decay_f[j], ret_decay_b[j]])[:, :, None], (2, RET_HEADS, LANES))
            q_r, k_r, v_r, g_r = _in_odd(xf, mod, b_of, mix_g, ret_w_in[j].astype(BF16), cr, sr)
            y_f, y_b = _retention(q_r, k_r, v_r, dec.astype(F32), batch, seq)
            x1, hf, logits = _out_odd(y_f, y_b, g_r, ret_gn_g[j].reshape(1, -1), ret_w_out[j].astype(BF16),
                                      xf, mod, b_of, ffn_g, rw, rb)
        xf = _moe(x1, hf, logits, mod, b_of, layer, moe_w_gu, moe_b_gu, moe_w_down, moe_b_down)
    return xf.reshape(batch, seq, D)
```
